```python
import math
import jax, jax.numpy as jnp
from jax import lax
import numpy as np

D_MODEL = 1024
BATCH = 8
SEQ = 2048
DEPTH = 2

GRID_W = 64
ROPE_THETA = 10000.0
HEAD_DIM = 64
ATT_Q_HEADS = 8
ATT_KV_HEADS = 2
Q_BLOCK = 128
ATT_Q_W = ATT_Q_HEADS * HEAD_DIM
ATT_KV_W = ATT_KV_HEADS * HEAD_DIM
HY_WIDTH = 512
HY_ORDER = 2
HY_BANDS = 16
HY_EMB = 1 + 2 * HY_BANDS
HY_FFN = 64
HY_FAST_DECAY = 0.3
HY_SLOW_DECAY = 1.5
HY_TARGET = 1e-2
GLA_HEADS = 4
GLA_DK = 64
GLA_DV = 128
GLA_RANK = 16
GLA_NORMALIZER = 16.0
GLA_CHUNK = 64
GLA_K_W = GLA_HEADS * GLA_DK
GLA_V_W = GLA_HEADS * GLA_DV
MEM_LEN = 256
X_HEADS = 4
X_HEAD_DIM = D_MODEL // X_HEADS
D_FF = 4 * D_MODEL
N_BRANCH = 3
EPS = 1e-6
IN_SPLITS = (ATT_Q_W, ATT_KV_W, ATT_KV_W, 3 * HY_WIDTH, GLA_K_W, GLA_K_W, GLA_V_W, GLA_V_W,
             2 * GLA_RANK, N_BRANCH * D_MODEL)
N_IN = ATT_Q_W + 2 * ATT_KV_W + 3 * HY_WIDTH + 2 * GLA_K_W + 2 * GLA_V_W + 2 * GLA_RANK + N_BRANCH * D_MODEL

kernel_name = "hybrid_gqa_hyena_gla_gated_encoder"

F32 = jnp.float32


def rmsnorm(x, g):
    xf = x.astype(F32)
    y = xf * lax.rsqrt(jnp.mean(xf * xf, axis=-1, keepdims=True) + EPS)
    return (y * g.astype(F32)).astype(x.dtype)


def split_cols(u, sizes):
    idx = [int(i) for i in np.cumsum(sizes)[:-1]]
    return jnp.split(u, idx, axis=-1)


def axial_rope_angles(L):
    rows = L // GRID_W
    r, c = jnp.meshgrid(jnp.arange(rows), jnp.arange(GRID_W), indexing="ij")
    n_freq = HEAD_DIM // 4
    inv = ROPE_THETA ** (-jnp.arange(n_freq, dtype=F32) / n_freq)
    pos = jnp.stack([r.reshape(-1), c.reshape(-1)], axis=1).astype(F32)
    ang = pos[:, :, None] * inv
    return jnp.cos(ang), jnp.sin(ang)


def apply_rope(x, cos, sin):
    xs = x.reshape(x.shape[:-1] + (2, 2, HEAD_DIM // 4)).astype(F32)
    x1, x2 = xs[..., 0, :], xs[..., 1, :]
    out = jnp.stack([x1 * cos - x2 * sin, x1 * sin + x2 * cos], axis=-2)
    return out.reshape(x.shape).astype(x.dtype)


def attention_mixer(q, k, v, qn, kn, cos, sin):
    B, L, _ = q.shape
    G = ATT_Q_HEADS // ATT_KV_HEADS
    q = rmsnorm(q.reshape(B, L, ATT_Q_HEADS, HEAD_DIM), qn).transpose(0, 2, 1, 3)
    k = rmsnorm(k.reshape(B, L, ATT_KV_HEADS, HEAD_DIM), kn).transpose(0, 2, 1, 3)
    v = v.reshape(B, L, ATT_KV_HEADS, HEAD_DIM).transpose(0, 2, 1, 3)
    q = apply_rope(q, cos, sin).reshape(B, ATT_KV_HEADS, G, L, HEAD_DIM)
    k = apply_rope(k, cos, sin)
    nb = L // Q_BLOCK
    qb = jnp.moveaxis(q.reshape(B, ATT_KV_HEADS, G, nb, Q_BLOCK, HEAD_DIM), 3, 0)
    scale = HEAD_DIM ** -0.5

    def block(qblk):
        s = jnp.einsum("bkgqd,bktd->bkgqt", qblk, k).astype(F32) * scale
        p = jax.nn.softmax(s, axis=-1).astype(v.dtype)
        return jnp.einsum("bkgqt,bktd->bkgqd", p, v)

    o = lax.map(block, qb)
    o = jnp.moveaxis(o, 0, 3).reshape(B, ATT_Q_HEADS, L, HEAD_DIM)
    return o.transpose(0, 2, 1, 3).reshape(B, L, ATT_Q_W)


def short_conv(u, w):
    up = jnp.pad(u, ((0, 0), (1, 1), (0, 0)))
    return up[:, :-2] * w[0] + up[:, 1:-1] * w[1] + up[:, 2:] * w[2]


def hyena_pos_features(L):
    t = jnp.arange(L, dtype=F32)
    t_norm = t / max(L - 1, 1)
    w = 2.0 * math.pi * t / L
    f = jnp.linspace(1e-4, HY_BANDS - 1, HY_BANDS, dtype=F32)
    fw = w[:, None] * f
    z = jnp.concatenate([t_norm[:, None], jnp.cos(fw), -jnp.sin(fw)], axis=-1)
    deltas = jnp.abs(jnp.linspace(math.log(HY_TARGET) / HY_FAST_DECAY,
                                  math.log(HY_TARGET) / HY_SLOW_DECAY, HY_WIDTH, dtype=F32))
    window = jnp.exp(-t_norm[:, None] * deltas)
    return z, window


def hyena_filter_spectra(z, window, w1, b1, w2, b2, w3, b3, freq):
    L = z.shape[0]
    c = lambda a: a.astype(F32)
    h = jnp.sin(c(freq[0]) * (z @ c(w1) + c(b1)))
    h = jnp.sin(c(freq[1]) * (h @ c(w2) + c(b2)))
    h = (h @ c(w3) + c(b3)).reshape(L, HY_ORDER, 2, HY_WIDTH) * window[:, None, None, :]
    fwd, bwd = h[:, :, 0], h[:, :, 1]
    f = jnp.concatenate([fwd, jnp.zeros((1, HY_ORDER, HY_WIDTH), F32), bwd[1:][::-1]], axis=0)
    f = f / (jnp.sum(jnp.abs(f), axis=0, keepdims=True) + EPS)
    return jnp.fft.rfft(f, axis=0)


def long_conv(u, spec, skip):
    L = u.shape[1]
    uf = u.astype(F32)
    y = jnp.fft.irfft(jnp.fft.rfft(uf, n=2 * L, axis=1) * spec, n=2 * L, axis=1)[:, :L]
    return (y + uf * skip.astype(F32)).astype(u.dtype)


def hyena_mixer(u, conv_w, spec, skip):
    u = short_conv(u, conv_w)
    v, x1, x2 = jnp.split(u, 3, axis=-1)
    z = x1 * long_conv(v, spec[:, 0], skip[0])
    return x2 * long_conv(z, spec[:, 1], skip[1])


def gla_scan(q, k, v, g, include_diag):
    B, H, L, dk = q.shape
    dv = v.shape[-1]
    C = GLA_CHUNK
    N = L // C
    q, k, g = (a.reshape(B, H, N, C, dk) for a in (q, k, g))
    v = v.reshape(B, H, N, C, dv)
    b = jnp.cumsum(g, axis=3)
    b_last = b[:, :, :, -1:, :]
    q_dec = q * jnp.exp(b)
    a = jnp.einsum("bhncd,bhnsd->bhncs", q_dec, k * jnp.exp(-b))
    mask = jnp.tril(jnp.ones((C, C), dtype=bool), k=0 if include_diag else -1)
    o = jnp.einsum("bhncs,bhnse->bhnce", jnp.where(mask, a, 0.0), v)
    d_state = jnp.einsum("bhncd,bhnce->bhnde", k * jnp.exp(b_last - b), v)
    decay = jnp.exp(b_last[:, :, :, 0, :])

    def step(S, inp):
        dec, ds = inp
        return dec[..., None] * S + ds, S

    _, s_prev = lax.scan(step, jnp.zeros((B, H, dk, dv), F32),
                         (jnp.moveaxis(decay, 2, 0), jnp.moveaxis(d_state, 2, 0)))
    s_prev = jnp.moveaxis(s_prev, 0, 2)
    o = o + jnp.einsum("bhncd,bhnde->bhnce", q_dec, s_prev)
    return o.reshape(B, H, L, dv)


def gla_mixer(q, k, v, og, lr, w_lr, b_lr, onorm):
    B, L, _ = q.shape
    heads = lambda a, d: a.astype(F32).reshape(B, L, GLA_HEADS, d).transpose(0, 2, 1, 3)
    q = heads(q, GLA_DK) * (GLA_DK ** -0.5)
    k = heads(k, GLA_DK)
    v = heads(v, GLA_DV)
    logit = jnp.einsum("btsr,srd->btsd", lr.astype(F32).reshape(B, L, 2, GLA_RANK),
                       w_lr.astype(F32)) + b_lr.astype(F32)
    g = jax.nn.log_sigmoid(logit) / GLA_NORMALIZER
    g_f = heads(g[:, :, 0], GLA_DK)
    g_b = heads(g[:, :, 1], GLA_DK)
    fl = lambda a: jnp.flip(a, axis=2)
    o_f = gla_scan(q, k, v, g_f, True)
    o_b = fl(gla_scan(fl(q), fl(k), fl(v), fl(g_b), False))
    o = (o_f + o_b).transpose(0, 2, 1, 3)
    o = rmsnorm(o, onorm) * jax.nn.silu(og.astype(F32).reshape(B, L, GLA_HEADS, GLA_DV))
    return o.reshape(B, L, GLA_V_W).astype(og.dtype)


def memory_cross_attention(hn, mn, wq, wk, wv, wo, qn, kn):
    B, L, _ = hn.shape
    M = mn.shape[1]
    q = rmsnorm((hn @ wq).reshape(B, L, X_HEADS, X_HEAD_DIM), qn)
    k = rmsnorm((mn @ wk).reshape(B, M, X_HEADS, X_HEAD_DIM), kn)
    v = (mn @ wv).reshape(B, M, X_HEADS, X_HEAD_DIM)
    s = jnp.einsum("blhd,bmhd->bhlm", q, k).astype(F32) * (X_HEAD_DIM ** -0.5)
    p = jax.nn.softmax(s, axis=-1).astype(v.dtype)
    o = jnp.einsum("bhlm,bmhd->blhd", p, v).reshape(B, L, D_MODEL)
    return o @ wo


def setup_inputs(seed: int = 0) -> dict:
    key = jax.random.key(seed)
    ks = iter(jax.random.split(key, 40))

    def nrm(shape, s):
        return jax.random.normal(next(ks), shape, F32) * s

    def gain(shape):
        return 1.0 + nrm(shape, 0.02)

    Ld = DEPTH
    return {
        "x": nrm((BATCH, SEQ, D_MODEL), 1.0),
        "mem": nrm((BATCH, MEM_LEN, D_MODEL), 1.0),
        "ln_mix": gain((Ld, D_MODEL)),
        "w_in": nrm((Ld, D_MODEL, N_IN), D_MODEL ** -0.5),
        "attn_qnorm": gain((Ld, HEAD_DIM)),
        "attn_knorm": gain((Ld, HEAD_DIM)),
        "hy_conv": nrm((Ld, 3, 3 * HY_WIDTH), 3 ** -0.5),
        "hy_w1": nrm((Ld, HY_EMB, HY_FFN), HY_EMB ** -0.5),
        "hy_b1": nrm((Ld, HY_FFN), 0.1),
        "hy_w2": nrm((Ld, HY_FFN, HY_FFN), HY_FFN ** -0.5),
        "hy_b2": nrm((Ld, HY_FFN), 0.1),
        "hy_w3": nrm((Ld, HY_FFN, HY_ORDER * 2 * HY_WIDTH), HY_FFN ** -0.5),
        "hy_b3": nrm((Ld, HY_ORDER * 2 * HY_WIDTH), 0.1),
        "hy_freq": 1.0 + nrm((Ld, 2, HY_FFN), 0.1),
        "hy_skip": nrm((Ld, HY_ORDER, HY_WIDTH), 0.5),
        "gla_w_lr": nrm((Ld, 2, GLA_RANK, GLA_K_W), GLA_RANK ** -0.5),
        "gla_b_lr": nrm((Ld, 2, GLA_K_W), 0.1),
        "gla_onorm": gain((Ld, GLA_DV)),
        "w_br_attn": nrm((Ld, ATT_Q_W, D_MODEL), ATT_Q_W ** -0.5),
        "w_br_hyena": nrm((Ld, HY_WIDTH, D_MODEL), HY_WIDTH ** -0.5),
        "w_br_gla": nrm((Ld, GLA_V_W, D_MODEL), GLA_V_W ** -0.5),
        "w_out": nrm((Ld, D_MODEL, D_MODEL), D_MODEL ** -0.5),
        "ln_x": gain((Ld, D_MODEL)),
        "ln_mem": gain((Ld, D_MODEL)),
        "x_wq": nrm((Ld, D_MODEL, D_MODEL), D_MODEL ** -0.5),
        "x_wk": nrm((Ld, D_MODEL, D_MODEL), D_MODEL ** -0.5),
        "x_wv": nrm((Ld, D_MODEL, D_MODEL), D_MODEL ** -0.5),
        "x_wo": nrm((Ld, D_MODEL, D_MODEL), D_MODEL ** -0.5),
        "x_qnorm": gain((Ld, X_HEAD_DIM)),
        "x_knorm": gain((Ld, X_HEAD_DIM)),
        "ln_mlp": gain((Ld, D_MODEL)),
        "mlp_w1": nrm((Ld, D_MODEL, D_FF), D_MODEL ** -0.5),
        "mlp_w2": nrm((Ld, D_FF, D_MODEL), D_FF ** -0.5),
    }


def reference(x, mem, ln_mix, w_in, attn_qnorm, attn_knorm, hy_conv, hy_w1, hy_b1, hy_w2, hy_b2,
              hy_w3, hy_b3, hy_freq, hy_skip, gla_w_lr, gla_b_lr, gla_onorm, w_br_attn, w_br_hyena,
              w_br_gla, w_out, ln_x, ln_mem, x_wq, x_wk, x_wv, x_wo, x_qnorm, x_knorm, ln_mlp,
              mlp_w1, mlp_w2):
    B, L, _ = x.shape
    cos, sin = axial_rope_angles(L)
    z_pos, window = hyena_pos_features(L)
    h = x
    for i in range(DEPTH):
        hn = rmsnorm(h, ln_mix[i])
        (aq, ak, av, hy, gq, gk, gv, go, glr, gates) = split_cols(hn @ w_in[i], IN_SPLITS)
        y_a = attention_mixer(aq, ak, av, attn_qnorm[i], attn_knorm[i], cos, sin) @ w_br_attn[i]
        spec = hyena_filter_spectra(z_pos, window, hy_w1[i], hy_b1[i], hy_w2[i], hy_b2[i],
                                    hy_w3[i], hy_b3[i], hy_freq[i])
        y_b = hyena_mixer(hy, hy_conv[i], spec, hy_skip[i]) @ w_br_hyena[i]
        y_c = gla_mixer(gq, gk, gv, go, glr, gla_w_lr[i], gla_b_lr[i], gla_onorm[i]) @ w_br_gla[i]
        gate = jax.nn.sigmoid(gates.reshape(B, L, N_BRANCH, D_MODEL))
        mixed = gate[:, :, 0] * y_a + gate[:, :, 1] * y_b + gate[:, :, 2] * y_c
        h = h + mixed @ w_out[i]
        h = h + memory_cross_attention(rmsnorm(h, ln_x[i]), rmsnorm(mem, ln_mem[i]),
                                       x_wq[i], x_wk[i], x_wv[i], x_wo[i], x_qnorm[i], x_knorm[i])
        hn = rmsnorm(h, ln_mlp[i])
        h = h + jnp.square(jax.nn.relu(hn @ mlp_w1[i])) @ mlp_w2[i]
    return h
```

```python
import functools
import math

import jax
import jax.numpy as jnp
from jax import lax
from jax.experimental import pallas as pl
from jax.experimental.pallas import tpu as pltpu

F32 = jnp.float32
BF16 = jnp.bfloat16

D_MODEL = 1024
GRID_W = 64
ROPE_THETA = 10000.0
HEAD_DIM = 64
ATT_Q_HEADS = 8
ATT_KV_HEADS = 2
ATT_Q_W = ATT_Q_HEADS * HEAD_DIM
ATT_KV_W = ATT_KV_HEADS * HEAD_DIM
HY_WIDTH = 512
HY_ORDER = 2
HY_BANDS = 16
HY_EMB = 1 + 2 * HY_BANDS
HY_FFN = 64
HY_FAST_DECAY = 0.3
HY_SLOW_DECAY = 1.5
HY_TARGET = 1e-2
GLA_HEADS = 4
GLA_DK = 64
GLA_DV = 128
GLA_RANK = 16
GLA_NORMALIZER = 16.0
GLA_CHUNK = 64
GLA_K_W = GLA_HEADS * GLA_DK
GLA_V_W = GLA_HEADS * GLA_DV
X_HEADS = 4
X_HEAD_DIM = D_MODEL // X_HEADS
D_FF = 4 * D_MODEL
N_BRANCH = 3
EPS = 1e-6

COL_AQ = 0
COL_HY = 512
COL_GQ = 2048
COL_GK = 2304
COL_GV = 2560
COL_GO = 3072
COL_AK = 3584
COL_AV = 3712
COL_LR = 3840
COL_GATE = 4096
N_PACK = 7168

VMEM_LIMIT_BYTES = 52 * 1024 * 1024


def _cparams(*sem):
    return pltpu.CompilerParams(dimension_semantics=sem, vmem_limit_bytes=VMEM_LIMIT_BYTES)


def _tile(n, t):
    t = min(n, t)
    assert n % t == 0, (n, t)
    return t


def _dot(a, b):
    return jnp.dot(a, b, preferred_element_type=F32)


def _dot_nt(a, b):
    return lax.dot_general(a, b, (((1,), (1,)), ((), ())), preferred_element_type=F32)


def _dot_tn(a, b):
    return lax.dot_general(a, b, (((0,), (0,)), ((), ())), preferred_element_type=F32)


def _sigmoid(x):
    return 1.0 / (1.0 + jnp.exp(-x))


def _norm_mm_kernel(x_ref, g_ref, w_ref, o_ref, xn_ref):
    @pl.when(pl.program_id(1) == 0)
    def _():
        x = x_ref[...]
        ms = jnp.mean(x * x, axis=-1, keepdims=True)
        xn_ref[...] = (x * lax.rsqrt(ms + EPS) * g_ref[...]).astype(BF16)

    o_ref[...] = _dot(xn_ref[...], w_ref[...]).astype(o_ref.dtype)


def norm_matmul(x, g, w, out_dtype, tm=512, tn=1024):
    M, K = x.shape
    N = w.shape[1]
    tm, tn = _tile(M, tm), _tile(N, tn)
    return pl.pallas_call(
        _norm_mm_kernel,
        grid=(M // tm, N // tn),
        in_specs=[
            pl.BlockSpec((tm, K), lambda i, j: (i, 0)),
            pl.BlockSpec((1, K), lambda i, j: (0, 0)),
            pl.BlockSpec((K, tn), lambda i, j: (0, j)),
        ],
        out_specs=pl.BlockSpec((tm, tn), lambda i, j: (i, j)),
        out_shape=jax.ShapeDtypeStruct((M, N), out_dtype),
        scratch_shapes=[pltpu.VMEM((tm, K), BF16)],
        compiler_params=_cparams("parallel", "arbitrary"),
        name="norm_matmul",
    )(x, g.reshape(1, K), w)


def _group_mean_sq(x, gm_ref):
    return _dot((x * x).astype(BF16), gm_ref[...])


def _rope(x, c_ref, sa_ref, sb_ref):
    w = x.shape[-1]
    return (x * c_ref[...] + pltpu.roll(x, w - HEAD_DIM // 4, 1) * sa_ref[...]
            + pltpu.roll(x, HEAD_DIM // 4, 1) * sb_ref[...])


def _attn_kernel(q_ref, k_ref, v_ref, qn_ref, kn_ref, cq_ref, saq_ref, sbq_ref,
                 ck_ref, sak_ref, sbk_ref, gmq_ref, gmk_ref, o_ref, kp_ref):
    @pl.when(pl.program_id(1) == 0)
    def _():
        k = k_ref[0].astype(F32)
        kh = k * lax.rsqrt(_group_mean_sq(k, gmk_ref) + EPS) * kn_ref[...]
        kp_ref[...] = _rope(kh, ck_ref, sak_ref, sbk_ref).astype(BF16)

    q = q_ref[0].astype(F32)
    qh = q * lax.rsqrt(_group_mean_sq(q, gmq_ref) + EPS) * qn_ref[...]
    qb = (_rope(qh, cq_ref, saq_ref, sbq_ref) * (HEAD_DIM ** -0.5)).astype(BF16)
    group = ATT_Q_HEADS // ATT_KV_HEADS
    outs = []
    for h in range(ATT_Q_HEADS):
        kv = h // group
        kk = kp_ref[:, kv * HEAD_DIM:(kv + 1) * HEAD_DIM]
        vv = v_ref[0, :, kv * HEAD_DIM:(kv + 1) * HEAD_DIM]
        s = _dot_nt(qb[:, h * HEAD_DIM:(h + 1) * HEAD_DIM], kk)
        p = jnp.exp(s - jnp.max(s, axis=-1, keepdims=True))
        l = jnp.sum(p, axis=-1, keepdims=True)
        outs.append(_dot(p.astype(BF16), vv) / l)
    o_ref[0] = jnp.concatenate(outs, axis=-1).astype(o_ref.dtype)


def _rope_tables(L):
    rows = L // GRID_W
    r, c = jnp.meshgrid(jnp.arange(rows), jnp.arange(GRID_W), indexing="ij")
    n_freq = HEAD_DIM // 4
    inv = ROPE_THETA ** (-jnp.arange(n_freq, dtype=F32) / n_freq)
    pos = jnp.stack([r.reshape(-1), c.reshape(-1)], axis=1).astype(F32)
    ang = pos[:, :, None] * inv
    cos, sin = jnp.cos(ang), jnp.sin(ang)
    zero = jnp.zeros_like(sin)
    c64 = jnp.concatenate([cos, cos], axis=-1).reshape(L, HEAD_DIM)
    sa64 = jnp.concatenate([-sin, zero], axis=-1).reshape(L, HEAD_DIM)
    sb64 = jnp.concatenate([zero, sin], axis=-1).reshape(L, HEAD_DIM)
    return c64, sa64, sb64


def attention(p3, qn, kn, tabs, tq=256):
    B, L, _ = p3.shape
    tq = _tile(L, tq)
    c64, sa64, sb64 = tabs
    tq_tabs = [jnp.tile(t, (1, ATT_Q_HEADS)) for t in (c64, sa64, sb64)]
    tk_tabs = [jnp.tile(t, (1, ATT_KV_HEADS)) for t in (c64, sa64, sb64)]
    qn_t = jnp.tile(qn.astype(F32), ATT_Q_HEADS).reshape(1, ATT_Q_W)
    kn_t = jnp.tile(kn.astype(F32), ATT_KV_HEADS).reshape(1, ATT_KV_W)

    def group_mean(width):
        g = jnp.arange(width) // HEAD_DIM
        return jnp.where(g[:, None] == g[None, :], 1.0 / HEAD_DIM, 0.0).astype(BF16)

    qtab = pl.BlockSpec((tq, ATT_Q_W), lambda b, i: (i, 0))
    ktab = pl.BlockSpec((L, ATT_KV_W), lambda b, i: (0, 0))
    return pl.pallas_call(
        _attn_kernel,
        grid=(B, L // tq),
        in_specs=[
            pl.BlockSpec((1, tq, ATT_Q_W), lambda b, i: (b, i, COL_AQ // ATT_Q_W)),
            pl.BlockSpec((1, L, ATT_KV_W), lambda b, i: (b, 0, COL_AK // ATT_KV_W)),
            pl.BlockSpec((1, L, ATT_KV_W), lambda b, i: (b, 0, COL_AV // ATT_KV_W)),
            pl.BlockSpec((1, ATT_Q_W), lambda b, i: (0, 0)),
            pl.BlockSpec((1, ATT_KV_W), lambda b, i: (0, 0)),
            qtab, qtab, qtab, ktab, ktab, ktab,
            pl.BlockSpec((ATT_Q_W, ATT_Q_W), lambda b, i: (0, 0)),
            pl.BlockSpec((ATT_KV_W, ATT_KV_W), lambda b, i: (0, 0)),
        ],
        out_specs=pl.BlockSpec((1, tq, ATT_Q_W), lambda b, i: (b, i, 0)),
        out_shape=jax.ShapeDtypeStruct((B, L, ATT_Q_W), BF16),
        scratch_shapes=[pltpu.VMEM((L, ATT_KV_W), BF16)],
        compiler_params=_cparams("parallel", "arbitrary"),
        name="gqa_attention",
    )(p3, p3, p3, qn_t, kn_t, *tq_tabs, *tk_tabs, group_mean(ATT_Q_W), group_mean(ATT_KV_W))


def _dft_tables(L):
    n = 2 * L
    k = jnp.arange(L, dtype=jnp.int32)[:, None]
    j = jnp.arange(L, dtype=jnp.int32)[None, :]
    ang = ((k * j) % n).astype(F32) * (2.0 * math.pi / n)
    c = jnp.cos(ang)
    s = -jnp.sin(ang)
    alt = jnp.where(j % 2 == 0, 1.0, -1.0).astype(F32)
    bottom = jnp.where(k == 0, alt, s)
    fm = jnp.stack([c, bottom]).astype(BF16)
    g = jnp.concatenate([c.T, bottom.T], axis=1).astype(BF16)
    return fm, g


def _hyena_pos_features(L):
    t = jnp.arange(L, dtype=F32)
    t_norm = t / max(L - 1, 1)
    w = 2.0 * math.pi * t / L
    f = jnp.linspace(1e-4, HY_BANDS - 1, HY_BANDS, dtype=F32)
    fw = w[:, None] * f
    z = jnp.concatenate([t_norm[:, None], jnp.cos(fw), -jnp.sin(fw)], axis=-1)
    deltas = jnp.abs(jnp.linspace(math.log(HY_TARGET) / HY_FAST_DECAY,
                                  math.log(HY_TARGET) / HY_SLOW_DECAY, HY_WIDTH, dtype=F32))
    window = jnp.exp(-t_norm[:, None] * deltas)
    return z, window


def _sconv_kernel(u_ref, w_ref, o_ref):
    u = u_ref[0].astype(F32)
    L = u.shape[0]
    row = lax.broadcasted_iota(jnp.int32, u.shape, 0)
    prev = jnp.where(row == 0, 0.0, pltpu.roll(u, 1, 0))
    nxt = jnp.where(row == L - 1, 0.0, pltpu.roll(u, L - 1, 0))
    o_ref[0] = (prev * w_ref[0:1, :] + u * w_ref[1:2, :] + nxt * w_ref[2:3, :]).astype(o_ref.dtype)


def short_conv(p3, w):
    B, L, _ = p3.shape
    nblk = 3
    return pl.pallas_call(
        _sconv_kernel,
        grid=(B, nblk),
        in_specs=[
            pl.BlockSpec((1, L, HY_WIDTH), lambda b, c: (b, 0, COL_HY // HY_WIDTH + c)),
            pl.BlockSpec((3, HY_WIDTH), lambda b, c: (0, c)),
        ],
        out_specs=pl.BlockSpec((1, L, HY_WIDTH), lambda b, c: (b, 0, c)),
        out_shape=jax.ShapeDtypeStruct((B, L, nblk * HY_WIDTH), BF16),
        compiler_params=_cparams("parallel", "parallel"),
        name="hyena_short_conv",
    )(p3, w)


def _hyfilter_kernel(z_ref, w1_ref, b1_ref, w2_ref, b2_ref, w3_ref, b3_ref, fr_ref, win_ref,
                     a_ref, bm_ref):
    hp = lax.Precision.HIGHEST
    h = jnp.sin(fr_ref[0:1, :] * (jnp.dot(z_ref[...], w1_ref[...], precision=hp,
                                          preferred_element_type=F32) + b1_ref[...]))
    h = jnp.sin(fr_ref[1:2, :] * (jnp.dot(h, w2_ref[...], precision=hp,
                                          preferred_element_type=F32) + b2_ref[...]))
    hf = jnp.dot(h, w3_ref[...], precision=hp, preferred_element_type=F32) + b3_ref[...]
    win = win_ref[...]
    fwd = hf[:, :HY_WIDTH] * win
    bwd = hf[:, HY_WIDTH:] * win
    row = lax.broadcasted_iota(jnp.int32, bwd.shape, 0)
    bwd = jnp.where(row == 0, 0.0, bwd)
    norm = (jnp.sum(jnp.abs(fwd), axis=0, keepdims=True)
            + jnp.sum(jnp.abs(bwd), axis=0, keepdims=True) + EPS)
    a_ref[...] = ((fwd + bwd) / norm).astype(a_ref.dtype)
    bm_ref[...] = ((fwd - bwd) / norm).astype(bm_ref.dtype)


def hyena_filters(z_pad, window, w1p, b1, w2, b2, w3, b3, freq):
    L = z_pad.shape[0]
    zp = z_pad.shape[1]
    full = lambda shape: pl.BlockSpec(shape, lambda o: (0,) * len(shape))
    out = jax.ShapeDtypeStruct((L, HY_ORDER * HY_WIDTH), BF16)
    return pl.pallas_call(
        _hyfilter_kernel,
        grid=(HY_ORDER,),
        in_specs=[
            full((L, zp)), full((zp, HY_FFN)), full((1, HY_FFN)),
            full((HY_FFN, HY_FFN)), full((1, HY_FFN)),
            pl.BlockSpec((HY_FFN, 2 * HY_WIDTH), lambda o: (0, o)),
            pl.BlockSpec((1, 2 * HY_WIDTH), lambda o: (0, o)),
            full((2, HY_FFN)), full((L, HY_WIDTH)),
        ],
        out_specs=[pl.BlockSpec((L, HY_WIDTH), lambda o: (0, o)),
                   pl.BlockSpec((L, HY_WIDTH), lambda o: (0, o))],
        out_shape=[out, out],
        compiler_params=_cparams("parallel"),
        name="hyena_filter_mlp",
    )(z_pad, w1p, b1.reshape(1, -1), w2, b2.reshape(1, -1), w3, b3.reshape(1, -1), freq, window)


def _spec_kernel(fm_ref, a_ref, bm_ref, h_ref, *, n):
    th = fm_ref.shape[1]
    a = a_ref[...]
    top = _dot(fm_ref[0], a)
    bot = _dot(fm_ref[1], bm_ref[...])
    nyq = _dot(fm_ref[1, 0:8, :], a)[0:1, :]
    row = lax.broadcasted_iota(jnp.int32, top.shape, 0) + pl.program_id(0) * th
    first = row == 0
    scale = jnp.where(first, 1.0 / n, 2.0 / n)
    h_ref[0] = top * scale
    h_ref[1] = jnp.where(first, nyq, bot) * scale


def hyena_spectra(fm, a, bm, th=256):
    L = fm.shape[1]
    W = a.shape[1]
    th = _tile(L, th)
    return pl.pallas_call(
        functools.partial(_spec_kernel, n=2 * L),
        grid=(L // th,),
        in_specs=[
            pl.BlockSpec((2, th, L), lambda i: (0, i, 0)),
            pl.BlockSpec((L, W), lambda i: (0, 0)),
            pl.BlockSpec((L, W), lambda i: (0, 0)),
        ],
        out_specs=pl.BlockSpec((2, th, W), lambda i: (0, i, 0)),
        out_shape=jax.ShapeDtypeStruct((2, L, W), F32),
        compiler_params=_cparams("parallel"),
        name="hyena_filter_spectrum",
    )(fm, a, bm)


def _dft_fwd_kernel(fm_ref, u_ref, h_ref, y_ref):
    th = fm_ref.shape[1]
    u = u_ref[0]
    xt = _dot(fm_ref[0], u)
    xb = _dot(fm_ref[1], u)
    ht = h_ref[0]
    hb = h_ref[1]
    row = lax.broadcasted_iota(jnp.int32, xt.shape, 0) + pl.program_id(0) * th
    first = row == 0
    bb = xb * hb
    y_ref[0, 0] = (xt * ht - jnp.where(first, 0.0, bb)).astype(y_ref.dtype)
    y_ref[0, 1] = jnp.where(first, bb, xt * hb + xb * ht).astype(y_ref.dtype)


def dft_forward(fm, u, u_col, h, order, th=512):
    B, L, _ = u.shape
    th = _tile(L, th)
    return pl.pallas_call(
        _dft_fwd_kernel,
        grid=(L // th, B),
        in_specs=[
            pl.BlockSpec((2, th, L), lambda i, b: (0, i, 0)),
            pl.BlockSpec((1, L, HY_WIDTH), lambda i, b: (b, 0, u_col)),
            pl.BlockSpec((2, th, HY_WIDTH), lambda i, b: (0, i, order)),
        ],
        out_specs=pl.BlockSpec((1, 2, th, HY_WIDTH), lambda i, b: (b, 0, i, 0)),
        out_shape=jax.ShapeDtypeStruct((B, 2, L, HY_WIDTH), BF16),
        compiler_params=_cparams("parallel", "parallel"),
        name="hyena_dft_forward",
    )(fm, u, h)


def _dft_inv_kernel(g_ref, y_ref, u_ref, gate_ref, skip_ref, o_ref):
    L = y_ref.shape[2]
    y = _dot(g_ref[:, :L], y_ref[0, 0]) + _dot(g_ref[:, L:], y_ref[0, 1])
    u = u_ref[0].astype(F32)
    o_ref[0] = ((y + u * skip_ref[...]) * gate_ref[0].astype(F32)).astype(o_ref.dtype)


def dft_inverse(g, y, u, u_col, gate, gate_col, skip, tt=512):
    B, _, L, _ = y.shape
    tt = _tile(L, tt)
    return pl.pallas_call(
        _dft_inv_kernel,
        grid=(L // tt, B),
        in_specs=[
            pl.BlockSpec((tt, 2 * L), lambda i, b: (i, 0)),
            pl.BlockSpec((1, 2, L, HY_WIDTH), lambda i, b: (b, 0, 0, 0)),
            pl.BlockSpec((1, tt, HY_WIDTH), lambda i, b: (b, i, u_col)),
            pl.BlockSpec((1, tt, HY_WIDTH), lambda i, b: (b, i, gate_col)),
            pl.BlockSpec((1, HY_WIDTH), lambda i, b: (0, 0)),
        ],
        out_specs=pl.BlockSpec((1, tt, HY_WIDTH), lambda i, b: (b, i, 0)),
        out_shape=jax.ShapeDtypeStruct((B, L, HY_WIDTH), BF16),
        compiler_params=_cparams("parallel", "parallel"),
        name="hyena_dft_inverse",
    )(g, y, u, gate, skip.reshape(1, HY_WIDTH).astype(F32))


def hyena_mixer(p3, conv_w, spec, skip, fm, g):
    uc = short_conv(p3, conv_w.astype(F32))
    y0 = dft_forward(fm, uc, 0, spec, 0)
    z = dft_inverse(g, y0, uc, 0, uc, 1, skip[0])
    y1 = dft_forward(fm, z, 0, spec, 1)
    return dft_inverse(g, y1, z, 0, uc, 2, skip[1])


def _log_sigmoid(x):
    return -(jnp.maximum(-x, 0.0) + jnp.log1p(jnp.exp(-jnp.abs(x))))


def _gla_kernel(q_ref, k_ref, v_ref, og_ref, lr_ref, wlr_ref, blr_ref, on_ref, o_ref,
                g_ref, acc_ref):
    L = q_ref.shape[1]
    C = GLA_CHUNK
    n_chunks = L // C
    KW, VW = GLA_K_W, GLA_V_W

    logit = _dot(lr_ref[0], wlr_ref[...]) + blr_ref[...]
    g_ref[...] = _log_sigmoid(logit) * (1.0 / GLA_NORMALIZER)

    tok_r = lax.broadcasted_iota(jnp.int32, (C, C), 0)
    tok_c = lax.broadcasted_iota(jnp.int32, (C, C), 1)
    lane_head = lax.broadcasted_iota(jnp.int32, (C, KW), 1) // GLA_DK
    st_lane_head = lax.broadcasted_iota(jnp.int32, (GLA_DV, KW), 1) // GLA_DK
    arow = lax.broadcasted_iota(jnp.int32, (GLA_HEADS * C, C), 0) % C
    acol = lax.broadcasted_iota(jnp.int32, (GLA_HEADS * C, C), 1)

    def run(direction):
        fwd = direction == 0
        cum = jnp.where(tok_c <= tok_r, 1.0, 0.0) if fwd else jnp.where(tok_c >= tok_r, 1.0, 0.0)
        amask = (acol <= arow) if fwd else (acol > arow)
        col0 = direction * KW

        def body(it, st):
            n = it if fwd else n_chunks - 1 - it
            r0 = pl.multiple_of(n * C, C)
            g = g_ref[pl.ds(r0, C), col0:col0 + KW]
            q = q_ref[0, pl.ds(r0, C), :].astype(F32) * (GLA_DK ** -0.5)
            k = k_ref[0, pl.ds(r0, C), :].astype(F32)
            v = v_ref[0, pl.ds(r0, C), :]
            b = jnp.dot(cum, g, precision=lax.Precision.HIGHEST, preferred_element_type=F32)
            tot = jnp.sum(g, axis=0, keepdims=True)
            qd = q * jnp.exp(b)
            ki = (k * jnp.exp(-b)).astype(BF16)
            ks = (k * jnp.exp(tot - b)).astype(BF16)
            qs = jnp.concatenate(
                [jnp.where(lane_head == h, qd, 0.0) for h in range(GLA_HEADS)], axis=0
            ).astype(BF16)
            a = jnp.where(amask, _dot_nt(qs, ki), 0.0).astype(BF16)
            inter = _dot_nt(qs, st.astype(BF16))
            outs = []
            for h in range(GLA_HEADS):
                oh = _dot(a[h * C:(h + 1) * C, :], v[:, h * GLA_DV:(h + 1) * GLA_DV])
                outs.append(oh + inter[h * C:(h + 1) * C, :])
            o = jnp.concatenate(outs, axis=-1)
            if fwd:
                acc_ref[pl.ds(r0, C), :] = o
            else:
                acc_ref[pl.ds(r0, C), :] += o
            dst = _dot_tn(v, ks)
            upd = jnp.zeros((GLA_DV, KW), F32)
            for h in range(GLA_HEADS):
                upd = jnp.where(st_lane_head == h, dst[h * GLA_DV:(h + 1) * GLA_DV, :], upd)
            return st * jnp.exp(tot) + upd

        lax.fori_loop(0, n_chunks, body, jnp.zeros((GLA_DV, KW), F32))

    run(0)
    run(1)

    rb = min(L, 256)

    def finalize(i, carry):
        r0 = pl.multiple_of(i * rb, rb)
        o = acc_ref[pl.ds(r0, rb), :]
        og = og_ref[0, pl.ds(r0, rb), :].astype(F32)
        outs = []
        for h in range(GLA_HEADS):
            oh = o[:, h * GLA_DV:(h + 1) * GLA_DV]
            ms = jnp.mean(oh * oh, axis=-1, keepdims=True)
            outs.append(oh * lax.rsqrt(ms + EPS) * on_ref[...])
        y = jnp.concatenate(outs, axis=-1) * (og * _sigmoid(og))
        o_ref[0, pl.ds(r0, rb), :] = y.astype(o_ref.dtype)
        return carry

    lax.fori_loop(0, L // rb, finalize, 0)


def gla_mixer(p3, w_lr, b_lr, onorm):
    B, L, _ = p3.shape
    lr_w = 256
    wl = jnp.zeros((lr_w, 2 * GLA_K_W), F32)
    wl = wl.at[0:GLA_RANK, 0:GLA_K_W].set(w_lr[0].astype(F32))
    wl = wl.at[GLA_RANK:2 * GLA_RANK, GLA_K_W:].set(w_lr[1].astype(F32))
    bl = b_lr.astype(F32).reshape(1, 2 * GLA_K_W)
    col = lambda width, c: pl.BlockSpec((1, L, width), lambda b: (b, 0, c // width))
    return pl.pallas_call(
        _gla_kernel,
        grid=(B,),
        in_specs=[
            col(GLA_K_W, COL_GQ), col(GLA_K_W, COL_GK), col(GLA_V_W, COL_GV),
            col(GLA_V_W, COL_GO), col(lr_w, COL_LR),
            pl.BlockSpec((lr_w, 2 * GLA_K_W), lambda b: (0, 0)),
            pl.BlockSpec((1, 2 * GLA_K_W), lambda b: (0, 0)),
            pl.BlockSpec((1, GLA_DV), lambda b: (0, 0)),
        ],
        out_specs=pl.BlockSpec((1, L, GLA_V_W), lambda b: (b, 0, 0)),
        out_shape=jax.ShapeDtypeStruct((B, L, GLA_V_W), BF16),
        scratch_shapes=[pltpu.VMEM((L, 2 * GLA_K_W), F32), pltpu.VMEM((L, GLA_V_W), F32)],
        compiler_params=_cparams("parallel"),
        name="gla_mixer",
    )(p3, p3, p3, p3, p3, wl.astype(BF16), bl, onorm.astype(F32).reshape(1, GLA_DV))


def _merge_kernel(a_ref, b_ref, c_ref, ga_ref, gb_ref, gc_ref, h_ref, wa_ref, wb_ref, wc_ref,
                  wo_ref, o_ref):
    mixed = (_sigmoid(ga_ref[...].astype(F32)) * _dot(a_ref[...], wa_ref[...])
             + _sigmoid(gb_ref[...].astype(F32)) * _dot(b_ref[...], wb_ref[...])
             + _sigmoid(gc_ref[...].astype(F32)) * _dot(c_ref[...], wc_ref[...]))
    o_ref[...] = h_ref[...] + _dot(mixed.astype(BF16), wo_ref[...])


def merge_out(ya, yb, yc, p2, h, wa, wb, wc, wo, tm=512):
    M, D = h.shape
    tm = _tile(M, tm)
    br = lambda w: pl.BlockSpec((tm, w), lambda i: (i, 0))
    gate = lambda c: pl.BlockSpec((tm, D), lambda i: (i, COL_GATE // D + c))
    wfull = lambda r: pl.BlockSpec((r, D), lambda i: (0, 0))
    return pl.pallas_call(
        _merge_kernel,
        grid=(M // tm,),
        in_specs=[br(ATT_Q_W), br(HY_WIDTH), br(GLA_V_W), gate(0), gate(1), gate(2),
                  pl.BlockSpec((tm, D), lambda i: (i, 0)),
                  wfull(ATT_Q_W), wfull(HY_WIDTH), wfull(GLA_V_W), wfull(D)],
        out_specs=pl.BlockSpec((tm, D), lambda i: (i, 0)),
        out_shape=jax.ShapeDtypeStruct((M, D), F32),
        compiler_params=_cparams("parallel"),
        name="gated_merge_out_proj",
    )(ya, yb, yc, p2, p2, p2, h, wa, wb, wc, wo)


def _head_rmsnorm(x, g):
    ms = jnp.mean(x * x, axis=-1, keepdims=True)
    return x * lax.rsqrt(ms + EPS) * g


def _xattn_kernel(h_ref, ln_ref, wq_ref, k_ref, v_ref, qn_ref, kn_ref, wo_ref, o_ref, att_ref):
    h = h_ref[0]
    hn = _head_rmsnorm(h, ln_ref[...]).astype(BF16)
    q = _dot(hn, wq_ref[...])
    for hd in range(X_HEADS):
        sl = slice(hd * X_HEAD_DIM, (hd + 1) * X_HEAD_DIM)
        qh = (_head_rmsnorm(q[:, sl], qn_ref[...]) * (X_HEAD_DIM ** -0.5)).astype(BF16)
        kh = _head_rmsnorm(k_ref[0, :, sl].astype(F32), kn_ref[...]).astype(BF16)
        s = _dot_nt(qh, kh)
        p = jnp.exp(s - jnp.max(s, axis=-1, keepdims=True))
        l = jnp.sum(p, axis=-1, keepdims=True)
        att_ref[:, sl] = (_dot(p.astype(BF16), v_ref[0, :, sl]) / l).astype(BF16)
    o_ref[0] = h + _dot(att_ref[...], wo_ref[...])


def cross_attention(h3, ln, wq, kv3, qn, kn, wo, tl=512):
    B, L, D = h3.shape
    Mm = kv3.shape[1]
    tl = _tile(L, tl)
    vec = lambda w: pl.BlockSpec((1, w), lambda b, i: (0, 0))
    mat = pl.BlockSpec((D, D), lambda b, i: (0, 0))
    return pl.pallas_call(
        _xattn_kernel,
        grid=(B, L // tl),
        in_specs=[
            pl.BlockSpec((1, tl, D), lambda b, i: (b, i, 0)),
            vec(D), mat,
            pl.BlockSpec((1, Mm, D), lambda b, i: (b, 0, 0)),
            pl.BlockSpec((1, Mm, D), lambda b, i: (b, 0, 1)),
            vec(X_HEAD_DIM), vec(X_HEAD_DIM), mat,
        ],
        out_specs=pl.BlockSpec((1, tl, D), lambda b, i: (b, i, 0)),
        out_shape=jax.ShapeDtypeStruct((B, L, D), F32),
        scratch_shapes=[pltpu.VMEM((tl, D), BF16)],
        compiler_params=_cparams("parallel", "parallel"),
        name="memory_cross_attention",
    )(h3, ln.astype(F32).reshape(1, D), wq, kv3, kv3, qn.astype(F32).reshape(1, X_HEAD_DIM),
      kn.astype(F32).reshape(1, X_HEAD_DIM), wo)


def _mlp_kernel(h_ref, g_ref, w1_ref, w2_ref, o_ref, hn_ref, acc_ref):
    j = pl.program_id(1)

    @pl.when(j == 0)
    def _():
        hn_ref[...] = _head_rmsnorm(h_ref[...], g_ref[...]).astype(BF16)
        acc_ref[...] = jnp.zeros_like(acc_ref)

    a = jnp.maximum(_dot(hn_ref[...], w1_ref[...]), 0.0)
    acc_ref[...] += _dot((a * a).astype(BF16), w2_ref[...])

    @pl.when(j == pl.num_programs(1) - 1)
    def _():
        o_ref[...] = h_ref[...] + acc_ref[...]


def mlp(h, g, w1, w2, tm=512, tf=1024):
    M, D = h.shape
    F = w1.shape[1]
    tm, tf = _tile(M, tm), _tile(F, tf)
    return pl.pallas_call(
        _mlp_kernel,
        grid=(M // tm, F // tf),
        in_specs=[
            pl.BlockSpec((tm, D), lambda i, j: (i, 0)),
            pl.BlockSpec((1, D), lambda i, j: (0, 0)),
            pl.BlockSpec((D, tf), lambda i, j: (0, j)),
            pl.BlockSpec((tf, D), lambda i, j: (j, 0)),
        ],
        out_specs=pl.BlockSpec((tm, D), lambda i, j: (i, 0)),
        out_shape=jax.ShapeDtypeStruct((M, D), F32),
        scratch_shapes=[pltpu.VMEM((tm, D), BF16), pltpu.VMEM((tm, D), F32)],
        compiler_params=_cparams("parallel", "arbitrary"),
        name="relu2_mlp",
    )(h, g.astype(F32).reshape(1, D), w1, w2)


def _pack_w_in(w):
    s = [0, ATT_Q_W, ATT_Q_W + ATT_KV_W, ATT_Q_W + 2 * ATT_KV_W]
    aq, ak, av = w[:, s[0]:s[1]], w[:, s[1]:s[2]], w[:, s[2]:s[3]]
    o = s[3]
    hy = w[:, o:o + 3 * HY_WIDTH]
    o += 3 * HY_WIDTH
    gq, gk = w[:, o:o + GLA_K_W], w[:, o + GLA_K_W:o + 2 * GLA_K_W]
    o += 2 * GLA_K_W
    gv, go = w[:, o:o + GLA_V_W], w[:, o + GLA_V_W:o + 2 * GLA_V_W]
    o += 2 * GLA_V_W
    lr = w[:, o:o + 2 * GLA_RANK]
    o += 2 * GLA_RANK
    gates = w[:, o:]
    pad = jnp.zeros((w.shape[0], COL_GATE - COL_LR - 2 * GLA_RANK), w.dtype)
    packed = jnp.concatenate([aq, hy, gq, gk, gv, go, ak, av, lr, pad, gates], axis=1)
    assert packed.shape[1] == N_PACK
    return packed.astype(BF16)


def kernel(x, mem, ln_mix, w_in, attn_qnorm, attn_knorm, hy_conv, hy_w1, hy_b1, hy_w2, hy_b2,
           hy_w3, hy_b3, hy_freq, hy_skip, gla_w_lr, gla_b_lr, gla_onorm, w_br_attn, w_br_hyena,
           w_br_gla, w_out, ln_x, ln_mem, x_wq, x_wk, x_wv, x_wo, x_qnorm, x_knorm, ln_mlp,
           mlp_w1, mlp_w2):
    B, L, D = x.shape
    Mm = mem.shape[1]
    depth = w_in.shape[0]
    M = B * L

    rope_tabs = _rope_tables(L)
    z_pos, window = _hyena_pos_features(L)
    z_pad_w = 128
    z_pad = jnp.pad(z_pos, ((0, 0), (0, z_pad_w - HY_EMB)))
    fm, g_inv = _dft_tables(L)
    bf = lambda a: a.astype(BF16)

    h = x.astype(F32).reshape(M, D)
    mem2 = mem.astype(F32).reshape(B * Mm, D)
    for i in range(depth):
        p2 = norm_matmul(h, ln_mix[i].astype(F32), _pack_w_in(w_in[i]), BF16)
        p3 = p2.reshape(B, L, N_PACK)
        y_a = attention(p3, attn_qnorm[i], attn_knorm[i], rope_tabs)
        w1p = jnp.pad(hy_w1[i].astype(F32), ((0, z_pad_w - HY_EMB), (0, 0)))
        fa, fb = hyena_filters(z_pad, window, w1p, hy_b1[i].astype(F32), hy_w2[i].astype(F32),
                               hy_b2[i].astype(F32), hy_w3[i].astype(F32),
                               hy_b3[i].astype(F32), hy_freq[i].astype(F32))
        spec = hyena_spectra(fm, fa, fb)
        y_b = hyena_mixer(p3, hy_conv[i], spec, hy_skip[i], fm, g_inv)
        y_c = gla_mixer(p3, gla_w_lr[i], gla_b_lr[i], gla_onorm[i])
        h = merge_out(y_a.reshape(M, ATT_Q_W), y_b.reshape(M, HY_WIDTH), y_c.reshape(M, GLA_V_W),
                      p2, h, bf(w_br_attn[i]), bf(w_br_hyena[i]), bf(w_br_gla[i]), bf(w_out[i]))
        wkv = jnp.concatenate([x_wk[i], x_wv[i]], axis=1)
        kv = norm_matmul(mem2, ln_mem[i].astype(F32), bf(wkv), BF16).reshape(B, Mm, 2 * D)
        h = cross_attention(h.reshape(B, L, D), ln_x[i], bf(x_wq[i]), kv, x_qnorm[i],
                            x_knorm[i], bf(x_wo[i])).reshape(M, D)
        h = mlp(h, ln_mlp[i], bf(mlp_w1[i]), bf(mlp_w2[i]))
    return h.reshape(B, L, D).astype(x.dtype)
```

```python
import functools
import math

import jax
import jax.numpy as jnp
from jax import lax
from jax.experimental import pallas as pl
from jax.experimental.pallas import tpu as pltpu

F32 = jnp.float32
BF16 = jnp.bfloat16

D_MODEL = 1024
GRID_W = 64
ROPE_THETA = 10000.0
HEAD_DIM = 64
ATT_Q_HEADS = 8
ATT_KV_HEADS = 2
ATT_Q_W = ATT_Q_HEADS * HEAD_DIM
ATT_KV_W = ATT_KV_HEADS * HEAD_DIM
HY_WIDTH = 512
HY_ORDER = 2
HY_BANDS = 16
HY_EMB = 1 + 2 * HY_BANDS
HY_FFN = 64
HY_FAST_DECAY = 0.3
HY_SLOW_DECAY = 1.5
HY_TARGET = 1e-2
GLA_HEADS = 4
GLA_DK = 64
GLA_DV = 128
GLA_RANK = 16
GLA_NORMALIZER = 16.0
GLA_CHUNK = 64
GLA_K_W = GLA_HEADS * GLA_DK
GLA_V_W = GLA_HEADS * GLA_DV
X_HEADS = 4
X_HEAD_DIM = D_MODEL // X_HEADS
D_FF = 4 * D_MODEL
N_BRANCH = 3
EPS = 1e-6

COL_AQ = 0
COL_HY = 512
COL_GQ = 2048
COL_GK = 2304
COL_GV = 2560
COL_GO = 3072
COL_AK = 3584
COL_AV = 3712
COL_LR = 3840
COL_GATE = 4096
N_PACK = 7168

VMEM_LIMIT_BYTES = 52 * 1024 * 1024


def _cparams(*sem):
    return pltpu.CompilerParams(dimension_semantics=sem, vmem_limit_bytes=VMEM_LIMIT_BYTES)


def _tile(n, t):
    t = min(n, t)
    assert n % t == 0, (n, t)
    return t


def _dot(a, b):
    return jnp.dot(a, b, preferred_element_type=F32)


def _dot_nt(a, b):
    return lax.dot_general(a, b, (((1,), (1,)), ((), ())), preferred_element_type=F32)


def _dot_tn(a, b):
    return lax.dot_general(a, b, (((0,), (0,)), ((), ())), preferred_element_type=F32)


def _sigmoid(x):
    return 1.0 / (1.0 + jnp.exp(-x))


def _norm_mm_kernel(x_ref, g_ref, w_ref, o_ref, xn_ref):
    @pl.when(pl.program_id(1) == 0)
    def _():
        x = x_ref[...]
        ms = jnp.mean(x * x, axis=-1, keepdims=True)
        xn_ref[...] = (x * lax.rsqrt(ms + EPS) * g_ref[...]).astype(BF16)

    o_ref[...] = _dot(xn_ref[...], w_ref[...]).astype(o_ref.dtype)


def norm_matmul(x, g, w, out_dtype, tm=1024, tn=1024):
    M, K = x.shape
    N = w.shape[1]
    tm, tn = _tile(M, tm), _tile(N, tn)
    return pl.pallas_call(
        _norm_mm_kernel,
        grid=(M // tm, N // tn),
        in_specs=[
            pl.BlockSpec((tm, K), lambda i, j: (i, 0)),
            pl.BlockSpec((1, K), lambda i, j: (0, 0)),
            pl.BlockSpec((K, tn), lambda i, j: (0, j)),
        ],
        out_specs=pl.BlockSpec((tm, tn), lambda i, j: (i, j)),
        out_shape=jax.ShapeDtypeStruct((M, N), out_dtype),
        scratch_shapes=[pltpu.VMEM((tm, K), BF16)],
        compiler_params=_cparams("parallel", "arbitrary"),
        name="norm_matmul",
    )(x, g.reshape(1, K), w)


def _group_mean_sq(x, gm_ref):
    return _dot((x * x).astype(BF16), gm_ref[...])


def _rope(x, c_ref, sa_ref, sb_ref):
    w = x.shape[-1]
    return (x * c_ref[...] + pltpu.roll(x, w - HEAD_DIM // 4, 1) * sa_ref[...]
            + pltpu.roll(x, HEAD_DIM // 4, 1) * sb_ref[...])


def _attn_kernel(q_ref, k_ref, v_ref, qn_ref, kn_ref, cq_ref, saq_ref, sbq_ref,
                 ck_ref, sak_ref, sbk_ref, gmq_ref, gmk_ref, o_ref, kp_ref):
    @pl.when(pl.program_id(1) == 0)
    def _():
        k = k_ref[0].astype(F32)
        kh = k * lax.rsqrt(_group_mean_sq(k, gmk_ref) + EPS) * kn_ref[...]
        kp_ref[...] = _rope(kh, ck_ref, sak_ref, sbk_ref).astype(BF16)

    q = q_ref[0].astype(F32)
    qh = q * lax.rsqrt(_group_mean_sq(q, gmq_ref) + EPS) * qn_ref[...]
    qb = (_rope(qh, cq_ref, saq_ref, sbq_ref) * (HEAD_DIM ** -0.5)).astype(BF16)
    group = ATT_Q_HEADS // ATT_KV_HEADS
    outs = []
    for h in range(ATT_Q_HEADS):
        kv = h // group
        kk = kp_ref[:, kv * HEAD_DIM:(kv + 1) * HEAD_DIM]
        vv = v_ref[0, :, kv * HEAD_DIM:(kv + 1) * HEAD_DIM]
        s = _dot_nt(qb[:, h * HEAD_DIM:(h + 1) * HEAD_DIM], kk)
        p = jnp.exp(s - jnp.max(s, axis=-1, keepdims=True))
        l = jnp.sum(p, axis=-1, keepdims=True)
        outs.append(_dot(p.astype(BF16), vv) / l)
    o_ref[0] = jnp.concatenate(outs, axis=-1).astype(o_ref.dtype)


def _rope_tables(L):
    rows = L // GRID_W
    r, c = jnp.meshgrid(jnp.arange(rows), jnp.arange(GRID_W), indexing="ij")
    n_freq = HEAD_DIM // 4
    inv = ROPE_THETA ** (-jnp.arange(n_freq, dtype=F32) / n_freq)
    pos = jnp.stack([r.reshape(-1), c.reshape(-1)], axis=1).astype(F32)
    ang = pos[:, :, None] * inv
    cos, sin = jnp.cos(ang), jnp.sin(ang)
    zero = jnp.zeros_like(sin)
    c64 = jnp.concatenate([cos, cos], axis=-1).reshape(L, HEAD_DIM)
    sa64 = jnp.concatenate([-sin, zero], axis=-1).reshape(L, HEAD_DIM)
    sb64 = jnp.concatenate([zero, sin], axis=-1).reshape(L, HEAD_DIM)
    return c64, sa64, sb64


def attention(p3, qn, kn, tabs, tq=256):
    B, L, _ = p3.shape
    tq = _tile(L, tq)
    c64, sa64, sb64 = tabs
    tq_tabs = [jnp.tile(t, (1, ATT_Q_HEADS)) for t in (c64, sa64, sb64)]
    tk_tabs = [jnp.tile(t, (1, ATT_KV_HEADS)) for t in (c64, sa64, sb64)]
    qn_t = jnp.tile(qn.astype(F32), ATT_Q_HEADS).reshape(1, ATT_Q_W)
    kn_t = jnp.tile(kn.astype(F32), ATT_KV_HEADS).reshape(1, ATT_KV_W)

    def group_mean(width):
        g = jnp.arange(width) // HEAD_DIM
        return jnp.where(g[:, None] == g[None, :], 1.0 / HEAD_DIM, 0.0).astype(BF16)

    qtab = pl.BlockSpec((tq, ATT_Q_W), lambda b, i: (i, 0))
    ktab = pl.BlockSpec((L, ATT_KV_W), lambda b, i: (0, 0))
    return pl.pallas_call(
        _attn_kernel,
        grid=(B, L // tq),
        in_specs=[
            pl.BlockSpec((1, tq, ATT_Q_W), lambda b, i: (b, i, COL_AQ // ATT_Q_W)),
            pl.BlockSpec((1, L, ATT_KV_W), lambda b, i: (b, 0, COL_AK // ATT_KV_W)),
            pl.BlockSpec((1, L, ATT_KV_W), lambda b, i: (b, 0, COL_AV // ATT_KV_W)),
            pl.BlockSpec((1, ATT_Q_W), lambda b, i: (0, 0)),
            pl.BlockSpec((1, ATT_KV_W), lambda b, i: (0, 0)),
            qtab, qtab, qtab, ktab, ktab, ktab,
            pl.BlockSpec((ATT_Q_W, ATT_Q_W), lambda b, i: (0, 0)),
            pl.BlockSpec((ATT_KV_W, ATT_KV_W), lambda b, i: (0, 0)),
        ],
        out_specs=pl.BlockSpec((1, tq, ATT_Q_W), lambda b, i: (b, i, 0)),
        out_shape=jax.ShapeDtypeStruct((B, L, ATT_Q_W), BF16),
        scratch_shapes=[pltpu.VMEM((L, ATT_KV_W), BF16)],
        compiler_params=_cparams("parallel", "arbitrary"),
        name="gqa_attention",
    )(p3, p3, p3, qn_t, kn_t, *tq_tabs, *tk_tabs, group_mean(ATT_Q_W), group_mean(ATT_KV_W))


def _dft_tables(L):
    n = 2 * L
    k = jnp.arange(L, dtype=jnp.int32)[:, None]
    j = jnp.arange(L, dtype=jnp.int32)[None, :]
    ang = ((k * j) % n).astype(F32) * (2.0 * math.pi / n)
    c = jnp.cos(ang)
    s = -jnp.sin(ang)
    bottom = jnp.where(k == 0, jnp.where(j % 2 == 0, 1.0, -1.0), s)
    bottom_t = jnp.where(j == 0, jnp.where(k % 2 == 0, 1.0, -1.0), s)
    fm = jnp.stack([c, bottom]).astype(BF16)
    g = jnp.concatenate([c, bottom_t], axis=1).astype(BF16)
    return fm, g


def _hyena_pos_features(L):
    t = jnp.arange(L, dtype=F32)
    t_norm = t / max(L - 1, 1)
    w = 2.0 * math.pi * t / L
    f = jnp.linspace(1e-4, HY_BANDS - 1, HY_BANDS, dtype=F32)
    fw = w[:, None] * f
    z = jnp.concatenate([t_norm[:, None], jnp.cos(fw), -jnp.sin(fw)], axis=-1)
    deltas = jnp.abs(jnp.linspace(math.log(HY_TARGET) / HY_FAST_DECAY,
                                  math.log(HY_TARGET) / HY_SLOW_DECAY, HY_WIDTH, dtype=F32))
    window = jnp.exp(-t_norm[:, None] * deltas)
    return z, window


def _sconv_kernel(u_ref, w_ref, o_ref):
    u = u_ref[0].astype(F32)
    L = u.shape[0]
    row = lax.broadcasted_iota(jnp.int32, u.shape, 0)
    prev = jnp.where(row == 0, 0.0, pltpu.roll(u, 1, 0))
    nxt = jnp.where(row == L - 1, 0.0, pltpu.roll(u, L - 1, 0))
    o_ref[0] = (prev * w_ref[0:1, :] + u * w_ref[1:2, :] + nxt * w_ref[2:3, :]).astype(o_ref.dtype)


def short_conv(p3, w):
    B, L, _ = p3.shape
    nblk = 3
    return pl.pallas_call(
        _sconv_kernel,
        grid=(B, nblk),
        in_specs=[
            pl.BlockSpec((1, L, HY_WIDTH), lambda b, c: (b, 0, COL_HY // HY_WIDTH + c)),
            pl.BlockSpec((3, HY_WIDTH), lambda b, c: (0, c)),
        ],
        out_specs=pl.BlockSpec((1, L, HY_WIDTH), lambda b, c: (b, 0, c)),
        out_shape=jax.ShapeDtypeStruct((B, L, nblk * HY_WIDTH), BF16),
        compiler_params=_cparams("parallel", "parallel"),
        name="hyena_short_conv",
    )(p3, w)


def _hyfilter_kernel(z_ref, w1_ref, b1_ref, w2_ref, b2_ref, w3_ref, b3_ref, fr_ref, win_ref,
                     a_ref, bm_ref):
    hp = lax.Precision.HIGHEST
    h = jnp.sin(fr_ref[0:1, :] * (jnp.dot(z_ref[...], w1_ref[...], precision=hp,
                                          preferred_element_type=F32) + b1_ref[...]))
    h = jnp.sin(fr_ref[1:2, :] * (jnp.dot(h, w2_ref[...], precision=hp,
                                          preferred_element_type=F32) + b2_ref[...]))
    hf = jnp.dot(h, w3_ref[...], precision=hp, preferred_element_type=F32) + b3_ref[...]
    win = win_ref[...]
    fwd = hf[:, :HY_WIDTH] * win
    bwd = hf[:, HY_WIDTH:] * win
    row = lax.broadcasted_iota(jnp.int32, bwd.shape, 0)
    bwd = jnp.where(row == 0, 0.0, bwd)
    norm = (jnp.sum(jnp.abs(fwd), axis=0, keepdims=True)
            + jnp.sum(jnp.abs(bwd), axis=0, keepdims=True) + EPS)
    a_ref[...] = ((fwd + bwd) / norm).astype(a_ref.dtype)
    bm_ref[...] = ((fwd - bwd) / norm).astype(bm_ref.dtype)


def hyena_filters(z_pad, window, w1p, b1, w2, b2, w3, b3, freq):
    L = z_pad.shape[0]
    zp = z_pad.shape[1]
    full = lambda shape: pl.BlockSpec(shape, lambda o: (0,) * len(shape))
    out = jax.ShapeDtypeStruct((L, HY_ORDER * HY_WIDTH), BF16)
    return pl.pallas_call(
        _hyfilter_kernel,
        grid=(HY_ORDER,),
        in_specs=[
            full((L, zp)), full((zp, HY_FFN)), full((1, HY_FFN)),
            full((HY_FFN, HY_FFN)), full((1, HY_FFN)),
            pl.BlockSpec((HY_FFN, 2 * HY_WIDTH), lambda o: (0, o)),
            pl.BlockSpec((1, 2 * HY_WIDTH), lambda o: (0, o)),
            full((2, HY_FFN)), full((L, HY_WIDTH)),
        ],
        out_specs=[pl.BlockSpec((L, HY_WIDTH), lambda o: (0, o)),
                   pl.BlockSpec((L, HY_WIDTH), lambda o: (0, o))],
        out_shape=[out, out],
        compiler_params=_cparams("parallel"),
        name="hyena_filter_mlp",
    )(z_pad, w1p, b1.reshape(1, -1), w2, b2.reshape(1, -1), w3, b3.reshape(1, -1), freq, window)


def _spec_kernel(fm_ref, a_ref, bm_ref, h_ref, *, n):
    th = fm_ref.shape[1]
    a = a_ref[...]
    top = _dot(fm_ref[0], a)
    bot = _dot(fm_ref[1], bm_ref[...])
    nyq = _dot(fm_ref[1, 0:8, :], a)[0:1, :]
    row = lax.broadcasted_iota(jnp.int32, top.shape, 0) + pl.program_id(0) * th
    first = row == 0
    scale = jnp.where(first, 1.0 / n, 2.0 / n)
    h_ref[0] = top * scale
    h_ref[1] = jnp.where(first, nyq, bot) * scale


def hyena_spectra(fm, a, bm, th=256):
    L = fm.shape[1]
    W = a.shape[1]
    th = _tile(L, th)
    return pl.pallas_call(
        functools.partial(_spec_kernel, n=2 * L),
        grid=(L // th,),
        in_specs=[
            pl.BlockSpec((2, th, L), lambda i: (0, i, 0)),
            pl.BlockSpec((L, W), lambda i: (0, 0)),
            pl.BlockSpec((L, W), lambda i: (0, 0)),
        ],
        out_specs=pl.BlockSpec((2, th, W), lambda i: (0, i, 0)),
        out_shape=jax.ShapeDtypeStruct((2, L, W), F32),
        compiler_params=_cparams("parallel"),
        name="hyena_filter_spectrum",
    )(fm, a, bm)


def _dft_fwd_kernel(fm_ref, u_ref, h_ref, y_ref):
    th = fm_ref.shape[1]
    u = u_ref[0]
    xt = _dot(fm_ref[0], u)
    xb = _dot(fm_ref[1], u)
    ht = h_ref[0]
    hb = h_ref[1]
    row = lax.broadcasted_iota(jnp.int32, xt.shape, 0) + pl.program_id(0) * th
    first = row == 0
    bb = xb * hb
    y_ref[0, 0] = (xt * ht - jnp.where(first, 0.0, bb)).astype(y_ref.dtype)
    y_ref[0, 1] = jnp.where(first, bb, xt * hb + xb * ht).astype(y_ref.dtype)


def dft_forward(fm, u, u_col, h, order, th=512):
    B, L, _ = u.shape
    th = _tile(L, th)
    return pl.pallas_call(
        _dft_fwd_kernel,
        grid=(L // th, B),
        in_specs=[
            pl.BlockSpec((2, th, L), lambda i, b: (0, i, 0)),
            pl.BlockSpec((1, L, HY_WIDTH), lambda i, b: (b, 0, u_col)),
            pl.BlockSpec((2, th, HY_WIDTH), lambda i, b: (0, i, order)),
        ],
        out_specs=pl.BlockSpec((1, 2, th, HY_WIDTH), lambda i, b: (b, 0, i, 0)),
        out_shape=jax.ShapeDtypeStruct((B, 2, L, HY_WIDTH), BF16),
        compiler_params=_cparams("parallel", "parallel"),
        name="hyena_dft_forward",
    )(fm, u, h)


def _dft_inv_kernel(g_ref, y_ref, u_ref, gate_ref, skip_ref, o_ref):
    L = y_ref.shape[2]
    y = _dot(g_ref[:, :L], y_ref[0, 0]) + _dot(g_ref[:, L:], y_ref[0, 1])
    u = u_ref[0].astype(F32)
    o_ref[0] = ((y + u * skip_ref[...]) * gate_ref[0].astype(F32)).astype(o_ref.dtype)


def dft_inverse(g, y, u, u_col, gate, gate_col, skip, tt=512):
    B, _, L, _ = y.shape
    tt = _tile(L, tt)
    return pl.pallas_call(
        _dft_inv_kernel,
        grid=(L // tt, B),
        in_specs=[
            pl.BlockSpec((tt, 2 * L), lambda i, b: (i, 0)),
            pl.BlockSpec((1, 2, L, HY_WIDTH), lambda i, b: (b, 0, 0, 0)),
            pl.BlockSpec((1, tt, HY_WIDTH), lambda i, b: (b, i, u_col)),
            pl.BlockSpec((1, tt, HY_WIDTH), lambda i, b: (b, i, gate_col)),
            pl.BlockSpec((1, HY_WIDTH), lambda i, b: (0, 0)),
        ],
        out_specs=pl.BlockSpec((1, tt, HY_WIDTH), lambda i, b: (b, i, 0)),
        out_shape=jax.ShapeDtypeStruct((B, L, HY_WIDTH), BF16),
        compiler_params=_cparams("parallel", "parallel"),
        name="hyena_dft_inverse",
    )(g, y, u, gate, skip.reshape(1, HY_WIDTH).astype(F32))


def hyena_mixer(p3, conv_w, spec, skip, fm, g):
    uc = short_conv(p3, conv_w.astype(F32))
    y0 = dft_forward(fm, uc, 0, spec, 0)
    z = dft_inverse(g, y0, uc, 0, uc, 1, skip[0])
    y1 = dft_forward(fm, z, 0, spec, 1)
    return dft_inverse(g, y1, z, 0, uc, 2, skip[1])


def _log_sigmoid(x):
    return -(jnp.maximum(-x, 0.0) + jnp.log(1.0 + jnp.exp(-jnp.abs(x))))


def _dot_hilo(m, x):
    hi = x.astype(BF16)
    lo = (x - hi.astype(F32)).astype(BF16)
    return _dot(m, hi) + _dot(m, lo)


GLA_PREP_ROWS = 256


def _gla_kernel(q_ref, k_ref, v_ref, og_ref, lr_ref, wlr_ref, blr_ref, on_ref, o_ref,
                qs_ref, ki_ref, ks_ref, dec_ref, acc_ref):
    L = q_ref.shape[1]
    C = GLA_CHUNK
    H = GLA_HEADS
    RB = GLA_PREP_ROWS
    n_chunks = L // C
    KW = GLA_K_W

    pr = lax.broadcasted_iota(jnp.int32, (RB, RB), 0)
    pc = lax.broadcasted_iota(jnp.int32, (RB, RB), 1)
    same = (pr // C) == (pc // C)
    cum_f = jnp.where(same & (pc <= pr), 1.0, 0.0).astype(BF16)
    cum_b = jnp.where(same & (pc >= pr), 1.0, 0.0).astype(BF16)
    tot_m = jnp.where(same, 1.0, 0.0).astype(BF16)
    lane_head = lax.broadcasted_iota(jnp.int32, (C, KW), 1) // GLA_DK

    def prep(i, carry):
        r0 = pl.multiple_of(i * RB, RB)
        logit = _dot(lr_ref[0, pl.ds(r0, RB), :], wlr_ref[...]) + blr_ref[...]
        g = _log_sigmoid(logit) * (1.0 / GLA_NORMALIZER)
        q = q_ref[0, pl.ds(r0, RB), :].astype(F32) * (GLA_DK ** -0.5)
        k = k_ref[0, pl.ds(r0, RB), :].astype(F32)
        for d, cum in ((0, cum_f), (1, cum_b)):
            gd = g[:, d * KW:(d + 1) * KW]
            b = _dot_hilo(cum, gd)
            tot = _dot_hilo(tot_m, gd)
            qd = q * jnp.exp(b)
            ki_ref[d, pl.ds(r0, RB), :] = (k * jnp.exp(-b)).astype(BF16)
            ks_ref[d, pl.ds(r0, RB), :] = (k * jnp.exp(tot - b)).astype(BF16)
            dec = jnp.exp(tot)
            for c in range(RB // C):
                n = i * (RB // C) + c
                dec_ref[d, pl.ds(n, 1), :] = dec[c * C:c * C + 1, :]
                qc = qd[c * C:(c + 1) * C, :]
                for h in range(H):
                    row = pl.multiple_of((n * H + h) * C, C)
                    qs_ref[d, pl.ds(row, C), :] = jnp.where(lane_head == h, qc, 0.0).astype(BF16)
        return carry

    lax.fori_loop(0, L // RB, prep, 0)

    st_lane_head = lax.broadcasted_iota(jnp.int32, (GLA_DV, KW), 1) // GLA_DK
    arow = lax.broadcasted_iota(jnp.int32, (H * C, C), 0) % C
    acol = lax.broadcasted_iota(jnp.int32, (H * C, C), 1)
    amasks = (acol <= arow, acol > arow)

    def body(it, sts):
        new = []
        for d in (0, 1):
            st = sts[d]
            n = it if d == 0 else n_chunks - 1 - it
            r0 = pl.multiple_of(n * C, C)
            rq = pl.multiple_of(n * (H * C), H * C)
            qs = qs_ref[d, pl.ds(rq, H * C), :]
            ki = ki_ref[d, pl.ds(r0, C), :]
            ks = ks_ref[d, pl.ds(r0, C), :]
            v = v_ref[0, pl.ds(r0, C), :]
            a = jnp.where(amasks[d], _dot_nt(qs, ki), 0.0).astype(BF16)
            inter = _dot_nt(qs, st.astype(BF16))
            outs = []
            for h in range(H):
                oh = _dot(a[h * C:(h + 1) * C, :], v[:, h * GLA_DV:(h + 1) * GLA_DV])
                outs.append(oh + inter[h * C:(h + 1) * C, :])
            acc_ref[d, pl.ds(r0, C), :] = jnp.concatenate(outs, axis=-1)
            dst = _dot_tn(v, ks)
            upd = jnp.zeros((GLA_DV, KW), F32)
            for h in range(H):
                upd = jnp.where(st_lane_head == h, dst[h * GLA_DV:(h + 1) * GLA_DV, :], upd)
            new.append(st * dec_ref[d, pl.ds(n, 1), :] + upd)
        return tuple(new)

    zero = jnp.zeros((GLA_DV, KW), F32)
    lax.fori_loop(0, n_chunks, body, (zero, zero), unroll=2)

    def finalize(i, carry):
        r0 = pl.multiple_of(i * RB, RB)
        o = acc_ref[0, pl.ds(r0, RB), :] + acc_ref[1, pl.ds(r0, RB), :]
        og = og_ref[0, pl.ds(r0, RB), :].astype(F32)
        outs = []
        for h in range(H):
            oh = o[:, h * GLA_DV:(h + 1) * GLA_DV]
            ms = jnp.mean(oh * oh, axis=-1, keepdims=True)
            outs.append(oh * lax.rsqrt(ms + EPS) * on_ref[...])
        y = jnp.concatenate(outs, axis=-1) * (og * _sigmoid(og))
        o_ref[0, pl.ds(r0, RB), :] = y.astype(o_ref.dtype)
        return carry

    lax.fori_loop(0, L // RB, finalize, 0)


def gla_mixer(p3, w_lr, b_lr, onorm):
    B, L, _ = p3.shape
    lr_w = 256
    wl = jnp.zeros((lr_w, 2 * GLA_K_W), F32)
    wl = wl.at[0:GLA_RANK, 0:GLA_K_W].set(w_lr[0].astype(F32))
    wl = wl.at[GLA_RANK:2 * GLA_RANK, GLA_K_W:].set(w_lr[1].astype(F32))
    bl = b_lr.astype(F32).reshape(1, 2 * GLA_K_W)
    col = lambda width, c: pl.BlockSpec((1, L, width), lambda b: (b, 0, c // width))
    return pl.pallas_call(
        _gla_kernel,
        grid=(B,),
        in_specs=[
            col(GLA_K_W, COL_GQ), col(GLA_K_W, COL_GK), col(GLA_V_W, COL_GV),
            col(GLA_V_W, COL_GO), col(lr_w, COL_LR),
            pl.BlockSpec((lr_w, 2 * GLA_K_W), lambda b: (0, 0)),
            pl.BlockSpec((1, 2 * GLA_K_W), lambda b: (0, 0)),
            pl.BlockSpec((1, GLA_DV), lambda b: (0, 0)),
        ],
        out_specs=pl.BlockSpec((1, L, GLA_V_W), lambda b: (b, 0, 0)),
        out_shape=jax.ShapeDtypeStruct((B, L, GLA_V_W), BF16),
        scratch_shapes=[
            pltpu.VMEM((2, L * GLA_HEADS, GLA_K_W), BF16),
            pltpu.VMEM((2, L, GLA_K_W), BF16),
            pltpu.VMEM((2, L, GLA_K_W), BF16),
            pltpu.VMEM((2, L // GLA_CHUNK, GLA_K_W), F32),
            pltpu.VMEM((2, L, GLA_V_W), F32),
        ],
        compiler_params=_cparams("parallel"),
        name="gla_mixer",
    )(p3, p3, p3, p3, p3, wl.astype(BF16), bl, onorm.astype(F32).reshape(1, GLA_DV))


def _merge_kernel(a_ref, b_ref, c_ref, ga_ref, gb_ref, gc_ref, h_ref, wa_ref, wb_ref, wc_ref,
                  wo_ref, o_ref):
    mixed = (_sigmoid(ga_ref[...].astype(F32)) * _dot(a_ref[...], wa_ref[...])
             + _sigmoid(gb_ref[...].astype(F32)) * _dot(b_ref[...], wb_ref[...])
             + _sigmoid(gc_ref[...].astype(F32)) * _dot(c_ref[...], wc_ref[...]))
    o_ref[...] = h_ref[...] + _dot(mixed.astype(BF16), wo_ref[...])


def merge_out(ya, yb, yc, p2, h, wa, wb, wc, wo, tm=512):
    M, D = h.shape
    tm = _tile(M, tm)
    br = lambda w: pl.BlockSpec((tm, w), lambda i: (i, 0))
    gate = lambda c: pl.BlockSpec((tm, D), lambda i: (i, COL_GATE // D + c))
    wfull = lambda r: pl.BlockSpec((r, D), lambda i: (0, 0))
    return pl.pallas_call(
        _merge_kernel,
        grid=(M // tm,),
        in_specs=[br(ATT_Q_W), br(HY_WIDTH), br(GLA_V_W), gate(0), gate(1), gate(2),
                  pl.BlockSpec((tm, D), lambda i: (i, 0)),
                  wfull(ATT_Q_W), wfull(HY_WIDTH), wfull(GLA_V_W), wfull(D)],
        out_specs=pl.BlockSpec((tm, D), lambda i: (i, 0)),
        out_shape=jax.ShapeDtypeStruct((M, D), F32),
        compiler_params=_cparams("parallel"),
        name="gated_merge_out_proj",
    )(ya, yb, yc, p2, p2, p2, h, wa, wb, wc, wo)


def _head_rmsnorm(x, g):
    ms = jnp.mean(x * x, axis=-1, keepdims=True)
    return x * lax.rsqrt(ms + EPS) * g


def _xattn_kernel(h_ref, ln_ref, wq_ref, k_ref, v_ref, qn_ref, kn_ref, wo_ref, o_ref, att_ref):
    h = h_ref[0]
    hn = _head_rmsnorm(h, ln_ref[...]).astype(BF16)
    q = _dot(hn, wq_ref[...])
    for hd in range(X_HEADS):
        sl = slice(hd * X_HEAD_DIM, (hd + 1) * X_HEAD_DIM)
        qh = (_head_rmsnorm(q[:, sl], qn_ref[...]) * (X_HEAD_DIM ** -0.5)).astype(BF16)
        kh = _head_rmsnorm(k_ref[0, :, sl].astype(F32), kn_ref[...]).astype(BF16)
        s = _dot_nt(qh, kh)
        p = jnp.exp(s - jnp.max(s, axis=-1, keepdims=True))
        l = jnp.sum(p, axis=-1, keepdims=True)
        att_ref[:, sl] = (_dot(p.astype(BF16), v_ref[0, :, sl]) / l).astype(BF16)
    o_ref[0] = h + _dot(att_ref[...], wo_ref[...])


def cross_attention(h3, ln, wq, kv3, qn, kn, wo, tl=512):
    B, L, D = h3.shape
    Mm = kv3.shape[1]
    tl = _tile(L, tl)
    vec = lambda w: pl.BlockSpec((1, w), lambda b, i: (0, 0))
    mat = pl.BlockSpec((D, D), lambda b, i: (0, 0))
    return pl.pallas_call(
        _xattn_kernel,
        grid=(B, L // tl),
        in_specs=[
            pl.BlockSpec((1, tl, D), lambda b, i: (b, i, 0)),
            vec(D), mat,
            pl.BlockSpec((1, Mm, D), lambda b, i: (b, 0, 0)),
            pl.BlockSpec((1, Mm, D), lambda b, i: (b, 0, 1)),
            vec(X_HEAD_DIM), vec(X_HEAD_DIM), mat,
        ],
        out_specs=pl.BlockSpec((1, tl, D), lambda b, i: (b, i, 0)),
        out_shape=jax.ShapeDtypeStruct((B, L, D), F32),
        scratch_shapes=[pltpu.VMEM((tl, D), BF16)],
        compiler_params=_cparams("parallel", "parallel"),
        name="memory_cross_attention",
    )(h3, ln.astype(F32).reshape(1, D), wq, kv3, kv3, qn.astype(F32).reshape(1, X_HEAD_DIM),
      kn.astype(F32).reshape(1, X_HEAD_DIM), wo)


def _mlp_kernel(h_ref, g_ref, w1_ref, w2_ref, o_ref, hn_ref, acc_ref):
    j = pl.program_id(1)

    @pl.when(j == 0)
    def _():
        hn_ref[...] = _head_rmsnorm(h_ref[...], g_ref[...]).astype(BF16)
        acc_ref[...] = jnp.zeros_like(acc_ref)

    a = jnp.maximum(_dot(hn_ref[...], w1_ref[...]), 0.0)
    acc_ref[...] += _dot((a * a).astype(BF16), w2_ref[...])

    @pl.when(j == pl.num_programs(1) - 1)
    def _():
        o_ref[...] = h_ref[...] + acc_ref[...]


def mlp(h, g, w1, w2, tm=1024, tf=1024):
    M, D = h.shape
    F = w1.shape[1]
    tm, tf = _tile(M, tm), _tile(F, tf)
    return pl.pallas_call(
        _mlp_kernel,
        grid=(M // tm, F // tf),
        in_specs=[
            pl.BlockSpec((tm, D), lambda i, j: (i, 0)),
            pl.BlockSpec((1, D), lambda i, j: (0, 0)),
            pl.BlockSpec((D, tf), lambda i, j: (0, j)),
            pl.BlockSpec((tf, D), lambda i, j: (j, 0)),
        ],
        out_specs=pl.BlockSpec((tm, D), lambda i, j: (i, 0)),
        out_shape=jax.ShapeDtypeStruct((M, D), F32),
        scratch_shapes=[pltpu.VMEM((tm, D), BF16), pltpu.VMEM((tm, D), F32)],
        compiler_params=_cparams("parallel", "arbitrary"),
        name="relu2_mlp",
    )(h, g.astype(F32).reshape(1, D), w1, w2)


def _pack_w_in(w):
    s = [0, ATT_Q_W, ATT_Q_W + ATT_KV_W, ATT_Q_W + 2 * ATT_KV_W]
    aq, ak, av = w[:, s[0]:s[1]], w[:, s[1]:s[2]], w[:, s[2]:s[3]]
    o = s[3]
    hy = w[:, o:o + 3 * HY_WIDTH]
    o += 3 * HY_WIDTH
    gq, gk = w[:, o:o + GLA_K_W], w[:, o + GLA_K_W:o + 2 * GLA_K_W]
    o += 2 * GLA_K_W
    gv, go = w[:, o:o + GLA_V_W], w[:, o + GLA_V_W:o + 2 * GLA_V_W]
    o += 2 * GLA_V_W
    lr = w[:, o:o + 2 * GLA_RANK]
    o += 2 * GLA_RANK
    gates = w[:, o:]
    pad = jnp.zeros((w.shape[0], COL_GATE - COL_LR - 2 * GLA_RANK), w.dtype)
    packed = jnp.concatenate([aq, hy, gq, gk, gv, go, ak, av, lr, pad, gates], axis=1)
    assert packed.shape[1] == N_PACK
    return packed.astype(BF16)


def kernel(x, mem, ln_mix, w_in, attn_qnorm, attn_knorm, hy_conv, hy_w1, hy_b1, hy_w2, hy_b2,
           hy_w3, hy_b3, hy_freq, hy_skip, gla_w_lr, gla_b_lr, gla_onorm, w_br_attn, w_br_hyena,
           w_br_gla, w_out, ln_x, ln_mem, x_wq, x_wk, x_wv, x_wo, x_qnorm, x_knorm, ln_mlp,
           mlp_w1, mlp_w2):
    B, L, D = x.shape
    Mm = mem.shape[1]
    depth = w_in.shape[0]
    M = B * L

    rope_tabs = _rope_tables(L)
    z_pos, window = _hyena_pos_features(L)
    z_pad_w = 128
    z_pad = jnp.pad(z_pos, ((0, 0), (0, z_pad_w - HY_EMB)))
    fm, g_inv = _dft_tables(L)
    bf = lambda a: a.astype(BF16)

    h = x.astype(F32).reshape(M, D)
    mem2 = mem.astype(F32).reshape(B * Mm, D)
    for i in range(depth):
        p2 = norm_matmul(h, ln_mix[i].astype(F32), _pack_w_in(w_in[i]), BF16)
        p3 = p2.reshape(B, L, N_PACK)
        y_a = attention(p3, attn_qnorm[i], attn_knorm[i], rope_tabs)
        w1p = jnp.pad(hy_w1[i].astype(F32), ((0, z_pad_w - HY_EMB), (0, 0)))
        fa, fb = hyena_filters(z_pad, window, w1p, hy_b1[i].astype(F32), hy_w2[i].astype(F32),
                               hy_b2[i].astype(F32), hy_w3[i].astype(F32),
                               hy_b3[i].astype(F32), hy_freq[i].astype(F32))
        spec = hyena_spectra(fm, fa, fb)
        y_b = hyena_mixer(p3, hy_conv[i], spec, hy_skip[i], fm, g_inv)
        y_c = gla_mixer(p3, gla_w_lr[i], gla_b_lr[i], gla_onorm[i])
        h = merge_out(y_a.reshape(M, ATT_Q_W), y_b.reshape(M, HY_WIDTH), y_c.reshape(M, GLA_V_W),
                      p2, h, bf(w_br_attn[i]), bf(w_br_hyena[i]), bf(w_br_gla[i]), bf(w_out[i]))
        wkv = jnp.concatenate([x_wk[i], x_wv[i]], axis=1)
        kv = norm_matmul(mem2, ln_mem[i].astype(F32), bf(wkv), BF16).reshape(B, Mm, 2 * D)
        h = cross_attention(h.reshape(B, L, D), ln_x[i], bf(x_wq[i]), kv, x_qnorm[i],
                            x_knorm[i], bf(x_wo[i])).reshape(M, D)
        h = mlp(h, ln_mlp[i], bf(mlp_w1[i]), bf(mlp_w2[i]))
    return h.reshape(B, L, D).astype(x.dtype)
```

```python
import functools
import math

import jax
import jax.numpy as jnp
from jax import lax
from jax.experimental import pallas as pl
from jax.experimental.pallas import tpu as pltpu

F32 = jnp.float32
BF16 = jnp.bfloat16

D_MODEL = 1024
GRID_W = 64
ROPE_THETA = 10000.0
HEAD_DIM = 64
ATT_Q_HEADS = 8
ATT_KV_HEADS = 2
ATT_Q_W = ATT_Q_HEADS * HEAD_DIM
ATT_KV_W = ATT_KV_HEADS * HEAD_DIM
HY_WIDTH = 512
HY_ORDER = 2
HY_BANDS = 16
HY_EMB = 1 + 2 * HY_BANDS
HY_FFN = 64
HY_FAST_DECAY = 0.3
HY_SLOW_DECAY = 1.5
HY_TARGET = 1e-2
GLA_HEADS = 4
GLA_DK = 64
GLA_DV = 128
GLA_RANK = 16
GLA_NORMALIZER = 16.0
GLA_CHUNK = 64
GLA_K_W = GLA_HEADS * GLA_DK
GLA_V_W = GLA_HEADS * GLA_DV
X_HEADS = 4
X_HEAD_DIM = D_MODEL // X_HEADS
D_FF = 4 * D_MODEL
N_BRANCH = 3
EPS = 1e-6

COL_AQ = 0
COL_HY = 512
COL_GQ = 2048
COL_GK = 2304
COL_GV = 2560
COL_GO = 3072
COL_AK = 3584
COL_AV = 3712
COL_LR = 3840
COL_GATE = 4096
N_PACK = 7168

VMEM_LIMIT_BYTES = 52 * 1024 * 1024


def _cparams(*sem):
    return pltpu.CompilerParams(dimension_semantics=sem, vmem_limit_bytes=VMEM_LIMIT_BYTES)


def _tile(n, t):
    t = min(n, t)
    assert n % t == 0, (n, t)
    return t


def _dot(a, b):
    return jnp.dot(a, b, preferred_element_type=F32)


def _dot_nt(a, b):
    return lax.dot_general(a, b, (((1,), (1,)), ((), ())), preferred_element_type=F32)


def _dot_tn(a, b):
    return lax.dot_general(a, b, (((0,), (0,)), ((), ())), preferred_element_type=F32)


def _sigmoid(x):
    return 1.0 / (1.0 + jnp.exp(-x))


def _norm_mm_kernel(x_ref, g_ref, w_ref, o_ref, xn_ref):
    @pl.when(pl.program_id(1) == 0)
    def _():
        x = x_ref[...]
        ms = jnp.mean(x * x, axis=-1, keepdims=True)
        xn_ref[...] = (x * lax.rsqrt(ms + EPS) * g_ref[...]).astype(BF16)

    o_ref[...] = _dot(xn_ref[...], w_ref[...]).astype(o_ref.dtype)


def norm_matmul(x, g, w, out_dtype, tm=1024, tn=1024):
    M, K = x.shape
    N = w.shape[1]
    tm, tn = _tile(M, tm), _tile(N, tn)
    return pl.pallas_call(
        _norm_mm_kernel,
        grid=(M // tm, N // tn),
        in_specs=[
            pl.BlockSpec((tm, K), lambda i, j: (i, 0)),
            pl.BlockSpec((1, K), lambda i, j: (0, 0)),
            pl.BlockSpec((K, tn), lambda i, j: (0, j)),
        ],
        out_specs=pl.BlockSpec((tm, tn), lambda i, j: (i, j)),
        out_shape=jax.ShapeDtypeStruct((M, N), out_dtype),
        scratch_shapes=[pltpu.VMEM((tm, K), BF16)],
        compiler_params=_cparams("parallel", "arbitrary"),
        name="norm_matmul",
    )(x, g.reshape(1, K), w)


def _group_mean_sq(x, gm_ref):
    return _dot((x * x).astype(BF16), gm_ref[...])


def _rope(x, c_ref, sa_ref, sb_ref):
    w = x.shape[-1]
    return (x * c_ref[...] + pltpu.roll(x, w - HEAD_DIM // 4, 1) * sa_ref[...]
            + pltpu.roll(x, HEAD_DIM // 4, 1) * sb_ref[...])


def _attn_kernel(q_ref, k_ref, v_ref, qn_ref, kn_ref, cq_ref, saq_ref, sbq_ref,
                 ck_ref, sak_ref, sbk_ref, gmq_ref, gmk_ref, o_ref, kp_ref, vx_ref):
    @pl.when(pl.program_id(1) == 0)
    def _():
        k = k_ref[0].astype(F32)
        kh = k * lax.rsqrt(_group_mean_sq(k, gmk_ref) + EPS) * kn_ref[...]
        kp_ref[...] = _rope(kh, ck_ref, sak_ref, sbk_ref).astype(BF16)
        v = v_ref[0]
        ones = jnp.ones((v.shape[0], HEAD_DIM), BF16)
        vx_ref[...] = jnp.concatenate(
            [piece for kv in range(ATT_KV_HEADS)
             for piece in (v[:, kv * HEAD_DIM:(kv + 1) * HEAD_DIM], ones)], axis=-1)

    q = q_ref[0].astype(F32)
    qh = q * lax.rsqrt(_group_mean_sq(q, gmq_ref) + EPS) * qn_ref[...]
    qb = (_rope(qh, cq_ref, saq_ref, sbq_ref) * (HEAD_DIM ** -0.5 * math.log2(math.e))).astype(BF16)
    group = ATT_Q_HEADS // ATT_KV_HEADS
    outs = []
    for h in range(ATT_Q_HEADS):
        kv = h // group
        kk = kp_ref[:, kv * HEAD_DIM:(kv + 1) * HEAD_DIM]
        s = _dot_nt(qb[:, h * HEAD_DIM:(h + 1) * HEAD_DIM], kk)
        p = jnp.exp2(s - jnp.max(s, axis=-1, keepdims=True))
        ox = _dot(p.astype(BF16), vx_ref[:, kv * 2 * HEAD_DIM:(kv + 1) * 2 * HEAD_DIM])
        outs.append(ox[:, :HEAD_DIM] / ox[:, HEAD_DIM:])
    o_ref[0] = jnp.concatenate(outs, axis=-1).astype(o_ref.dtype)


def _rope_tables(L):
    rows = L // GRID_W
    r, c = jnp.meshgrid(jnp.arange(rows), jnp.arange(GRID_W), indexing="ij")
    n_freq = HEAD_DIM // 4
    inv = ROPE_THETA ** (-jnp.arange(n_freq, dtype=F32) / n_freq)
    pos = jnp.stack([r.reshape(-1), c.reshape(-1)], axis=1).astype(F32)
    ang = pos[:, :, None] * inv
    cos, sin = jnp.cos(ang), jnp.sin(ang)
    zero = jnp.zeros_like(sin)
    c64 = jnp.concatenate([cos, cos], axis=-1).reshape(L, HEAD_DIM)
    sa64 = jnp.concatenate([-sin, zero], axis=-1).reshape(L, HEAD_DIM)
    sb64 = jnp.concatenate([zero, sin], axis=-1).reshape(L, HEAD_DIM)
    return c64, sa64, sb64


def attention(p3, qn, kn, tabs, tq=256):
    B, L, _ = p3.shape
    tq = _tile(L, tq)
    c64, sa64, sb64 = tabs
    tq_tabs = [jnp.tile(t, (1, ATT_Q_HEADS)) for t in (c64, sa64, sb64)]
    tk_tabs = [jnp.tile(t, (1, ATT_KV_HEADS)) for t in (c64, sa64, sb64)]
    qn_t = jnp.tile(qn.astype(F32), ATT_Q_HEADS).reshape(1, ATT_Q_W)
    kn_t = jnp.tile(kn.astype(F32), ATT_KV_HEADS).reshape(1, ATT_KV_W)

    def group_mean(width):
        g = jnp.arange(width) // HEAD_DIM
        return jnp.where(g[:, None] == g[None, :], 1.0 / HEAD_DIM, 0.0).astype(BF16)

    qtab = pl.BlockSpec((tq, ATT_Q_W), lambda b, i: (i, 0))
    ktab = pl.BlockSpec((L, ATT_KV_W), lambda b, i: (0, 0))
    return pl.pallas_call(
        _attn_kernel,
        grid=(B, L // tq),
        in_specs=[
            pl.BlockSpec((1, tq, ATT_Q_W), lambda b, i: (b, i, COL_AQ // ATT_Q_W)),
            pl.BlockSpec((1, L, ATT_KV_W), lambda b, i: (b, 0, COL_AK // ATT_KV_W)),
            pl.BlockSpec((1, L, ATT_KV_W), lambda b, i: (b, 0, COL_AV // ATT_KV_W)),
            pl.BlockSpec((1, ATT_Q_W), lambda b, i: (0, 0)),
            pl.BlockSpec((1, ATT_KV_W), lambda b, i: (0, 0)),
            qtab, qtab, qtab, ktab, ktab, ktab,
            pl.BlockSpec((ATT_Q_W, ATT_Q_W), lambda b, i: (0, 0)),
            pl.BlockSpec((ATT_KV_W, ATT_KV_W), lambda b, i: (0, 0)),
        ],
        out_specs=pl.BlockSpec((1, tq, ATT_Q_W), lambda b, i: (b, i, 0)),
        out_shape=jax.ShapeDtypeStruct((B, L, ATT_Q_W), BF16),
        scratch_shapes=[pltpu.VMEM((L, ATT_KV_W), BF16), pltpu.VMEM((L, 2 * ATT_KV_W), BF16)],
        compiler_params=_cparams("parallel", "arbitrary"),
        name="gqa_attention",
    )(p3, p3, p3, qn_t, kn_t, *tq_tabs, *tk_tabs, group_mean(ATT_Q_W), group_mean(ATT_KV_W))


def _dft_tables(L):
    n = 2 * L
    k = jnp.arange(L, dtype=jnp.int32)[:, None]
    j = jnp.arange(L, dtype=jnp.int32)[None, :]
    step = 64
    ja = jnp.arange(0, L, step, dtype=jnp.int32)[None, :]
    jb = jnp.arange(step, dtype=jnp.int32)[None, :]
    ang_a = ((k * ja) % n).astype(F32) * (2.0 * math.pi / n)
    ang_b = ((k * jb) % n).astype(F32) * (2.0 * math.pi / n)
    ca, sa = jnp.cos(ang_a)[:, :, None], jnp.sin(ang_a)[:, :, None]
    cb, sb = jnp.cos(ang_b)[:, None, :], jnp.sin(ang_b)[:, None, :]
    c = (ca * cb - sa * sb).reshape(L, L)
    s = -(sa * cb + ca * sb).reshape(L, L)
    bottom = jnp.where(k == 0, jnp.where(j % 2 == 0, 1.0, -1.0), s)
    bottom_t = jnp.where(j == 0, jnp.where(k % 2 == 0, 1.0, -1.0), s)
    fm = jnp.stack([c, bottom]).astype(BF16)
    g = jnp.concatenate([c, bottom_t], axis=1).astype(BF16)
    return fm, g


def _hyena_pos_features(L):
    t = jnp.arange(L, dtype=F32)
    t_norm = t / max(L - 1, 1)
    w = 2.0 * math.pi * t / L
    f = jnp.linspace(1e-4, HY_BANDS - 1, HY_BANDS, dtype=F32)
    fw = w[:, None] * f
    z = jnp.concatenate([t_norm[:, None], jnp.cos(fw), -jnp.sin(fw)], axis=-1)
    deltas = jnp.abs(jnp.linspace(math.log(HY_TARGET) / HY_FAST_DECAY,
                                  math.log(HY_TARGET) / HY_SLOW_DECAY, HY_WIDTH, dtype=F32))
    window = jnp.exp(-t_norm[:, None] * deltas)
    return z, window


def _sconv_kernel(u_ref, w_ref, o_ref):
    u = u_ref[0].astype(F32)
    L = u.shape[0]
    row = lax.broadcasted_iota(jnp.int32, u.shape, 0)
    prev = jnp.where(row == 0, 0.0, pltpu.roll(u, 1, 0))
    nxt = jnp.where(row == L - 1, 0.0, pltpu.roll(u, L - 1, 0))
    o_ref[0] = (prev * w_ref[0:1, :] + u * w_ref[1:2, :] + nxt * w_ref[2:3, :]).astype(o_ref.dtype)


def short_conv(p3, w):
    B, L, _ = p3.shape
    nblk = 3
    return pl.pallas_call(
        _sconv_kernel,
        grid=(B, nblk),
        in_specs=[
            pl.BlockSpec((1, L, HY_WIDTH), lambda b, c: (b, 0, COL_HY // HY_WIDTH + c)),
            pl.BlockSpec((3, HY_WIDTH), lambda b, c: (0, c)),
        ],
        out_specs=pl.BlockSpec((1, L, HY_WIDTH), lambda b, c: (b, 0, c)),
        out_shape=jax.ShapeDtypeStruct((B, L, nblk * HY_WIDTH), BF16),
        compiler_params=_cparams("parallel", "parallel"),
        name="hyena_short_conv",
    )(p3, w)


def _hyfilter_kernel(z_ref, w1_ref, b1_ref, w2_ref, b2_ref, w3_ref, b3_ref, fr_ref, win_ref,
                     a_ref, bm_ref):
    hp = lax.Precision.HIGHEST
    h = jnp.sin(fr_ref[0:1, :] * (jnp.dot(z_ref[...], w1_ref[...], precision=hp,
                                          preferred_element_type=F32) + b1_ref[...]))
    h = jnp.sin(fr_ref[1:2, :] * (jnp.dot(h, w2_ref[...], precision=hp,
                                          preferred_element_type=F32) + b2_ref[...]))
    hf = jnp.dot(h, w3_ref[...], precision=hp, preferred_element_type=F32) + b3_ref[...]
    win = win_ref[...]
    fwd = hf[:, :HY_WIDTH] * win
    bwd = hf[:, HY_WIDTH:] * win
    row = lax.broadcasted_iota(jnp.int32, bwd.shape, 0)
    bwd = jnp.where(row == 0, 0.0, bwd)
    norm = (jnp.sum(jnp.abs(fwd), axis=0, keepdims=True)
            + jnp.sum(jnp.abs(bwd), axis=0, keepdims=True) + EPS)
    a_ref[...] = ((fwd + bwd) / norm).astype(a_ref.dtype)
    bm_ref[...] = ((fwd - bwd) / norm).astype(bm_ref.dtype)


def hyena_filters(z_pad, window, w1p, b1, w2, b2, w3, b3, freq):
    L = z_pad.shape[0]
    zp = z_pad.shape[1]
    full = lambda shape: pl.BlockSpec(shape, lambda o: (0,) * len(shape))
    out = jax.ShapeDtypeStruct((L, HY_ORDER * HY_WIDTH), BF16)
    return pl.pallas_call(
        _hyfilter_kernel,
        grid=(HY_ORDER,),
        in_specs=[
            full((L, zp)), full((zp, HY_FFN)), full((1, HY_FFN)),
            full((HY_FFN, HY_FFN)), full((1, HY_FFN)),
            pl.BlockSpec((HY_FFN, 2 * HY_WIDTH), lambda o: (0, o)),
            pl.BlockSpec((1, 2 * HY_WIDTH), lambda o: (0, o)),
            full((2, HY_FFN)), full((L, HY_WIDTH)),
        ],
        out_specs=[pl.BlockSpec((L, HY_WIDTH), lambda o: (0, o)),
                   pl.BlockSpec((L, HY_WIDTH), lambda o: (0, o))],
        out_shape=[out, out],
        compiler_params=_cparams("parallel"),
        name="hyena_filter_mlp",
    )(z_pad, w1p, b1.reshape(1, -1), w2, b2.reshape(1, -1), w3, b3.reshape(1, -1), freq, window)


def _spec_kernel(fm_ref, a_ref, bm_ref, h_ref, *, n):
    th = fm_ref.shape[1]
    a = a_ref[...]
    top = _dot(fm_ref[0], a)
    bot = _dot(fm_ref[1], bm_ref[...])
    nyq = _dot(fm_ref[1, 0:8, :], a)[0:1, :]
    row = lax.broadcasted_iota(jnp.int32, top.shape, 0) + pl.program_id(0) * th
    first = row == 0
    scale = jnp.where(first, 1.0 / n, 2.0 / n)
    h_ref[0] = top * scale
    h_ref[1] = jnp.where(first, nyq, bot) * scale


def hyena_spectra(fm, a, bm, th=256):
    L = fm.shape[1]
    W = a.shape[1]
    th = _tile(L, th)
    return pl.pallas_call(
        functools.partial(_spec_kernel, n=2 * L),
        grid=(L // th,),
        in_specs=[
            pl.BlockSpec((2, th, L), lambda i: (0, i, 0)),
            pl.BlockSpec((L, W), lambda i: (0, 0)),
            pl.BlockSpec((L, W), lambda i: (0, 0)),
        ],
        out_specs=pl.BlockSpec((2, th, W), lambda i: (0, i, 0)),
        out_shape=jax.ShapeDtypeStruct((2, L, W), F32),
        compiler_params=_cparams("parallel"),
        name="hyena_filter_spectrum",
    )(fm, a, bm)


def _dft_fwd_kernel(fm_ref, u_ref, h_ref, y_ref):
    th = fm_ref.shape[1]
    u = u_ref[0]
    xt = _dot(fm_ref[0], u)
    xb = _dot(fm_ref[1], u)
    ht = h_ref[0]
    hb = h_ref[1]
    row = lax.broadcasted_iota(jnp.int32, xt.shape, 0) + pl.program_id(0) * th
    first = row == 0
    bb = xb * hb
    y_ref[0, 0] = (xt * ht - jnp.where(first, 0.0, bb)).astype(y_ref.dtype)
    y_ref[0, 1] = jnp.where(first, bb, xt * hb + xb * ht).astype(y_ref.dtype)


def dft_forward(fm, u, u_col, h, order, th=512):
    B, L, _ = u.shape
    th = _tile(L, th)
    return pl.pallas_call(
        _dft_fwd_kernel,
        grid=(L // th, B),
        in_specs=[
            pl.BlockSpec((2, th, L), lambda i, b: (0, i, 0)),
            pl.BlockSpec((1, L, HY_WIDTH), lambda i, b: (b, 0, u_col)),
            pl.BlockSpec((2, th, HY_WIDTH), lambda i, b: (0, i, order)),
        ],
        out_specs=pl.BlockSpec((1, 2, th, HY_WIDTH), lambda i, b: (b, 0, i, 0)),
        out_shape=jax.ShapeDtypeStruct((B, 2, L, HY_WIDTH), BF16),
        compiler_params=_cparams("parallel", "parallel"),
        name="hyena_dft_forward",
    )(fm, u, h)


def _dft_inv_kernel(g_ref, y_ref, u_ref, gate_ref, skip_ref, o_ref):
    L = y_ref.shape[2]
    y = _dot(g_ref[:, :L], y_ref[0, 0]) + _dot(g_ref[:, L:], y_ref[0, 1])
    u = u_ref[0].astype(F32)
    o_ref[0] = ((y + u * skip_ref[...]) * gate_ref[0].astype(F32)).astype(o_ref.dtype)


def dft_inverse(g, y, u, u_col, gate, gate_col, skip, tt=512):
    B, _, L, _ = y.shape
    tt = _tile(L, tt)
    return pl.pallas_call(
        _dft_inv_kernel,
        grid=(L // tt, B),
        in_specs=[
            pl.BlockSpec((tt, 2 * L), lambda i, b: (i, 0)),
            pl.BlockSpec((1, 2, L, HY_WIDTH), lambda i, b: (b, 0, 0, 0)),
            pl.BlockSpec((1, tt, HY_WIDTH), lambda i, b: (b, i, u_col)),
            pl.BlockSpec((1, tt, HY_WIDTH), lambda i, b: (b, i, gate_col)),
            pl.BlockSpec((1, HY_WIDTH), lambda i, b: (0, 0)),
        ],
        out_specs=pl.BlockSpec((1, tt, HY_WIDTH), lambda i, b: (b, i, 0)),
        out_shape=jax.ShapeDtypeStruct((B, L, HY_WIDTH), BF16),
        compiler_params=_cparams("parallel", "parallel"),
        name="hyena_dft_inverse",
    )(g, y, u, gate, skip.reshape(1, HY_WIDTH).astype(F32))


def hyena_mixer(p3, conv_w, spec, skip, fm, g):
    uc = short_conv(p3, conv_w.astype(F32))
    y0 = dft_forward(fm, uc, 0, spec, 0)
    z = dft_inverse(g, y0, uc, 0, uc, 1, skip[0])
    y1 = dft_forward(fm, z, 0, spec, 1)
    return dft_inverse(g, y1, z, 0, uc, 2, skip[1])


def _log_sigmoid(x):
    return -(jnp.maximum(-x, 0.0) + jnp.log(1.0 + jnp.exp(-jnp.abs(x))))


def _dot_hilo(m, x):
    hi = x.astype(BF16)
    lo = (x - hi.astype(F32)).astype(BF16)
    return _dot(m, hi) + _dot(m, lo)


GLA_PREP_ROWS = 256


def _gla_kernel(q_ref, k_ref, v_ref, og_ref, lr_ref, wlr_ref, blr_ref, on_ref, o_ref,
                qs_ref, ki_ref, ks_ref, dec_ref, acc_ref):
    L = q_ref.shape[1]
    C = GLA_CHUNK
    H = GLA_HEADS
    RB = GLA_PREP_ROWS
    n_chunks = L // C
    KW = GLA_K_W

    pr = lax.broadcasted_iota(jnp.int32, (RB, RB), 0)
    pc = lax.broadcasted_iota(jnp.int32, (RB, RB), 1)
    same = (pr // C) == (pc // C)
    cum_f = jnp.where(same & (pc <= pr), 1.0, 0.0).astype(BF16)
    cum_b = jnp.where(same & (pc >= pr), 1.0, 0.0).astype(BF16)
    tot_m = jnp.where(same, 1.0, 0.0).astype(BF16)
    lane_head = lax.broadcasted_iota(jnp.int32, (C, KW), 1) // GLA_DK

    def prep(i, carry):
        r0 = pl.multiple_of(i * RB, RB)
        logit = _dot(lr_ref[0, pl.ds(r0, RB), :], wlr_ref[...]) + blr_ref[...]
        g = _log_sigmoid(logit) * (1.0 / GLA_NORMALIZER)
        q = q_ref[0, pl.ds(r0, RB), :].astype(F32) * (GLA_DK ** -0.5)
        k = k_ref[0, pl.ds(r0, RB), :].astype(F32)
        for d, cum in ((0, cum_f), (1, cum_b)):
            gd = g[:, d * KW:(d + 1) * KW]
            b = _dot_hilo(cum, gd)
            tot = _dot_hilo(tot_m, gd)
            qd = q * jnp.exp(b)
            ki_ref[d, pl.ds(r0, RB), :] = (k * jnp.exp(-b)).astype(BF16)
            ks_ref[d, pl.ds(r0, RB), :] = (k * jnp.exp(tot - b)).astype(BF16)
            dec = jnp.exp(tot)
            for c in range(RB // C):
                n = i * (RB // C) + c
                dec_ref[d, pl.ds(n, 1), :] = dec[c * C:c * C + 1, :]
                qc = qd[c * C:(c + 1) * C, :]
                for h in range(H):
                    row = pl.multiple_of((n * H + h) * C, C)
                    qs_ref[d, pl.ds(row, C), :] = jnp.where(lane_head == h, qc, 0.0).astype(BF16)
        return carry

    lax.fori_loop(0, L // RB, prep, 0)

    st_lane_head = lax.broadcasted_iota(jnp.int32, (GLA_DV, KW), 1) // GLA_DK
    arow = lax.broadcasted_iota(jnp.int32, (H * C, C), 0) % C
    acol = lax.broadcasted_iota(jnp.int32, (H * C, C), 1)
    amasks = (acol <= arow, acol > arow)

    def body(it, sts):
        new = []
        for d in (0, 1):
            st = sts[d]
            n = it if d == 0 else n_chunks - 1 - it
            r0 = pl.multiple_of(n * C, C)
            rq = pl.multiple_of(n * (H * C), H * C)
            qs = qs_ref[d, pl.ds(rq, H * C), :]
            ki = ki_ref[d, pl.ds(r0, C), :]
            ks = ks_ref[d, pl.ds(r0, C), :]
            v = v_ref[0, pl.ds(r0, C), :]
            a = jnp.where(amasks[d], _dot_nt(qs, ki), 0.0).astype(BF16)
            inter = _dot_nt(qs, st.astype(BF16))
            outs = []
            for h in range(H):
                oh = _dot(a[h * C:(h + 1) * C, :], v[:, h * GLA_DV:(h + 1) * GLA_DV])
                outs.append(oh + inter[h * C:(h + 1) * C, :])
            acc_ref[d, pl.ds(r0, C), :] = jnp.concatenate(outs, axis=-1)
            dst = _dot_tn(v, ks)
            upd = jnp.zeros((GLA_DV, KW), F32)
            for h in range(H):
                upd = jnp.where(st_lane_head == h, dst[h * GLA_DV:(h + 1) * GLA_DV, :], upd)
            new.append(st * dec_ref[d, pl.ds(n, 1), :] + upd)
        return tuple(new)

    zero = jnp.zeros((GLA_DV, KW), F32)
    lax.fori_loop(0, n_chunks, body, (zero, zero), unroll=2)

    def finalize(i, carry):
        r0 = pl.multiple_of(i * RB, RB)
        o = acc_ref[0, pl.ds(r0, RB), :] + acc_ref[1, pl.ds(r0, RB), :]
        og = og_ref[0, pl.ds(r0, RB), :].astype(F32)
        outs = []
        for h in range(H):
            oh = o[:, h * GLA_DV:(h + 1) * GLA_DV]
            ms = jnp.mean(oh * oh, axis=-1, keepdims=True)
            outs.append(oh * lax.rsqrt(ms + EPS) * on_ref[...])
        y = jnp.concatenate(outs, axis=-1) * (og * _sigmoid(og))
        o_ref[0, pl.ds(r0, RB), :] = y.astype(o_ref.dtype)
        return carry

    lax.fori_loop(0, L // RB, finalize, 0)


def gla_mixer(p3, w_lr, b_lr, onorm):
    B, L, _ = p3.shape
    lr_w = 256
    wl = jnp.zeros((lr_w, 2 * GLA_K_W), F32)
    wl = wl.at[0:GLA_RANK, 0:GLA_K_W].set(w_lr[0].astype(F32))
    wl = wl.at[GLA_RANK:2 * GLA_RANK, GLA_K_W:].set(w_lr[1].astype(F32))
    bl = b_lr.astype(F32).reshape(1, 2 * GLA_K_W)
    col = lambda width, c: pl.BlockSpec((1, L, width), lambda b: (b, 0, c // width))
    return pl.pallas_call(
        _gla_kernel,
        grid=(B,),
        in_specs=[
            col(GLA_K_W, COL_GQ), col(GLA_K_W, COL_GK), col(GLA_V_W, COL_GV),
            col(GLA_V_W, COL_GO), col(lr_w, COL_LR),
            pl.BlockSpec((lr_w, 2 * GLA_K_W), lambda b: (0, 0)),
            pl.BlockSpec((1, 2 * GLA_K_W), lambda b: (0, 0)),
            pl.BlockSpec((1, GLA_DV), lambda b: (0, 0)),
        ],
        out_specs=pl.BlockSpec((1, L, GLA_V_W), lambda b: (b, 0, 0)),
        out_shape=jax.ShapeDtypeStruct((B, L, GLA_V_W), BF16),
        scratch_shapes=[
            pltpu.VMEM((2, L * GLA_HEADS, GLA_K_W), BF16),
            pltpu.VMEM((2, L, GLA_K_W), BF16),
            pltpu.VMEM((2, L, GLA_K_W), BF16),
            pltpu.VMEM((2, L // GLA_CHUNK, GLA_K_W), F32),
            pltpu.VMEM((2, L, GLA_V_W), F32),
        ],
        compiler_params=_cparams("parallel"),
        name="gla_mixer",
    )(p3, p3, p3, p3, p3, wl.astype(BF16), bl, onorm.astype(F32).reshape(1, GLA_DV))


def _merge_kernel(a_ref, b_ref, c_ref, ga_ref, gb_ref, gc_ref, h_ref, wa_ref, wb_ref, wc_ref,
                  wo_ref, o_ref):
    mixed = (_sigmoid(ga_ref[...].astype(F32)) * _dot(a_ref[...], wa_ref[...])
             + _sigmoid(gb_ref[...].astype(F32)) * _dot(b_ref[...], wb_ref[...])
             + _sigmoid(gc_ref[...].astype(F32)) * _dot(c_ref[...], wc_ref[...]))
    o_ref[...] = h_ref[...] + _dot(mixed.astype(BF16), wo_ref[...])


def merge_out(ya, yb, yc, p2, h, wa, wb, wc, wo, tm=512):
    M, D = h.shape
    tm = _tile(M, tm)
    br = lambda w: pl.BlockSpec((tm, w), lambda i: (i, 0))
    gate = lambda c: pl.BlockSpec((tm, D), lambda i: (i, COL_GATE // D + c))
    wfull = lambda r: pl.BlockSpec((r, D), lambda i: (0, 0))
    return pl.pallas_call(
        _merge_kernel,
        grid=(M // tm,),
        in_specs=[br(ATT_Q_W), br(HY_WIDTH), br(GLA_V_W), gate(0), gate(1), gate(2),
                  pl.BlockSpec((tm, D), lambda i: (i, 0)),
                  wfull(ATT_Q_W), wfull(HY_WIDTH), wfull(GLA_V_W), wfull(D)],
        out_specs=pl.BlockSpec((tm, D), lambda i: (i, 0)),
        out_shape=jax.ShapeDtypeStruct((M, D), F32),
        compiler_params=_cparams("parallel"),
        name="gated_merge_out_proj",
    )(ya, yb, yc, p2, p2, p2, h, wa, wb, wc, wo)


def _head_rmsnorm(x, g):
    ms = jnp.mean(x * x, axis=-1, keepdims=True)
    return x * lax.rsqrt(ms + EPS) * g


def _xattn_kernel(h_ref, ln_ref, wq_ref, k_ref, v_ref, qn_ref, kn_ref, wo_ref, o_ref, att_ref):
    h = h_ref[0]
    hn = _head_rmsnorm(h, ln_ref[...]).astype(BF16)
    q = _dot(hn, wq_ref[...])
    for hd in range(X_HEADS):
        sl = slice(hd * X_HEAD_DIM, (hd + 1) * X_HEAD_DIM)
        qh = (_head_rmsnorm(q[:, sl], qn_ref[...]) * (X_HEAD_DIM ** -0.5)).astype(BF16)
        kh = _head_rmsnorm(k_ref[0, :, sl].astype(F32), kn_ref[...]).astype(BF16)
        s = _dot_nt(qh, kh)
        p = jnp.exp(s - jnp.max(s, axis=-1, keepdims=True))
        l = jnp.sum(p, axis=-1, keepdims=True)
        att_ref[:, sl] = (_dot(p.astype(BF16), v_ref[0, :, sl]) / l).astype(BF16)
    o_ref[0] = h + _dot(att_ref[...], wo_ref[...])


def cross_attention(h3, ln, wq, kv3, qn, kn, wo, tl=512):
    B, L, D = h3.shape
    Mm = kv3.shape[1]
    tl = _tile(L, tl)
    vec = lambda w: pl.BlockSpec((1, w), lambda b, i: (0, 0))
    mat = pl.BlockSpec((D, D), lambda b, i: (0, 0))
    return pl.pallas_call(
        _xattn_kernel,
        grid=(B, L // tl),
        in_specs=[
            pl.BlockSpec((1, tl, D), lambda b, i: (b, i, 0)),
            vec(D), mat,
            pl.BlockSpec((1, Mm, D), lambda b, i: (b, 0, 0)),
            pl.BlockSpec((1, Mm, D), lambda b, i: (b, 0, 1)),
            vec(X_HEAD_DIM), vec(X_HEAD_DIM), mat,
        ],
        out_specs=pl.BlockSpec((1, tl, D), lambda b, i: (b, i, 0)),
        out_shape=jax.ShapeDtypeStruct((B, L, D), F32),
        scratch_shapes=[pltpu.VMEM((tl, D), BF16)],
        compiler_params=_cparams("parallel", "parallel"),
        name="memory_cross_attention",
    )(h3, ln.astype(F32).reshape(1, D), wq, kv3, kv3, qn.astype(F32).reshape(1, X_HEAD_DIM),
      kn.astype(F32).reshape(1, X_HEAD_DIM), wo)


def _mlp_kernel(h_ref, g_ref, w1_ref, w2_ref, o_ref, hn_ref, acc_ref):
    j = pl.program_id(1)

    @pl.when(j == 0)
    def _():
        hn_ref[...] = _head_rmsnorm(h_ref[...], g_ref[...]).astype(BF16)
        acc_ref[...] = jnp.zeros_like(acc_ref)

    a = jnp.maximum(_dot(hn_ref[...], w1_ref[...]), 0.0)
    acc_ref[...] += _dot((a * a).astype(BF16), w2_ref[...])

    @pl.when(j == pl.num_programs(1) - 1)
    def _():
        o_ref[...] = h_ref[...] + acc_ref[...]


def mlp(h, g, w1, w2, tm=1024, tf=1024):
    M, D = h.shape
    F = w1.shape[1]
    tm, tf = _tile(M, tm), _tile(F, tf)
    return pl.pallas_call(
        _mlp_kernel,
        grid=(M // tm, F // tf),
        in_specs=[
            pl.BlockSpec((tm, D), lambda i, j: (i, 0)),
            pl.BlockSpec((1, D), lambda i, j: (0, 0)),
            pl.BlockSpec((D, tf), lambda i, j: (0, j)),
            pl.BlockSpec((tf, D), lambda i, j: (j, 0)),
        ],
        out_specs=pl.BlockSpec((tm, D), lambda i, j: (i, 0)),
        out_shape=jax.ShapeDtypeStruct((M, D), F32),
        scratch_shapes=[pltpu.VMEM((tm, D), BF16), pltpu.VMEM((tm, D), F32)],
        compiler_params=_cparams("parallel", "arbitrary"),
        name="relu2_mlp",
    )(h, g.astype(F32).reshape(1, D), w1, w2)


def _pack_w_in(w):
    w = w.astype(BF16)
    s = [0, ATT_Q_W, ATT_Q_W + ATT_KV_W, ATT_Q_W + 2 * ATT_KV_W]
    aq, ak, av = w[:, s[0]:s[1]], w[:, s[1]:s[2]], w[:, s[2]:s[3]]
    o = s[3]
    hy = w[:, o:o + 3 * HY_WIDTH]
    o += 3 * HY_WIDTH
    gq, gk = w[:, o:o + GLA_K_W], w[:, o + GLA_K_W:o + 2 * GLA_K_W]
    o += 2 * GLA_K_W
    gv, go = w[:, o:o + GLA_V_W], w[:, o + GLA_V_W:o + 2 * GLA_V_W]
    o += 2 * GLA_V_W
    lr = w[:, o:o + 2 * GLA_RANK]
    o += 2 * GLA_RANK
    gates = w[:, o:]
    pad = jnp.zeros((w.shape[0], COL_GATE - COL_LR - 2 * GLA_RANK), w.dtype)
    packed = jnp.concatenate([aq, hy, gq, gk, gv, go, ak, av, lr, pad, gates], axis=1)
    assert packed.shape[1] == N_PACK
    return packed.astype(BF16)


def kernel(x, mem, ln_mix, w_in, attn_qnorm, attn_knorm, hy_conv, hy_w1, hy_b1, hy_w2, hy_b2,
           hy_w3, hy_b3, hy_freq, hy_skip, gla_w_lr, gla_b_lr, gla_onorm, w_br_attn, w_br_hyena,
           w_br_gla, w_out, ln_x, ln_mem, x_wq, x_wk, x_wv, x_wo, x_qnorm, x_knorm, ln_mlp,
           mlp_w1, mlp_w2):
    B, L, D = x.shape
    Mm = mem.shape[1]
    depth = w_in.shape[0]
    M = B * L

    rope_tabs = _rope_tables(L)
    z_pos, window = _hyena_pos_features(L)
    z_pad_w = 128
    z_pad = jnp.pad(z_pos, ((0, 0), (0, z_pad_w - HY_EMB)))
    fm, g_inv = _dft_tables(L)
    bf = lambda a: a.astype(BF16)

    h = x.astype(F32).reshape(M, D)
    mem2 = mem.astype(F32).reshape(B * Mm, D)
    for i in range(depth):
        p2 = norm_matmul(h, ln_mix[i].astype(F32), _pack_w_in(w_in[i]), BF16)
        p3 = p2.reshape(B, L, N_PACK)
        y_a = attention(p3, attn_qnorm[i], attn_knorm[i], rope_tabs)
        w1p = jnp.pad(hy_w1[i].astype(F32), ((0, z_pad_w - HY_EMB), (0, 0)))
        fa, fb = hyena_filters(z_pad, window, w1p, hy_b1[i].astype(F32), hy_w2[i].astype(F32),
                               hy_b2[i].astype(F32), hy_w3[i].astype(F32),
                               hy_b3[i].astype(F32), hy_freq[i].astype(F32))
        spec = hyena_spectra(fm, fa, fb)
        y_b = hyena_mixer(p3, hy_conv[i], spec, hy_skip[i], fm, g_inv)
        y_c = gla_mixer(p3, gla_w_lr[i], gla_b_lr[i], gla_onorm[i])
        h = merge_out(y_a.reshape(M, ATT_Q_W), y_b.reshape(M, HY_WIDTH), y_c.reshape(M, GLA_V_W),
                      p2, h, bf(w_br_attn[i]), bf(w_br_hyena[i]), bf(w_br_gla[i]), bf(w_out[i]))
        wkv = jnp.concatenate([x_wk[i], x_wv[i]], axis=1)
        kv = norm_matmul(mem2, ln_mem[i].astype(F32), bf(wkv), BF16).reshape(B, Mm, 2 * D)
        h = cross_attention(h.reshape(B, L, D), ln_x[i], bf(x_wq[i]), kv, x_qnorm[i],
                            x_knorm[i], bf(x_wo[i])).reshape(M, D)
        h = mlp(h, ln_mlp[i], bf(mlp_w1[i]), bf(mlp_w2[i]))
    return h.reshape(B, L, D).astype(x.dtype)
```

```python
import functools
import math

import jax
import jax.numpy as jnp
from jax import lax
from jax.experimental import pallas as pl
from jax.experimental.pallas import tpu as pltpu

F32 = jnp.float32
BF16 = jnp.bfloat16

D_MODEL = 1024
GRID_W = 64
ROPE_THETA = 10000.0
HEAD_DIM = 64
ATT_Q_HEADS = 8
ATT_KV_HEADS = 2
ATT_Q_W = ATT_Q_HEADS * HEAD_DIM
ATT_KV_W = ATT_KV_HEADS * HEAD_DIM
HY_WIDTH = 512
HY_ORDER = 2
HY_BANDS = 16
HY_EMB = 1 + 2 * HY_BANDS
HY_FFN = 64
HY_FAST_DECAY = 0.3
HY_SLOW_DECAY = 1.5
HY_TARGET = 1e-2
GLA_HEADS = 4
GLA_DK = 64
GLA_DV = 128
GLA_RANK = 16
GLA_NORMALIZER = 16.0
GLA_CHUNK = 64
GLA_K_W = GLA_HEADS * GLA_DK
GLA_V_W = GLA_HEADS * GLA_DV
X_HEADS = 4
X_HEAD_DIM = D_MODEL // X_HEADS
D_FF = 4 * D_MODEL
N_BRANCH = 3
EPS = 1e-6

COL_AQ = 0
COL_HY = 512
COL_GQ = 2048
COL_GK = 2304
COL_GV = 2560
COL_GO = 3072
COL_AK = 3584
COL_AV = 3712
COL_LR = 3840
COL_GATE = 4096
N_PACK = 7168

VMEM_LIMIT_BYTES = 52 * 1024 * 1024


def _cparams(*sem):
    return pltpu.CompilerParams(dimension_semantics=sem, vmem_limit_bytes=VMEM_LIMIT_BYTES)


def _tile(n, t):
    t = min(n, t)
    assert n % t == 0, (n, t)
    return t


def _dot(a, b):
    return jnp.dot(a, b, preferred_element_type=F32)


def _dot_nt(a, b):
    return lax.dot_general(a, b, (((1,), (1,)), ((), ())), preferred_element_type=F32)


def _dot_tn(a, b):
    return lax.dot_general(a, b, (((0,), (0,)), ((), ())), preferred_element_type=F32)


def _sigmoid(x):
    return 1.0 / (1.0 + jnp.exp(-x))


def _norm_mm_kernel(x_ref, g_ref, w_ref, o_ref, xn_ref):
    @pl.when(pl.program_id(1) == 0)
    def _():
        x = x_ref[...]
        ms = jnp.mean(x * x, axis=-1, keepdims=True)
        xn_ref[...] = (x * lax.rsqrt(ms + EPS) * g_ref[...]).astype(BF16)

    o_ref[...] = _dot(xn_ref[...], w_ref[...]).astype(o_ref.dtype)


def norm_matmul(x, g, w, out_dtype, tm=1024, tn=1024):
    M, K = x.shape
    N = w.shape[1]
    tm, tn = _tile(M, tm), _tile(N, tn)
    return pl.pallas_call(
        _norm_mm_kernel,
        grid=(M // tm, N // tn),
        in_specs=[
            pl.BlockSpec((tm, K), lambda i, j: (i, 0)),
            pl.BlockSpec((1, K), lambda i, j: (0, 0)),
            pl.BlockSpec((K, tn), lambda i, j: (0, j)),
        ],
        out_specs=pl.BlockSpec((tm, tn), lambda i, j: (i, j)),
        out_shape=jax.ShapeDtypeStruct((M, N), out_dtype),
        scratch_shapes=[pltpu.VMEM((tm, K), BF16)],
        compiler_params=_cparams("parallel", "arbitrary"),
        name="norm_matmul",
    )(x, g.reshape(1, K), w)


def _group_mean_sq(x, gm_ref):
    return _dot((x * x).astype(BF16), gm_ref[...])


def _rope(x, c_ref, sa_ref, sb_ref):
    w = x.shape[-1]
    return (x * c_ref[...] + pltpu.roll(x, w - HEAD_DIM // 4, 1) * sa_ref[...]
            + pltpu.roll(x, HEAD_DIM // 4, 1) * sb_ref[...])


def _attn_kernel(q_ref, k_ref, v_ref, qn_ref, kn_ref, cq_ref, saq_ref, sbq_ref,
                 ck_ref, sak_ref, sbk_ref, gmq_ref, gmk_ref, o_ref, kp_ref, vx_ref):
    @pl.when(pl.program_id(1) == 0)
    def _():
        k = k_ref[0].astype(F32)
        kh = k * lax.rsqrt(_group_mean_sq(k, gmk_ref) + EPS) * kn_ref[...]
        kp_ref[...] = _rope(kh, ck_ref, sak_ref, sbk_ref).astype(BF16)
        v = v_ref[0]
        ones = jnp.ones((v.shape[0], HEAD_DIM), BF16)
        vx_ref[...] = jnp.concatenate(
            [piece for kv in range(ATT_KV_HEADS)
             for piece in (v[:, kv * HEAD_DIM:(kv + 1) * HEAD_DIM], ones)], axis=-1)

    q = q_ref[0].astype(F32)
    qh = q * lax.rsqrt(_group_mean_sq(q, gmq_ref) + EPS) * qn_ref[...]
    qb = (_rope(qh, cq_ref, saq_ref, sbq_ref) * (HEAD_DIM ** -0.5 * math.log2(math.e))).astype(BF16)
    group = ATT_Q_HEADS // ATT_KV_HEADS
    outs = []
    for h in range(ATT_Q_HEADS):
        kv = h // group
        kk = kp_ref[:, kv * HEAD_DIM:(kv + 1) * HEAD_DIM]
        s = _dot_nt(qb[:, h * HEAD_DIM:(h + 1) * HEAD_DIM], kk)
        p = jnp.exp2(s - jnp.max(s, axis=-1, keepdims=True))
        ox = _dot(p.astype(BF16), vx_ref[:, kv * 2 * HEAD_DIM:(kv + 1) * 2 * HEAD_DIM])
        outs.append(ox[:, :HEAD_DIM] / ox[:, HEAD_DIM:])
    o_ref[0] = jnp.concatenate(outs, axis=-1).astype(o_ref.dtype)


def _rope_tables(L):
    rows = L // GRID_W
    r, c = jnp.meshgrid(jnp.arange(rows), jnp.arange(GRID_W), indexing="ij")
    n_freq = HEAD_DIM // 4
    inv = ROPE_THETA ** (-jnp.arange(n_freq, dtype=F32) / n_freq)
    pos = jnp.stack([r.reshape(-1), c.reshape(-1)], axis=1).astype(F32)
    ang = pos[:, :, None] * inv
    cos, sin = jnp.cos(ang), jnp.sin(ang)
    zero = jnp.zeros_like(sin)
    c64 = jnp.concatenate([cos, cos], axis=-1).reshape(L, HEAD_DIM)
    sa64 = jnp.concatenate([-sin, zero], axis=-1).reshape(L, HEAD_DIM)
    sb64 = jnp.concatenate([zero, sin], axis=-1).reshape(L, HEAD_DIM)
    return c64, sa64, sb64


def attention(p3, qn, kn, tabs, tq=256):
    B, L, _ = p3.shape
    tq = _tile(L, tq)
    c64, sa64, sb64 = tabs
    tq_tabs = [jnp.tile(t, (1, ATT_Q_HEADS)) for t in (c64, sa64, sb64)]
    tk_tabs = [jnp.tile(t, (1, ATT_KV_HEADS)) for t in (c64, sa64, sb64)]
    qn_t = jnp.tile(qn.astype(F32), ATT_Q_HEADS).reshape(1, ATT_Q_W)
    kn_t = jnp.tile(kn.astype(F32), ATT_KV_HEADS).reshape(1, ATT_KV_W)

    def group_mean(width):
        g = jnp.arange(width) // HEAD_DIM
        return jnp.where(g[:, None] == g[None, :], 1.0 / HEAD_DIM, 0.0).astype(BF16)

    qtab = pl.BlockSpec((tq, ATT_Q_W), lambda b, i: (i, 0))
    ktab = pl.BlockSpec((L, ATT_KV_W), lambda b, i: (0, 0))
    return pl.pallas_call(
        _attn_kernel,
        grid=(B, L // tq),
        in_specs=[
            pl.BlockSpec((1, tq, ATT_Q_W), lambda b, i: (b, i, COL_AQ // ATT_Q_W)),
            pl.BlockSpec((1, L, ATT_KV_W), lambda b, i: (b, 0, COL_AK // ATT_KV_W)),
            pl.BlockSpec((1, L, ATT_KV_W), lambda b, i: (b, 0, COL_AV // ATT_KV_W)),
            pl.BlockSpec((1, ATT_Q_W), lambda b, i: (0, 0)),
            pl.BlockSpec((1, ATT_KV_W), lambda b, i: (0, 0)),
            qtab, qtab, qtab, ktab, ktab, ktab,
            pl.BlockSpec((ATT_Q_W, ATT_Q_W), lambda b, i: (0, 0)),
            pl.BlockSpec((ATT_KV_W, ATT_KV_W), lambda b, i: (0, 0)),
        ],
        out_specs=pl.BlockSpec((1, tq, ATT_Q_W), lambda b, i: (b, i, 0)),
        out_shape=jax.ShapeDtypeStruct((B, L, ATT_Q_W), BF16),
        scratch_shapes=[pltpu.VMEM((L, ATT_KV_W), BF16), pltpu.VMEM((L, 2 * ATT_KV_W), BF16)],
        compiler_params=_cparams("parallel", "arbitrary"),
        name="gqa_attention",
    )(p3, p3, p3, qn_t, kn_t, *tq_tabs, *tk_tabs, group_mean(ATT_Q_W), group_mean(ATT_KV_W))


def _dft_tables(L):
    n = 2 * L
    k = jnp.arange(L, dtype=jnp.int32)[:, None]
    j = jnp.arange(L, dtype=jnp.int32)[None, :]
    step = 64
    ja = jnp.arange(0, L, step, dtype=jnp.int32)[None, :]
    jb = jnp.arange(step, dtype=jnp.int32)[None, :]
    ang_a = ((k * ja) % n).astype(F32) * (2.0 * math.pi / n)
    ang_b = ((k * jb) % n).astype(F32) * (2.0 * math.pi / n)
    ca, sa = jnp.cos(ang_a)[:, :, None], jnp.sin(ang_a)[:, :, None]
    cb, sb = jnp.cos(ang_b)[:, None, :], jnp.sin(ang_b)[:, None, :]
    c = (ca * cb - sa * sb).reshape(L, L)
    s = -(sa * cb + ca * sb).reshape(L, L)
    bottom = jnp.where(k == 0, jnp.where(j % 2 == 0, 1.0, -1.0), s)
    bottom_t = jnp.where(j == 0, jnp.where(k % 2 == 0, 1.0, -1.0), s)
    fm = jnp.stack([c, bottom]).astype(BF16)
    g = jnp.concatenate([c, bottom_t], axis=1).astype(BF16)
    return fm, g


def _hyena_pos_features(L):
    t = jnp.arange(L, dtype=F32)
    t_norm = t / max(L - 1, 1)
    w = 2.0 * math.pi * t / L
    f = jnp.linspace(1e-4, HY_BANDS - 1, HY_BANDS, dtype=F32)
    fw = w[:, None] * f
    z = jnp.concatenate([t_norm[:, None], jnp.cos(fw), -jnp.sin(fw)], axis=-1)
    deltas = jnp.abs(jnp.linspace(math.log(HY_TARGET) / HY_FAST_DECAY,
                                  math.log(HY_TARGET) / HY_SLOW_DECAY, HY_WIDTH, dtype=F32))
    window = jnp.exp(-t_norm[:, None] * deltas)
    return z, window


def _sconv_kernel(u_ref, w_ref, o_ref):
    u = u_ref[0].astype(F32)
    L = u.shape[0]
    row = lax.broadcasted_iota(jnp.int32, u.shape, 0)
    prev = jnp.where(row == 0, 0.0, pltpu.roll(u, 1, 0))
    nxt = jnp.where(row == L - 1, 0.0, pltpu.roll(u, L - 1, 0))
    o_ref[0] = (prev * w_ref[0:1, :] + u * w_ref[1:2, :] + nxt * w_ref[2:3, :]).astype(o_ref.dtype)


def short_conv(p3, w):
    B, L, _ = p3.shape
    nblk = 3
    return pl.pallas_call(
        _sconv_kernel,
        grid=(B, nblk),
        in_specs=[
            pl.BlockSpec((1, L, HY_WIDTH), lambda b, c: (b, 0, COL_HY // HY_WIDTH + c)),
            pl.BlockSpec((3, HY_WIDTH), lambda b, c: (0, c)),
        ],
        out_specs=pl.BlockSpec((1, L, HY_WIDTH), lambda b, c: (b, 0, c)),
        out_shape=jax.ShapeDtypeStruct((B, L, nblk * HY_WIDTH), BF16),
        compiler_params=_cparams("parallel", "parallel"),
        name="hyena_short_conv",
    )(p3, w)


def _hyfilter_kernel(z_ref, w1_ref, b1_ref, w2_ref, b2_ref, w3_ref, b3_ref, fr_ref, win_ref,
                     a_ref, bm_ref):
    hp = lax.Precision.HIGHEST
    h = jnp.sin(fr_ref[0:1, :] * (jnp.dot(z_ref[...], w1_ref[...], precision=hp,
                                          preferred_element_type=F32) + b1_ref[...]))
    h = jnp.sin(fr_ref[1:2, :] * (jnp.dot(h, w2_ref[...], precision=hp,
                                          preferred_element_type=F32) + b2_ref[...]))
    hf = jnp.dot(h, w3_ref[...], precision=hp, preferred_element_type=F32) + b3_ref[...]
    win = win_ref[...]
    fwd = hf[:, :HY_WIDTH] * win
    bwd = hf[:, HY_WIDTH:] * win
    row = lax.broadcasted_iota(jnp.int32, bwd.shape, 0)
    bwd = jnp.where(row == 0, 0.0, bwd)
    norm = (jnp.sum(jnp.abs(fwd), axis=0, keepdims=True)
            + jnp.sum(jnp.abs(bwd), axis=0, keepdims=True) + EPS)
    a_ref[...] = (fwd / norm).astype(a_ref.dtype)
    bm_ref[...] = (bwd / norm).astype(bm_ref.dtype)


def hyena_filters(z_pad, window, w1p, b1, w2, b2, w3, b3, freq):
    L = z_pad.shape[0]
    zp = z_pad.shape[1]
    full = lambda shape: pl.BlockSpec(shape, lambda o: (0,) * len(shape))
    out = jax.ShapeDtypeStruct((L, HY_ORDER * HY_WIDTH), BF16)
    return pl.pallas_call(
        _hyfilter_kernel,
        grid=(HY_ORDER,),
        in_specs=[
            full((L, zp)), full((zp, HY_FFN)), full((1, HY_FFN)),
            full((HY_FFN, HY_FFN)), full((1, HY_FFN)),
            pl.BlockSpec((HY_FFN, 2 * HY_WIDTH), lambda o: (0, o)),
            pl.BlockSpec((1, 2 * HY_WIDTH), lambda o: (0, o)),
            full((2, HY_FFN)), full((L, HY_WIDTH)),
        ],
        out_specs=[pl.BlockSpec((L, HY_WIDTH), lambda o: (0, o)),
                   pl.BlockSpec((L, HY_WIDTH), lambda o: (0, o))],
        out_shape=[out, out],
        compiler_params=_cparams("parallel"),
        name="hyena_filter_mlp",
    )(z_pad, w1p, b1.reshape(1, -1), w2, b2.reshape(1, -1), w3, b3.reshape(1, -1), freq, window)


def _spec_kernel(fm_ref, lo_ref, hi_ref, h_ref, *, n):
    P = fm_ref.shape[1]
    row = lax.broadcasted_iota(jnp.int32, (P, hi_ref.shape[1]), 0)
    sign = jnp.where(row % 2 == 0, 1.0, -1.0)
    scale = jnp.where(row == 0, 1.0 / n, 2.0 / n)
    for half in range(2):
        h_ref[0, half] = (_dot(fm_ref[half], hi_ref[...])
                          + sign * _dot(fm_ref[half], lo_ref[...])) * scale


def hyena_spectra(fm, f_all):
    P = fm.shape[1]
    W = f_all.shape[1]
    n_lag = f_all.shape[0] // P - 1
    return pl.pallas_call(
        functools.partial(_spec_kernel, n=2 * P),
        grid=(n_lag,),
        in_specs=[
            pl.BlockSpec((2, P, P), lambda i: (0, 0, 0)),
            pl.BlockSpec((P, W), lambda i: (i, 0)),
            pl.BlockSpec((P, W), lambda i: (i + 1, 0)),
        ],
        out_specs=pl.BlockSpec((1, 2, P, W), lambda i: (i, 0, 0, 0)),
        out_shape=jax.ShapeDtypeStruct((n_lag, 2, P, W), F32),
        compiler_params=_cparams("parallel"),
        name="hyena_filter_spectrum",
    )(fm, f_all, f_all)


HY_BLOCK = 512
HY_CONV_ROWS = 32


def _lconv_kernel(fm_ref, g_ref, u_ref, gate_ref, skip_ref, h_ref, o_ref, x_ref, yt_ref, yb_ref):
    P = fm_ref.shape[1]
    nb = u_ref.shape[1] // P
    ct = u_ref.shape[2]
    for j in range(nb):
        uj = u_ref[0, j * P:(j + 1) * P, :]
        x_ref[j, 0] = _dot(fm_ref[0], uj)
        x_ref[j, 1] = _dot(fm_ref[1], uj)
    first = lax.broadcasted_iota(jnp.int32, (HY_CONV_ROWS, ct), 0) == 0
    for i in range(nb):
        for r in range(P // HY_CONV_ROWS):
            rows = slice(r * HY_CONV_ROWS, (r + 1) * HY_CONV_ROWS)
            at = jnp.zeros((HY_CONV_ROWS, ct), F32)
            ab = jnp.zeros((HY_CONV_ROWS, ct), F32)
            for j in range(nb):
                d = i - j + nb - 1
                xt, xb = x_ref[j, 0, rows, :], x_ref[j, 1, rows, :]
                ht, hb = h_ref[d, 0, rows, :], h_ref[d, 1, rows, :]
                bb = xb * hb
                if r == 0:
                    at += xt * ht - jnp.where(first, 0.0, bb)
                    ab += jnp.where(first, bb, xt * hb + xb * ht)
                else:
                    at += xt * ht - bb
                    ab += xt * hb + xb * ht
            yt_ref[i, rows, :] = at.astype(BF16)
            yb_ref[i, rows, :] = ab.astype(BF16)
        y = _dot(g_ref[:, :P], yt_ref[i]) + _dot(g_ref[:, P:], yb_ref[i])
        blk = slice(i * P, (i + 1) * P)
        ui = u_ref[0, blk, :].astype(F32)
        o_ref[0, blk, :] = ((y + ui * skip_ref[...]) * gate_ref[0, blk, :].astype(F32)
                            ).astype(o_ref.dtype)


def long_conv(fm, g, spec, order, u, u_col, gate, gate_col, skip, ct=256):
    B, L, _ = u.shape
    P = fm.shape[1]
    nb = L // P
    n_lag = spec.shape[0]
    per = HY_WIDTH // ct
    return pl.pallas_call(
        _lconv_kernel,
        grid=(per, B),
        in_specs=[
            pl.BlockSpec((2, P, P), lambda c, b: (0, 0, 0)),
            pl.BlockSpec((P, 2 * P), lambda c, b: (0, 0)),
            pl.BlockSpec((1, L, ct), lambda c, b: (b, 0, u_col * per + c)),
            pl.BlockSpec((1, L, ct), lambda c, b: (b, 0, gate_col * per + c)),
            pl.BlockSpec((1, ct), lambda c, b: (0, c)),
            pl.BlockSpec((n_lag, 2, P, ct), lambda c, b: (0, 0, 0, order * per + c)),
        ],
        out_specs=pl.BlockSpec((1, L, ct), lambda c, b: (b, 0, c)),
        out_shape=jax.ShapeDtypeStruct((B, L, HY_WIDTH), BF16),
        scratch_shapes=[pltpu.VMEM((nb, 2, P, ct), F32), pltpu.VMEM((nb, P, ct), BF16),
                        pltpu.VMEM((nb, P, ct), BF16)],
        compiler_params=_cparams("parallel", "parallel"),
        name="hyena_long_conv",
    )(fm, g, u, gate, skip.reshape(1, HY_WIDTH).astype(F32), spec)


def hyena_mixer(p3, conv_w, spec, skip, fm, g):
    uc = short_conv(p3, conv_w.astype(F32))
    z = long_conv(fm, g, spec, 0, uc, 0, uc, 1, skip[0])
    return long_conv(fm, g, spec, 1, z, 0, uc, 2, skip[1])


def _log_sigmoid(x):
    return -(jnp.maximum(-x, 0.0) + jnp.log(1.0 + jnp.exp(-jnp.abs(x))))


def _dot_hilo(m, x):
    hi = x.astype(BF16)
    lo = (x - hi.astype(F32)).astype(BF16)
    return _dot(m, hi) + _dot(m, lo)


GLA_PREP_ROWS = 256


def _gla_kernel(q_ref, k_ref, v_ref, og_ref, lr_ref, wlr_ref, blr_ref, on_ref, o_ref,
                qs_ref, ki_ref, ks_ref, dec_ref, acc_ref):
    L = q_ref.shape[1]
    C = GLA_CHUNK
    H = GLA_HEADS
    RB = GLA_PREP_ROWS
    n_chunks = L // C
    KW = GLA_K_W

    pr = lax.broadcasted_iota(jnp.int32, (RB, RB), 0)
    pc = lax.broadcasted_iota(jnp.int32, (RB, RB), 1)
    same = (pr // C) == (pc // C)
    cum_f = jnp.where(same & (pc <= pr), 1.0, 0.0).astype(BF16)
    cum_b = jnp.where(same & (pc >= pr), 1.0, 0.0).astype(BF16)
    tot_m = jnp.where(same, 1.0, 0.0).astype(BF16)
    lane_head = lax.broadcasted_iota(jnp.int32, (C, KW), 1) // GLA_DK

    def prep(i, carry):
        r0 = pl.multiple_of(i * RB, RB)
        logit = _dot(lr_ref[0, pl.ds(r0, RB), :], wlr_ref[...]) + blr_ref[...]
        g = _log_sigmoid(logit) * (1.0 / GLA_NORMALIZER)
        q = q_ref[0, pl.ds(r0, RB), :].astype(F32) * (GLA_DK ** -0.5)
        k = k_ref[0, pl.ds(r0, RB), :].astype(F32)
        for d, cum in ((0, cum_f), (1, cum_b)):
            gd = g[:, d * KW:(d + 1) * KW]
            b = _dot_hilo(cum, gd)
            tot = _dot_hilo(tot_m, gd)
            qd = q * jnp.exp(b)
            ki_ref[d, pl.ds(r0, RB), :] = (k * jnp.exp(-b)).astype(BF16)
            ks_ref[d, pl.ds(r0, RB), :] = (k * jnp.exp(tot - b)).astype(BF16)
            dec = jnp.exp(tot)
            for c in range(RB // C):
                n = i * (RB // C) + c
                dec_ref[d, pl.ds(n, 1), :] = dec[c * C:c * C + 1, :]
                qc = qd[c * C:(c + 1) * C, :]
                for h in range(H):
                    row = pl.multiple_of((n * H + h) * C, C)
                    qs_ref[d, pl.ds(row, C), :] = jnp.where(lane_head == h, qc, 0.0).astype(BF16)
        return carry

    lax.fori_loop(0, L // RB, prep, 0)

    st_lane_head = lax.broadcasted_iota(jnp.int32, (GLA_DV, KW), 1) // GLA_DK
    arow = lax.broadcasted_iota(jnp.int32, (H * C, C), 0) % C
    acol = lax.broadcasted_iota(jnp.int32, (H * C, C), 1)
    amasks = (acol <= arow, acol > arow)

    def body(it, sts):
        new = []
        for d in (0, 1):
            st = sts[d]
            n = it if d == 0 else n_chunks - 1 - it
            r0 = pl.multiple_of(n * C, C)
            rq = pl.multiple_of(n * (H * C), H * C)
            qs = qs_ref[d, pl.ds(rq, H * C), :]
            ki = ki_ref[d, pl.ds(r0, C), :]
            ks = ks_ref[d, pl.ds(r0, C), :]
            v = v_ref[0, pl.ds(r0, C), :]
            a = jnp.where(amasks[d], _dot_nt(qs, ki), 0.0).astype(BF16)
            inter = _dot_nt(qs, st.astype(BF16))
            outs = []
            for h in range(H):
                oh = _dot(a[h * C:(h + 1) * C, :], v[:, h * GLA_DV:(h + 1) * GLA_DV])
                outs.append(oh + inter[h * C:(h + 1) * C, :])
            acc_ref[d, pl.ds(r0, C), :] = jnp.concatenate(outs, axis=-1)
            dst = _dot_tn(v, ks)
            upd = jnp.zeros((GLA_DV, KW), F32)
            for h in range(H):
                upd = jnp.where(st_lane_head == h, dst[h * GLA_DV:(h + 1) * GLA_DV, :], upd)
            new.append(st * dec_ref[d, pl.ds(n, 1), :] + upd)
        return tuple(new)

    zero = jnp.zeros((GLA_DV, KW), F32)
    lax.fori_loop(0, n_chunks, body, (zero, zero), unroll=2)

    def finalize(i, carry):
        r0 = pl.multiple_of(i * RB, RB)
        o = acc_ref[0, pl.ds(r0, RB), :] + acc_ref[1, pl.ds(r0, RB), :]
        og = og_ref[0, pl.ds(r0, RB), :].astype(F32)
        outs = []
        for h in range(H):
            oh = o[:, h * GLA_DV:(h + 1) * GLA_DV]
            ms = jnp.mean(oh * oh, axis=-1, keepdims=True)
            outs.append(oh * lax.rsqrt(ms + EPS) * on_ref[...])
        y = jnp.concatenate(outs, axis=-1) * (og * _sigmoid(og))
        o_ref[0, pl.ds(r0, RB), :] = y.astype(o_ref.dtype)
        return carry

    lax.fori_loop(0, L // RB, finalize, 0)


def gla_mixer(p3, w_lr, b_lr, onorm):
    B, L, _ = p3.shape
    lr_w = 256
    wl = jnp.zeros((lr_w, 2 * GLA_K_W), F32)
    wl = wl.at[0:GLA_RANK, 0:GLA_K_W].set(w_lr[0].astype(F32))
    wl = wl.at[GLA_RANK:2 * GLA_RANK, GLA_K_W:].set(w_lr[1].astype(F32))
    bl = b_lr.astype(F32).reshape(1, 2 * GLA_K_W)
    col = lambda width, c: pl.BlockSpec((1, L, width), lambda b: (b, 0, c // width))
    return pl.pallas_call(
        _gla_kernel,
        grid=(B,),
        in_specs=[
            col(GLA_K_W, COL_GQ), col(GLA_K_W, COL_GK), col(GLA_V_W, COL_GV),
            col(GLA_V_W, COL_GO), col(lr_w, COL_LR),
            pl.BlockSpec((lr_w, 2 * GLA_K_W), lambda b: (0, 0)),
            pl.BlockSpec((1, 2 * GLA_K_W), lambda b: (0, 0)),
            pl.BlockSpec((1, GLA_DV), lambda b: (0, 0)),
        ],
        out_specs=pl.BlockSpec((1, L, GLA_V_W), lambda b: (b, 0, 0)),
        out_shape=jax.ShapeDtypeStruct((B, L, GLA_V_W), BF16),
        scratch_shapes=[
            pltpu.VMEM((2, L * GLA_HEADS, GLA_K_W), BF16),
            pltpu.VMEM((2, L, GLA_K_W), BF16),
            pltpu.VMEM((2, L, GLA_K_W), BF16),
            pltpu.VMEM((2, L // GLA_CHUNK, GLA_K_W), F32),
            pltpu.VMEM((2, L, GLA_V_W), F32),
        ],
        compiler_params=_cparams("parallel"),
        name="gla_mixer",
    )(p3, p3, p3, p3, p3, wl.astype(BF16), bl, onorm.astype(F32).reshape(1, GLA_DV))


def _merge_kernel(a_ref, b_ref, c_ref, ga_ref, gb_ref, gc_ref, h_ref, wa_ref, wb_ref, wc_ref,
                  wo_ref, o_ref):
    mixed = (_sigmoid(ga_ref[...].astype(F32)) * _dot(a_ref[...], wa_ref[...])
             + _sigmoid(gb_ref[...].astype(F32)) * _dot(b_ref[...], wb_ref[...])
             + _sigmoid(gc_ref[...].astype(F32)) * _dot(c_ref[...], wc_ref[...]))
    o_ref[...] = h_ref[...] + _dot(mixed.astype(BF16), wo_ref[...])


def merge_out(ya, yb, yc, p2, h, wa, wb, wc, wo, tm=512):
    M, D = h.shape
    tm = _tile(M, tm)
    br = lambda w: pl.BlockSpec((tm, w), lambda i: (i, 0))
    gate = lambda c: pl.BlockSpec((tm, D), lambda i: (i, COL_GATE // D + c))
    wfull = lambda r: pl.BlockSpec((r, D), lambda i: (0, 0))
    return pl.pallas_call(
        _merge_kernel,
        grid=(M // tm,),
        in_specs=[br(ATT_Q_W), br(HY_WIDTH), br(GLA_V_W), gate(0), gate(1), gate(2),
                  pl.BlockSpec((tm, D), lambda i: (i, 0)),
                  wfull(ATT_Q_W), wfull(HY_WIDTH), wfull(GLA_V_W), wfull(D)],
        out_specs=pl.BlockSpec((tm, D), lambda i: (i, 0)),
        out_shape=jax.ShapeDtypeStruct((M, D), F32),
        compiler_params=_cparams("parallel"),
        name="gated_merge_out_proj",
    )(ya, yb, yc, p2, p2, p2, h, wa, wb, wc, wo)


def _head_rmsnorm(x, g):
    ms = jnp.mean(x * x, axis=-1, keepdims=True)
    return x * lax.rsqrt(ms + EPS) * g


def _xattn_kernel(h_ref, ln_ref, wq_ref, k_ref, v_ref, qn_ref, kn_ref, wo_ref, o_ref, att_ref):
    h = h_ref[0]
    hn = _head_rmsnorm(h, ln_ref[...]).astype(BF16)
    q = _dot(hn, wq_ref[...])
    for hd in range(X_HEADS):
        sl = slice(hd * X_HEAD_DIM, (hd + 1) * X_HEAD_DIM)
        qh = (_head_rmsnorm(q[:, sl], qn_ref[...]) * (X_HEAD_DIM ** -0.5)).astype(BF16)
        kh = _head_rmsnorm(k_ref[0, :, sl].astype(F32), kn_ref[...]).astype(BF16)
        s = _dot_nt(qh, kh)
        p = jnp.exp(s - jnp.max(s, axis=-1, keepdims=True))
        l = jnp.sum(p, axis=-1, keepdims=True)
        att_ref[:, sl] = (_dot(p.astype(BF16), v_ref[0, :, sl]) / l).astype(BF16)
    o_ref[0] = h + _dot(att_ref[...], wo_ref[...])


def cross_attention(h3, ln, wq, kv3, qn, kn, wo, tl=512):
    B, L, D = h3.shape
    Mm = kv3.shape[1]
    tl = _tile(L, tl)
    vec = lambda w: pl.BlockSpec((1, w), lambda b, i: (0, 0))
    mat = pl.BlockSpec((D, D), lambda b, i: (0, 0))
    return pl.pallas_call(
        _xattn_kernel,
        grid=(B, L // tl),
        in_specs=[
            pl.BlockSpec((1, tl, D), lambda b, i: (b, i, 0)),
            vec(D), mat,
            pl.BlockSpec((1, Mm, D), lambda b, i: (b, 0, 0)),
            pl.BlockSpec((1, Mm, D), lambda b, i: (b, 0, 1)),
            vec(X_HEAD_DIM), vec(X_HEAD_DIM), mat,
        ],
        out_specs=pl.BlockSpec((1, tl, D), lambda b, i: (b, i, 0)),
        out_shape=jax.ShapeDtypeStruct((B, L, D), F32),
        scratch_shapes=[pltpu.VMEM((tl, D), BF16)],
        compiler_params=_cparams("parallel", "parallel"),
        name="memory_cross_attention",
    )(h3, ln.astype(F32).reshape(1, D), wq, kv3, kv3, qn.astype(F32).reshape(1, X_HEAD_DIM),
      kn.astype(F32).reshape(1, X_HEAD_DIM), wo)


def _mlp_kernel(h_ref, g_ref, w1_ref, w2_ref, o_ref, hn_ref, acc_ref):
    j = pl.program_id(1)

    @pl.when(j == 0)
    def _():
        hn_ref[...] = _head_rmsnorm(h_ref[...], g_ref[...]).astype(BF16)
        acc_ref[...] = jnp.zeros_like(acc_ref)

    a = jnp.maximum(_dot(hn_ref[...], w1_ref[...]), 0.0)
    acc_ref[...] += _dot((a * a).astype(BF16), w2_ref[...])

    @pl.when(j == pl.num_programs(1) - 1)
    def _():
        o_ref[...] = h_ref[...] + acc_ref[...]


def mlp(h, g, w1, w2, tm=1024, tf=1024):
    M, D = h.shape
    F = w1.shape[1]
    tm, tf = _tile(M, tm), _tile(F, tf)
    return pl.pallas_call(
        _mlp_kernel,
        grid=(M // tm, F // tf),
        in_specs=[
            pl.BlockSpec((tm, D), lambda i, j: (i, 0)),
            pl.BlockSpec((1, D), lambda i, j: (0, 0)),
            pl.BlockSpec((D, tf), lambda i, j: (0, j)),
            pl.BlockSpec((tf, D), lambda i, j: (j, 0)),
        ],
        out_specs=pl.BlockSpec((tm, D), lambda i, j: (i, 0)),
        out_shape=jax.ShapeDtypeStruct((M, D), F32),
        scratch_shapes=[pltpu.VMEM((tm, D), BF16), pltpu.VMEM((tm, D), F32)],
        compiler_params=_cparams("parallel", "arbitrary"),
        name="relu2_mlp",
    )(h, g.astype(F32).reshape(1, D), w1, w2)


def _pack_w_in(w):
    w = w.astype(BF16)
    s = [0, ATT_Q_W, ATT_Q_W + ATT_KV_W, ATT_Q_W + 2 * ATT_KV_W]
    aq, ak, av = w[:, s[0]:s[1]], w[:, s[1]:s[2]], w[:, s[2]:s[3]]
    o = s[3]
    hy = w[:, o:o + 3 * HY_WIDTH]
    o += 3 * HY_WIDTH
    gq, gk = w[:, o:o + GLA_K_W], w[:, o + GLA_K_W:o + 2 * GLA_K_W]
    o += 2 * GLA_K_W
    gv, go = w[:, o:o + GLA_V_W], w[:, o + GLA_V_W:o + 2 * GLA_V_W]
    o += 2 * GLA_V_W
    lr = w[:, o:o + 2 * GLA_RANK]
    o += 2 * GLA_RANK
    gates = w[:, o:]
    pad = jnp.zeros((w.shape[0], COL_GATE - COL_LR - 2 * GLA_RANK), w.dtype)
    packed = jnp.concatenate([aq, hy, gq, gk, gv, go, ak, av, lr, pad, gates], axis=1)
    assert packed.shape[1] == N_PACK
    return packed.astype(BF16)


def kernel(x, mem, ln_mix, w_in, attn_qnorm, attn_knorm, hy_conv, hy_w1, hy_b1, hy_w2, hy_b2,
           hy_w3, hy_b3, hy_freq, hy_skip, gla_w_lr, gla_b_lr, gla_onorm, w_br_attn, w_br_hyena,
           w_br_gla, w_out, ln_x, ln_mem, x_wq, x_wk, x_wv, x_wo, x_qnorm, x_knorm, ln_mlp,
           mlp_w1, mlp_w2):
    B, L, D = x.shape
    Mm = mem.shape[1]
    depth = w_in.shape[0]
    M = B * L

    rope_tabs = _rope_tables(L)
    z_pos, window = _hyena_pos_features(L)
    z_pad_w = 128
    z_pad = jnp.pad(z_pos, ((0, 0), (0, z_pad_w - HY_EMB)))
    fm, g_inv = _dft_tables(min(L, HY_BLOCK))
    bf = lambda a: a.astype(BF16)

    h = x.astype(F32).reshape(M, D)
    mem2 = mem.astype(F32).reshape(B * Mm, D)
    for i in range(depth):
        p2 = norm_matmul(h, ln_mix[i].astype(F32), _pack_w_in(w_in[i]), BF16)
        p3 = p2.reshape(B, L, N_PACK)
        y_a = attention(p3, attn_qnorm[i], attn_knorm[i], rope_tabs)
        w1p = jnp.pad(hy_w1[i].astype(F32), ((0, z_pad_w - HY_EMB), (0, 0)))
        fa, fb = hyena_filters(z_pad, window, w1p, hy_b1[i].astype(F32), hy_w2[i].astype(F32),
                               hy_b2[i].astype(F32), hy_w3[i].astype(F32),
                               hy_b3[i].astype(F32), hy_freq[i].astype(F32))
        f_all = jnp.concatenate([jnp.zeros((1, fa.shape[1]), fa.dtype),
                                 jnp.flip(fb, axis=0)[:L - 1], fa], axis=0)
        spec = hyena_spectra(fm, f_all)
        y_b = hyena_mixer(p3, hy_conv[i], spec, hy_skip[i], fm, g_inv)
        y_c = gla_mixer(p3, gla_w_lr[i], gla_b_lr[i], gla_onorm[i])
        h = merge_out(y_a.reshape(M, ATT_Q_W), y_b.reshape(M, HY_WIDTH), y_c.reshape(M, GLA_V_W),
                      p2, h, bf(w_br_attn[i]), bf(w_br_hyena[i]), bf(w_br_gla[i]), bf(w_out[i]))
        wkv = jnp.concatenate([x_wk[i], x_wv[i]], axis=1)
        kv = norm_matmul(mem2, ln_mem[i].astype(F32), bf(wkv), BF16).reshape(B, Mm, 2 * D)
        h = cross_attention(h.reshape(B, L, D), ln_x[i], bf(x_wq[i]), kv, x_qnorm[i],
                            x_knorm[i], bf(x_wo[i])).reshape(M, D)
        h = mlp(h, ln_mlp[i], bf(mlp_w1[i]), bf(mlp_w2[i]))
    return h.reshape(B, L, D).astype(x.dtype)
```

```python
import functools
import math

import jax
import jax.numpy as jnp
from jax import lax
from jax.experimental import pallas as pl
from jax.experimental.pallas import tpu as pltpu

F32 = jnp.float32
BF16 = jnp.bfloat16

D_MODEL = 1024
GRID_W = 64
ROPE_THETA = 10000.0
HEAD_DIM = 64
ATT_Q_HEADS = 8
ATT_KV_HEADS = 2
ATT_Q_W = ATT_Q_HEADS * HEAD_DIM
ATT_KV_W = ATT_KV_HEADS * HEAD_DIM
HY_WIDTH = 512
HY_ORDER = 2
HY_BANDS = 16
HY_EMB = 1 + 2 * HY_BANDS
HY_FFN = 64
HY_FAST_DECAY = 0.3
HY_SLOW_DECAY = 1.5
HY_TARGET = 1e-2
GLA_HEADS = 4
GLA_DK = 64
GLA_DV = 128
GLA_RANK = 16
GLA_NORMALIZER = 16.0
GLA_CHUNK = 64
GLA_K_W = GLA_HEADS * GLA_DK
GLA_V_W = GLA_HEADS * GLA_DV
X_HEADS = 4
X_HEAD_DIM = D_MODEL // X_HEADS
D_FF = 4 * D_MODEL
N_BRANCH = 3
EPS = 1e-6

COL_AQ = 0
COL_HY = 512
COL_GQ = 2048
COL_GK = 2304
COL_GV = 2560
COL_GO = 3072
COL_AK = 3584
COL_AV = 3712
COL_LR = 3840
COL_GATE = 4096
N_PACK = 7168

VMEM_LIMIT_BYTES = 52 * 1024 * 1024


def _cparams(*sem):
    return pltpu.CompilerParams(dimension_semantics=sem, vmem_limit_bytes=VMEM_LIMIT_BYTES)


def _tile(n, t):
    t = min(n, t)
    assert n % t == 0, (n, t)
    return t


def _dot(a, b):
    return jnp.dot(a, b, preferred_element_type=F32)


def _dot_nt(a, b):
    return lax.dot_general(a, b, (((1,), (1,)), ((), ())), preferred_element_type=F32)


def _dot_tn(a, b):
    return lax.dot_general(a, b, (((0,), (0,)), ((), ())), preferred_element_type=F32)


def _sigmoid(x):
    return 1.0 / (1.0 + jnp.exp(-x))


def _norm_mm_kernel(x_ref, g_ref, w_ref, o_ref, xn_ref):
    @pl.when(pl.program_id(1) == 0)
    def _():
        x = x_ref[...]
        ms = jnp.mean(x * x, axis=-1, keepdims=True)
        xn_ref[...] = (x * lax.rsqrt(ms + EPS) * g_ref[...]).astype(BF16)

    o_ref[...] = _dot(xn_ref[...], w_ref[...]).astype(o_ref.dtype)


def norm_matmul(x, g, w, out_dtype, tm=1024, tn=1024):
    M, K = x.shape
    N = w.shape[1]
    tm, tn = _tile(M, tm), _tile(N, tn)
    return pl.pallas_call(
        _norm_mm_kernel,
        grid=(M // tm, N // tn),
        in_specs=[
            pl.BlockSpec((tm, K), lambda i, j: (i, 0)),
            pl.BlockSpec((1, K), lambda i, j: (0, 0)),
            pl.BlockSpec((K, tn), lambda i, j: (0, j)),
        ],
        out_specs=pl.BlockSpec((tm, tn), lambda i, j: (i, j)),
        out_shape=jax.ShapeDtypeStruct((M, N), out_dtype),
        scratch_shapes=[pltpu.VMEM((tm, K), BF16)],
        compiler_params=_cparams("parallel", "arbitrary"),
        name="norm_matmul",
    )(x, g.reshape(1, K), w)


def _group_mean_sq(x, gm_ref):
    return _dot((x * x).astype(BF16), gm_ref[...])


def _rope(x, c_ref, sa_ref, sb_ref):
    w = x.shape[-1]
    return (x * c_ref[...] + pltpu.roll(x, w - HEAD_DIM // 4, 1) * sa_ref[...]
            + pltpu.roll(x, HEAD_DIM // 4, 1) * sb_ref[...])


def _attn_kernel(q_ref, k_ref, v_ref, qn_ref, kn_ref, cq_ref, saq_ref, sbq_ref,
                 ck_ref, sak_ref, sbk_ref, gmq_ref, gmk_ref, o_ref, kp_ref, vx_ref):
    @pl.when(pl.program_id(1) == 0)
    def _():
        k = k_ref[0].astype(F32)
        kh = k * lax.rsqrt(_group_mean_sq(k, gmk_ref) + EPS) * kn_ref[...]
        kp_ref[...] = _rope(kh, ck_ref, sak_ref, sbk_ref).astype(BF16)
        v = v_ref[0]
        ones = jnp.ones((v.shape[0], HEAD_DIM), BF16)
        vx_ref[...] = jnp.concatenate(
            [piece for kv in range(ATT_KV_HEADS)
             for piece in (v[:, kv * HEAD_DIM:(kv + 1) * HEAD_DIM], ones)], axis=-1)

    q = q_ref[0].astype(F32)
    qh = q * lax.rsqrt(_group_mean_sq(q, gmq_ref) + EPS) * qn_ref[...]
    qb = (_rope(qh, cq_ref, saq_ref, sbq_ref) * (HEAD_DIM ** -0.5 * math.log2(math.e))).astype(BF16)
    group = ATT_Q_HEADS // ATT_KV_HEADS
    outs = []
    for h in range(ATT_Q_HEADS):
        kv = h // group
        kk = kp_ref[:, kv * HEAD_DIM:(kv + 1) * HEAD_DIM]
        s = _dot_nt(qb[:, h * HEAD_DIM:(h + 1) * HEAD_DIM], kk)
        p = jnp.exp2(s - jnp.max(s, axis=-1, keepdims=True))
        ox = _dot(p.astype(BF16), vx_ref[:, kv * 2 * HEAD_DIM:(kv + 1) * 2 * HEAD_DIM])
        outs.append(ox[:, :HEAD_DIM] / ox[:, HEAD_DIM:])
    o_ref[0] = jnp.concatenate(outs, axis=-1).astype(o_ref.dtype)


def _rope_tables(L):
    rows = L // GRID_W
    r, c = jnp.meshgrid(jnp.arange(rows), jnp.arange(GRID_W), indexing="ij")
    n_freq = HEAD_DIM // 4
    inv = ROPE_THETA ** (-jnp.arange(n_freq, dtype=F32) / n_freq)
    pos = jnp.stack([r.reshape(-1), c.reshape(-1)], axis=1).astype(F32)
    ang = pos[:, :, None] * inv
    cos, sin = jnp.cos(ang), jnp.sin(ang)
    zero = jnp.zeros_like(sin)
    c64 = jnp.concatenate([cos, cos], axis=-1).reshape(L, HEAD_DIM)
    sa64 = jnp.concatenate([-sin, zero], axis=-1).reshape(L, HEAD_DIM)
    sb64 = jnp.concatenate([zero, sin], axis=-1).reshape(L, HEAD_DIM)
    return c64, sa64, sb64


def attention(p3, qn, kn, tabs, tq=256):
    B, L, _ = p3.shape
    tq = _tile(L, tq)
    c64, sa64, sb64 = tabs
    tq_tabs = [jnp.tile(t, (1, ATT_Q_HEADS)) for t in (c64, sa64, sb64)]
    tk_tabs = [jnp.tile(t, (1, ATT_KV_HEADS)) for t in (c64, sa64, sb64)]
    qn_t = jnp.tile(qn.astype(F32), ATT_Q_HEADS).reshape(1, ATT_Q_W)
    kn_t = jnp.tile(kn.astype(F32), ATT_KV_HEADS).reshape(1, ATT_KV_W)

    def group_mean(width):
        g = jnp.arange(width) // HEAD_DIM
        return jnp.where(g[:, None] == g[None, :], 1.0 / HEAD_DIM, 0.0).astype(BF16)

    qtab = pl.BlockSpec((tq, ATT_Q_W), lambda b, i: (i, 0))
    ktab = pl.BlockSpec((L, ATT_KV_W), lambda b, i: (0, 0))
    return pl.pallas_call(
        _attn_kernel,
        grid=(B, L // tq),
        in_specs=[
            pl.BlockSpec((1, tq, ATT_Q_W), lambda b, i: (b, i, COL_AQ // ATT_Q_W)),
            pl.BlockSpec((1, L, ATT_KV_W), lambda b, i: (b, 0, COL_AK // ATT_KV_W)),
            pl.BlockSpec((1, L, ATT_KV_W), lambda b, i: (b, 0, COL_AV // ATT_KV_W)),
            pl.BlockSpec((1, ATT_Q_W), lambda b, i: (0, 0)),
            pl.BlockSpec((1, ATT_KV_W), lambda b, i: (0, 0)),
            qtab, qtab, qtab, ktab, ktab, ktab,
            pl.BlockSpec((ATT_Q_W, ATT_Q_W), lambda b, i: (0, 0)),
            pl.BlockSpec((ATT_KV_W, ATT_KV_W), lambda b, i: (0, 0)),
        ],
        out_specs=pl.BlockSpec((1, tq, ATT_Q_W), lambda b, i: (b, i, 0)),
        out_shape=jax.ShapeDtypeStruct((B, L, ATT_Q_W), BF16),
        scratch_shapes=[pltpu.VMEM((L, ATT_KV_W), BF16), pltpu.VMEM((L, 2 * ATT_KV_W), BF16)],
        compiler_params=_cparams("parallel", "arbitrary"),
        name="gqa_attention",
    )(p3, p3, p3, qn_t, kn_t, *tq_tabs, *tk_tabs, group_mean(ATT_Q_W), group_mean(ATT_KV_W))


def _dft_tables(L):
    n = 2 * L
    k = jnp.arange(L, dtype=jnp.int32)[:, None]
    j = jnp.arange(L, dtype=jnp.int32)[None, :]
    step = 64
    ja = jnp.arange(0, L, step, dtype=jnp.int32)[None, :]
    jb = jnp.arange(step, dtype=jnp.int32)[None, :]
    ang_a = ((k * ja) % n).astype(F32) * (2.0 * math.pi / n)
    ang_b = ((k * jb) % n).astype(F32) * (2.0 * math.pi / n)
    ca, sa = jnp.cos(ang_a)[:, :, None], jnp.sin(ang_a)[:, :, None]
    cb, sb = jnp.cos(ang_b)[:, None, :], jnp.sin(ang_b)[:, None, :]
    c = (ca * cb - sa * sb).reshape(L, L)
    s = -(sa * cb + ca * sb).reshape(L, L)
    bottom = jnp.where(k == 0, jnp.where(j % 2 == 0, 1.0, -1.0), s)
    bottom_t = jnp.where(j == 0, jnp.where(k % 2 == 0, 1.0, -1.0), s)
    fm = jnp.stack([c, bottom]).astype(BF16)
    g = jnp.concatenate([c, bottom_t], axis=1).astype(BF16)
    return fm, g


def _hyena_pos_features(L, t):
    t_norm = t / max(L - 1, 1)
    w = 2.0 * math.pi * t / L
    f = jnp.linspace(1e-4, HY_BANDS - 1, HY_BANDS, dtype=F32)
    fw = w[:, None] * f
    z = jnp.concatenate([t_norm[:, None], jnp.cos(fw), -jnp.sin(fw)], axis=-1)
    deltas = jnp.abs(jnp.linspace(math.log(HY_TARGET) / HY_FAST_DECAY,
                                  math.log(HY_TARGET) / HY_SLOW_DECAY, HY_WIDTH, dtype=F32))
    window = jnp.exp(-t_norm[:, None] * deltas)
    return z, window


def _sconv_kernel(u_ref, w_ref, o_ref):
    u = u_ref[0].astype(F32)
    L = u.shape[0]
    row = lax.broadcasted_iota(jnp.int32, u.shape, 0)
    prev = jnp.where(row == 0, 0.0, pltpu.roll(u, 1, 0))
    nxt = jnp.where(row == L - 1, 0.0, pltpu.roll(u, L - 1, 0))
    o_ref[0] = (prev * w_ref[0:1, :] + u * w_ref[1:2, :] + nxt * w_ref[2:3, :]).astype(o_ref.dtype)


def short_conv(p3, w):
    B, L, _ = p3.shape
    nblk = 3
    return pl.pallas_call(
        _sconv_kernel,
        grid=(B, nblk),
        in_specs=[
            pl.BlockSpec((1, L, HY_WIDTH), lambda b, c: (b, 0, COL_HY // HY_WIDTH + c)),
            pl.BlockSpec((3, HY_WIDTH), lambda b, c: (0, c)),
        ],
        out_specs=pl.BlockSpec((1, L, HY_WIDTH), lambda b, c: (b, 0, c)),
        out_shape=jax.ShapeDtypeStruct((B, L, nblk * HY_WIDTH), BF16),
        compiler_params=_cparams("parallel", "parallel"),
        name="hyena_short_conv",
    )(p3, w)


def _hyfilter_kernel(z_ref, w1_ref, b1_ref, w2_ref, b2_ref, w3_ref, b3_ref, fr_ref, win_ref,
                     f_ref):
    L = z_ref.shape[0] // 2
    hp = lax.Precision.HIGHEST
    h = jnp.sin(fr_ref[0:1, :] * (jnp.dot(z_ref[...], w1_ref[...], precision=hp,
                                          preferred_element_type=F32) + b1_ref[...]))
    h = jnp.sin(fr_ref[1:2, :] * (jnp.dot(h, w2_ref[...], precision=hp,
                                          preferred_element_type=F32) + b2_ref[...]))
    neg = (jnp.dot(h[:L], w3_ref[:, HY_WIDTH:], precision=hp, preferred_element_type=F32)
           + b3_ref[:, HY_WIDTH:]) * win_ref[:L, :]
    pos = (jnp.dot(h[L:], w3_ref[:, :HY_WIDTH], precision=hp, preferred_element_type=F32)
           + b3_ref[:, :HY_WIDTH]) * win_ref[L:, :]
    norm = (jnp.sum(jnp.abs(pos), axis=0, keepdims=True)
            + jnp.sum(jnp.abs(neg), axis=0, keepdims=True) + EPS)
    f_ref[:L, :] = (neg / norm).astype(f_ref.dtype)
    f_ref[L:, :] = (pos / norm).astype(f_ref.dtype)


def hyena_filters(z_pad, window, w1p, b1, w2, b2, w3, b3, freq):
    L2 = z_pad.shape[0]
    zp = z_pad.shape[1]
    full = lambda shape: pl.BlockSpec(shape, lambda o: (0,) * len(shape))
    return pl.pallas_call(
        _hyfilter_kernel,
        grid=(HY_ORDER,),
        in_specs=[
            full((L2, zp)), full((zp, HY_FFN)), full((1, HY_FFN)),
            full((HY_FFN, HY_FFN)), full((1, HY_FFN)),
            pl.BlockSpec((HY_FFN, 2 * HY_WIDTH), lambda o: (0, o)),
            pl.BlockSpec((1, 2 * HY_WIDTH), lambda o: (0, o)),
            full((2, HY_FFN)), full((L2, HY_WIDTH)),
        ],
        out_specs=pl.BlockSpec((L2, HY_WIDTH), lambda o: (0, o)),
        out_shape=jax.ShapeDtypeStruct((L2, HY_ORDER * HY_WIDTH), BF16),
        compiler_params=_cparams("parallel"),
        name="hyena_filter_mlp",
    )(z_pad, w1p, b1.reshape(1, -1), w2, b2.reshape(1, -1), w3, b3.reshape(1, -1), freq, window)


def _spec_kernel(fm_ref, lo_ref, hi_ref, h_ref, *, n):
    P = fm_ref.shape[1]
    row = lax.broadcasted_iota(jnp.int32, (P, hi_ref.shape[1]), 0)
    sign = jnp.where(row % 2 == 0, 1.0, -1.0)
    scale = jnp.where(row == 0, 1.0 / n, 2.0 / n)
    for half in range(2):
        h_ref[0, half] = (_dot(fm_ref[half], hi_ref[...])
                          + sign * _dot(fm_ref[half], lo_ref[...])) * scale


def hyena_spectra(fm, f_all):
    P = fm.shape[1]
    W = f_all.shape[1]
    n_lag = f_all.shape[0] // P - 1
    return pl.pallas_call(
        functools.partial(_spec_kernel, n=2 * P),
        grid=(n_lag,),
        in_specs=[
            pl.BlockSpec((2, P, P), lambda i: (0, 0, 0)),
            pl.BlockSpec((P, W), lambda i: (i, 0)),
            pl.BlockSpec((P, W), lambda i: (i + 1, 0)),
        ],
        out_specs=pl.BlockSpec((1, 2, P, W), lambda i: (i, 0, 0, 0)),
        out_shape=jax.ShapeDtypeStruct((n_lag, 2, P, W), F32),
        compiler_params=_cparams("parallel"),
        name="hyena_filter_spectrum",
    )(fm, f_all, f_all)


HY_BLOCK = 512
HY_CONV_ROWS = 32


def _lconv_kernel(fm_ref, g_ref, u_ref, gate_ref, skip_ref, h_ref, o_ref, x_ref, yt_ref, yb_ref):
    P = fm_ref.shape[1]
    nb = u_ref.shape[1] // P
    ct = u_ref.shape[2]
    for j in range(nb):
        uj = u_ref[0, j * P:(j + 1) * P, :]
        x_ref[j, 0] = _dot(fm_ref[0], uj)
        x_ref[j, 1] = _dot(fm_ref[1], uj)
    first = lax.broadcasted_iota(jnp.int32, (HY_CONV_ROWS, ct), 0) == 0
    for i in range(nb):
        for r in range(P // HY_CONV_ROWS):
            rows = slice(r * HY_CONV_ROWS, (r + 1) * HY_CONV_ROWS)
            at = jnp.zeros((HY_CONV_ROWS, ct), F32)
            ab = jnp.zeros((HY_CONV_ROWS, ct), F32)
            for j in range(nb):
                d = i - j + nb - 1
                xt, xb = x_ref[j, 0, rows, :], x_ref[j, 1, rows, :]
                ht, hb = h_ref[d, 0, rows, :], h_ref[d, 1, rows, :]
                bb = xb * hb
                if r == 0:
                    at += xt * ht - jnp.where(first, 0.0, bb)
                    ab += jnp.where(first, bb, xt * hb + xb * ht)
                else:
                    at += xt * ht - bb
                    ab += xt * hb + xb * ht
            yt_ref[i, rows, :] = at.astype(BF16)
            yb_ref[i, rows, :] = ab.astype(BF16)
        y = _dot(g_ref[:, :P], yt_ref[i]) + _dot(g_ref[:, P:], yb_ref[i])
        blk = slice(i * P, (i + 1) * P)
        ui = u_ref[0, blk, :].astype(F32)
        o_ref[0, blk, :] = ((y + ui * skip_ref[...]) * gate_ref[0, blk, :].astype(F32)
                            ).astype(o_ref.dtype)


def long_conv(fm, g, spec, order, u, u_col, gate, gate_col, skip, ct=256):
    B, L, _ = u.shape
    P = fm.shape[1]
    nb = L // P
    n_lag = spec.shape[0]
    per = HY_WIDTH // ct
    return pl.pallas_call(
        _lconv_kernel,
        grid=(per, B),
        in_specs=[
            pl.BlockSpec((2, P, P), lambda c, b: (0, 0, 0)),
            pl.BlockSpec((P, 2 * P), lambda c, b: (0, 0)),
            pl.BlockSpec((1, L, ct), lambda c, b: (b, 0, u_col * per + c)),
            pl.BlockSpec((1, L, ct), lambda c, b: (b, 0, gate_col * per + c)),
            pl.BlockSpec((1, ct), lambda c, b: (0, c)),
            pl.BlockSpec((n_lag, 2, P, ct), lambda c, b: (0, 0, 0, order * per + c)),
        ],
        out_specs=pl.BlockSpec((1, L, ct), lambda c, b: (b, 0, c)),
        out_shape=jax.ShapeDtypeStruct((B, L, HY_WIDTH), BF16),
        scratch_shapes=[pltpu.VMEM((nb, 2, P, ct), F32), pltpu.VMEM((nb, P, ct), BF16),
                        pltpu.VMEM((nb, P, ct), BF16)],
        compiler_params=_cparams("parallel", "parallel"),
        name="hyena_long_conv",
    )(fm, g, u, gate, skip.reshape(1, HY_WIDTH).astype(F32), spec)


def hyena_mixer(p3, conv_w, spec, skip, fm, g):
    uc = short_conv(p3, conv_w.astype(F32))
    z = long_conv(fm, g, spec, 0, uc, 0, uc, 1, skip[0])
    return long_conv(fm, g, spec, 1, z, 0, uc, 2, skip[1])


def _log_sigmoid(x):
    return -(jnp.maximum(-x, 0.0) + jnp.log(1.0 + jnp.exp(-jnp.abs(x))))


def _dot_hilo(m, x):
    hi = x.astype(BF16)
    lo = (x - hi.astype(F32)).astype(BF16)
    return _dot(m, hi) + _dot(m, lo)


GLA_PREP_ROWS = 256


def _gla_kernel(q_ref, k_ref, v_ref, og_ref, lr_ref, wlr_ref, blr_ref, on_ref, o_ref,
                qs_ref, ki_ref, ks_ref, dec_ref, acc_ref):
    L = q_ref.shape[1]
    C = GLA_CHUNK
    H = GLA_HEADS
    RB = GLA_PREP_ROWS
    n_chunks = L // C
    KW = GLA_K_W

    pr = lax.broadcasted_iota(jnp.int32, (RB, RB), 0)
    pc = lax.broadcasted_iota(jnp.int32, (RB, RB), 1)
    same = (pr // C) == (pc // C)
    cum_f = jnp.where(same & (pc <= pr), 1.0, 0.0).astype(BF16)
    cum_b = jnp.where(same & (pc >= pr), 1.0, 0.0).astype(BF16)
    tot_m = jnp.where(same, 1.0, 0.0).astype(BF16)
    lane_head = lax.broadcasted_iota(jnp.int32, (C, KW), 1) // GLA_DK

    def prep(i, carry):
        r0 = pl.multiple_of(i * RB, RB)
        logit = _dot(lr_ref[0, pl.ds(r0, RB), :], wlr_ref[...]) + blr_ref[...]
        g = _log_sigmoid(logit) * (1.0 / GLA_NORMALIZER)
        q = q_ref[0, pl.ds(r0, RB), :].astype(F32) * (GLA_DK ** -0.5)
        k = k_ref[0, pl.ds(r0, RB), :].astype(F32)
        for d, cum in ((0, cum_f), (1, cum_b)):
            gd = g[:, d * KW:(d + 1) * KW]
            b = _dot_hilo(cum, gd)
            tot = _dot_hilo(tot_m, gd)
            qd = q * jnp.exp(b)
            ki_ref[d, pl.ds(r0, RB), :] = (k * jnp.exp(-b)).astype(BF16)
            ks_ref[d, pl.ds(r0, RB), :] = (k * jnp.exp(tot - b)).astype(BF16)
            dec = jnp.exp(tot)
            for c in range(RB // C):
                n = i * (RB // C) + c
                dec_ref[d, pl.ds(n, 1), :] = dec[c * C:c * C + 1, :]
                qc = qd[c * C:(c + 1) * C, :]
                for h in range(H):
                    row = pl.multiple_of((n * H + h) * C, C)
                    qs_ref[d, pl.ds(row, C), :] = jnp.where(lane_head == h, qc, 0.0).astype(BF16)
        return carry

    lax.fori_loop(0, L // RB, prep, 0)

    st_lane_head = lax.broadcasted_iota(jnp.int32, (GLA_DV, KW), 1) // GLA_DK
    arow = lax.broadcasted_iota(jnp.int32, (H * C, C), 0) % C
    acol = lax.broadcasted_iota(jnp.int32, (H * C, C), 1)
    amasks = (acol <= arow, acol > arow)

    def body(it, sts):
        new = []
        for d in (0, 1):
            st = sts[d]
            n = it if d == 0 else n_chunks - 1 - it
            r0 = pl.multiple_of(n * C, C)
            rq = pl.multiple_of(n * (H * C), H * C)
            qs = qs_ref[d, pl.ds(rq, H * C), :]
            ki = ki_ref[d, pl.ds(r0, C), :]
            ks = ks_ref[d, pl.ds(r0, C), :]
            v = v_ref[0, pl.ds(r0, C), :]
            a = jnp.where(amasks[d], _dot_nt(qs, ki), 0.0).astype(BF16)
            inter = _dot_nt(qs, st.astype(BF16))
            outs = []
            for h in range(H):
                oh = _dot(a[h * C:(h + 1) * C, :], v[:, h * GLA_DV:(h + 1) * GLA_DV])
                outs.append(oh + inter[h * C:(h + 1) * C, :])
            acc_ref[d, pl.ds(r0, C), :] = jnp.concatenate(outs, axis=-1)
            dst = _dot_tn(v, ks)
            upd = jnp.zeros((GLA_DV, KW), F32)
            for h in range(H):
                upd = jnp.where(st_lane_head == h, dst[h * GLA_DV:(h + 1) * GLA_DV, :], upd)
            new.append(st * dec_ref[d, pl.ds(n, 1), :] + upd)
        return tuple(new)

    zero = jnp.zeros((GLA_DV, KW), F32)
    lax.fori_loop(0, n_chunks, body, (zero, zero), unroll=2)

    def finalize(i, carry):
        r0 = pl.multiple_of(i * RB, RB)
        o = acc_ref[0, pl.ds(r0, RB), :] + acc_ref[1, pl.ds(r0, RB), :]
        og = og_ref[0, pl.ds(r0, RB), :].astype(F32)
        outs = []
        for h in range(H):
            oh = o[:, h * GLA_DV:(h + 1) * GLA_DV]
            ms = jnp.mean(oh * oh, axis=-1, keepdims=True)
            outs.append(oh * lax.rsqrt(ms + EPS) * on_ref[...])
        y = jnp.concatenate(outs, axis=-1) * (og * _sigmoid(og))
        o_ref[0, pl.ds(r0, RB), :] = y.astype(o_ref.dtype)
        return carry

    lax.fori_loop(0, L // RB, finalize, 0)


def gla_mixer(p3, w_lr, b_lr, onorm):
    B, L, _ = p3.shape
    lr_w = 256
    wl = jnp.zeros((lr_w, 2 * GLA_K_W), F32)
    wl = wl.at[0:GLA_RANK, 0:GLA_K_W].set(w_lr[0].astype(F32))
    wl = wl.at[GLA_RANK:2 * GLA_RANK, GLA_K_W:].set(w_lr[1].astype(F32))
    bl = b_lr.astype(F32).reshape(1, 2 * GLA_K_W)
    col = lambda width, c: pl.BlockSpec((1, L, width), lambda b: (b, 0, c // width))
    return pl.pallas_call(
        _gla_kernel,
        grid=(B,),
        in_specs=[
            col(GLA_K_W, COL_GQ), col(GLA_K_W, COL_GK), col(GLA_V_W, COL_GV),
            col(GLA_V_W, COL_GO), col(lr_w, COL_LR),
            pl.BlockSpec((lr_w, 2 * GLA_K_W), lambda b: (0, 0)),
            pl.BlockSpec((1, 2 * GLA_K_W), lambda b: (0, 0)),
            pl.BlockSpec((1, GLA_DV), lambda b: (0, 0)),
        ],
        out_specs=pl.BlockSpec((1, L, GLA_V_W), lambda b: (b, 0, 0)),
        out_shape=jax.ShapeDtypeStruct((B, L, GLA_V_W), BF16),
        scratch_shapes=[
            pltpu.VMEM((2, L * GLA_HEADS, GLA_K_W), BF16),
            pltpu.VMEM((2, L, GLA_K_W), BF16),
            pltpu.VMEM((2, L, GLA_K_W), BF16),
            pltpu.VMEM((2, L // GLA_CHUNK, GLA_K_W), F32),
            pltpu.VMEM((2, L, GLA_V_W), F32),
        ],
        compiler_params=_cparams("parallel"),
        name="gla_mixer",
    )(p3, p3, p3, p3, p3, wl.astype(BF16), bl, onorm.astype(F32).reshape(1, GLA_DV))


def _merge_kernel(a_ref, b_ref, c_ref, ga_ref, gb_ref, gc_ref, h_ref, wa_ref, wb_ref, wc_ref,
                  wo_ref, o_ref):
    mixed = (_sigmoid(ga_ref[...].astype(F32)) * _dot(a_ref[...], wa_ref[...])
             + _sigmoid(gb_ref[...].astype(F32)) * _dot(b_ref[...], wb_ref[...])
             + _sigmoid(gc_ref[...].astype(F32)) * _dot(c_ref[...], wc_ref[...]))
    o_ref[...] = h_ref[...] + _dot(mixed.astype(BF16), wo_ref[...])


def merge_out(ya, yb, yc, p2, h, wa, wb, wc, wo, tm=512):
    M, D = h.shape
    tm = _tile(M, tm)
    br = lambda w: pl.BlockSpec((tm, w), lambda i: (i, 0))
    gate = lambda c: pl.BlockSpec((tm, D), lambda i: (i, COL_GATE // D + c))
    wfull = lambda r: pl.BlockSpec((r, D), lambda i: (0, 0))
    return pl.pallas_call(
        _merge_kernel,
        grid=(M // tm,),
        in_specs=[br(ATT_Q_W), br(HY_WIDTH), br(GLA_V_W), gate(0), gate(1), gate(2),
                  pl.BlockSpec((tm, D), lambda i: (i, 0)),
                  wfull(ATT_Q_W), wfull(HY_WIDTH), wfull(GLA_V_W), wfull(D)],
        out_specs=pl.BlockSpec((tm, D), lambda i: (i, 0)),
        out_shape=jax.ShapeDtypeStruct((M, D), F32),
        compiler_params=_cparams("parallel"),
        name="gated_merge_out_proj",
    )(ya, yb, yc, p2, p2, p2, h, wa, wb, wc, wo)


def _head_rmsnorm(x, g):
    ms = jnp.mean(x * x, axis=-1, keepdims=True)
    return x * lax.rsqrt(ms + EPS) * g


def _xattn_kernel(h_ref, ln_ref, wq_ref, k_ref, v_ref, qn_ref, kn_ref, wo_ref, o_ref, att_ref):
    h = h_ref[0]
    hn = _head_rmsnorm(h, ln_ref[...]).astype(BF16)
    q = _dot(hn, wq_ref[...])
    for hd in range(X_HEADS):
        sl = slice(hd * X_HEAD_DIM, (hd + 1) * X_HEAD_DIM)
        qh = (_head_rmsnorm(q[:, sl], qn_ref[...]) * (X_HEAD_DIM ** -0.5)).astype(BF16)
        kh = _head_rmsnorm(k_ref[0, :, sl].astype(F32), kn_ref[...]).astype(BF16)
        s = _dot_nt(qh, kh)
        p = jnp.exp(s - jnp.max(s, axis=-1, keepdims=True))
        l = jnp.sum(p, axis=-1, keepdims=True)
        att_ref[:, sl] = (_dot(p.astype(BF16), v_ref[0, :, sl]) / l).astype(BF16)
    o_ref[0] = h + _dot(att_ref[...], wo_ref[...])


def cross_attention(h3, ln, wq, kv3, qn, kn, wo, tl=512):
    B, L, D = h3.shape
    Mm = kv3.shape[1]
    tl = _tile(L, tl)
    vec = lambda w: pl.BlockSpec((1, w), lambda b, i: (0, 0))
    mat = pl.BlockSpec((D, D), lambda b, i: (0, 0))
    return pl.pallas_call(
        _xattn_kernel,
        grid=(B, L // tl),
        in_specs=[
            pl.BlockSpec((1, tl, D), lambda b, i: (b, i, 0)),
            vec(D), mat,
            pl.BlockSpec((1, Mm, D), lambda b, i: (b, 0, 0)),
            pl.BlockSpec((1, Mm, D), lambda b, i: (b, 0, 1)),
            vec(X_HEAD_DIM), vec(X_HEAD_DIM), mat,
        ],
        out_specs=pl.BlockSpec((1, tl, D), lambda b, i: (b, i, 0)),
        out_shape=jax.ShapeDtypeStruct((B, L, D), F32),
        scratch_shapes=[pltpu.VMEM((tl, D), BF16)],
        compiler_params=_cparams("parallel", "parallel"),
        name="memory_cross_attention",
    )(h3, ln.astype(F32).reshape(1, D), wq, kv3, kv3, qn.astype(F32).reshape(1, X_HEAD_DIM),
      kn.astype(F32).reshape(1, X_HEAD_DIM), wo)


def _mlp_kernel(h_ref, g_ref, w1_ref, w2_ref, o_ref, hn_ref, acc_ref):
    j = pl.program_id(1)

    @pl.when(j == 0)
    def _():
        hn_ref[...] = _head_rmsnorm(h_ref[...], g_ref[...]).astype(BF16)
        acc_ref[...] = jnp.zeros_like(acc_ref)

    a = jnp.maximum(_dot(hn_ref[...], w1_ref[...]), 0.0)
    acc_ref[...] += _dot((a * a).astype(BF16), w2_ref[...])

    @pl.when(j == pl.num_programs(1) - 1)
    def _():
        o_ref[...] = h_ref[...] + acc_ref[...]


def mlp(h, g, w1, w2, tm=1024, tf=1024):
    M, D = h.shape
    F = w1.shape[1]
    tm, tf = _tile(M, tm), _tile(F, tf)
    return pl.pallas_call(
        _mlp_kernel,
        grid=(M // tm, F // tf),
        in_specs=[
            pl.BlockSpec((tm, D), lambda i, j: (i, 0)),
            pl.BlockSpec((1, D), lambda i, j: (0, 0)),
            pl.BlockSpec((D, tf), lambda i, j: (0, j)),
            pl.BlockSpec((tf, D), lambda i, j: (j, 0)),
        ],
        out_specs=pl.BlockSpec((tm, D), lambda i, j: (i, 0)),
        out_shape=jax.ShapeDtypeStruct((M, D), F32),
        scratch_shapes=[pltpu.VMEM((tm, D), BF16), pltpu.VMEM((tm, D), F32)],
        compiler_params=_cparams("parallel", "arbitrary"),
        name="relu2_mlp",
    )(h, g.astype(F32).reshape(1, D), w1, w2)


def _pack_w_in(w):
    kv0 = ATT_Q_W
    hy0 = kv0 + 2 * ATT_KV_W
    lr0 = hy0 + 3 * HY_WIDTH + 2 * GLA_K_W + 2 * GLA_V_W
    g0 = lr0 + 2 * GLA_RANK
    assert lr0 - hy0 == COL_AK - COL_HY and w.shape[2] - g0 == N_PACK - COL_GATE
    packed = jnp.zeros(w.shape[:2] + (N_PACK,), BF16)
    for src0, src1, dst in ((0, kv0, COL_AQ), (kv0, hy0, COL_AK), (hy0, lr0, COL_HY),
                            (lr0, g0, COL_LR), (g0, w.shape[2], COL_GATE)):
        packed = lax.dynamic_update_slice(packed, w[:, :, src0:src1].astype(BF16), (0, 0, dst))
    return packed


def kernel(x, mem, ln_mix, w_in, attn_qnorm, attn_knorm, hy_conv, hy_w1, hy_b1, hy_w2, hy_b2,
           hy_w3, hy_b3, hy_freq, hy_skip, gla_w_lr, gla_b_lr, gla_onorm, w_br_attn, w_br_hyena,
           w_br_gla, w_out, ln_x, ln_mem, x_wq, x_wk, x_wv, x_wo, x_qnorm, x_knorm, ln_mlp,
           mlp_w1, mlp_w2):
    B, L, D = x.shape
    Mm = mem.shape[1]
    depth = w_in.shape[0]
    M = B * L

    rope_tabs = _rope_tables(L)
    lag = jnp.arange(-L, L)
    z_pos, window = _hyena_pos_features(L, jnp.abs(lag).astype(F32))
    window = jnp.where((lag == -L)[:, None], 0.0, window)
    z_pad_w = 128
    z_pad = jnp.pad(z_pos, ((0, 0), (0, z_pad_w - HY_EMB)))
    fm, g_inv = _dft_tables(min(L, HY_BLOCK))
    bf = lambda a: a.astype(BF16)

    w_in_packed = _pack_w_in(w_in)
    h = x.astype(F32).reshape(M, D)
    mem2 = mem.astype(F32).reshape(B * Mm, D)
    for i in range(depth):
        p2 = norm_matmul(h, ln_mix[i].astype(F32), w_in_packed[i], BF16)
        p3 = p2.reshape(B, L, N_PACK)
        y_a = attention(p3, attn_qnorm[i], attn_knorm[i], rope_tabs)
        w1p = jnp.pad(hy_w1[i].astype(F32), ((0, z_pad_w - HY_EMB), (0, 0)))
        f_all = hyena_filters(z_pad, window, w1p, hy_b1[i].astype(F32), hy_w2[i].astype(F32),
                              hy_b2[i].astype(F32), hy_w3[i].astype(F32),
                              hy_b3[i].astype(F32), hy_freq[i].astype(F32))
        spec = hyena_spectra(fm, f_all)
        y_b = hyena_mixer(p3, hy_conv[i], spec, hy_skip[i], fm, g_inv)
        y_c = gla_mixer(p3, gla_w_lr[i], gla_b_lr[i], gla_onorm[i])
        h = merge_out(y_a.reshape(M, ATT_Q_W), y_b.reshape(M, HY_WIDTH), y_c.reshape(M, GLA_V_W),
                      p2, h, bf(w_br_attn[i]), bf(w_br_hyena[i]), bf(w_br_gla[i]), bf(w_out[i]))
        wkv = jnp.concatenate([x_wk[i], x_wv[i]], axis=1)
        kv = norm_matmul(mem2, ln_mem[i].astype(F32), bf(wkv), BF16).reshape(B, Mm, 2 * D)
        h = cross_attention(h.reshape(B, L, D), ln_x[i], bf(x_wq[i]), kv, x_qnorm[i],
                            x_knorm[i], bf(x_wo[i])).reshape(M, D)
        h = mlp(h, ln_mlp[i], bf(mlp_w1[i]), bf(mlp_w2[i]))
    return h.reshape(B, L, D).astype(x.dtype)
```

```python
import functools
import math

import jax
import jax.numpy as jnp
from jax import lax
from jax.experimental import pallas as pl
from jax.experimental.pallas import tpu as pltpu

F32 = jnp.float32
BF16 = jnp.bfloat16

D_MODEL = 1024
GRID_W = 64
ROPE_THETA = 10000.0
HEAD_DIM = 64
ATT_Q_HEADS = 8
ATT_KV_HEADS = 2
ATT_Q_W = ATT_Q_HEADS * HEAD_DIM
ATT_KV_W = ATT_KV_HEADS * HEAD_DIM
HY_WIDTH = 512
HY_ORDER = 2
HY_BANDS = 16
HY_EMB = 1 + 2 * HY_BANDS
HY_FFN = 64
HY_FAST_DECAY = 0.3
HY_SLOW_DECAY = 1.5
HY_TARGET = 1e-2
GLA_HEADS = 4
GLA_DK = 64
GLA_DV = 128
GLA_RANK = 16
GLA_NORMALIZER = 16.0
GLA_CHUNK = 64
GLA_K_W = GLA_HEADS * GLA_DK
GLA_V_W = GLA_HEADS * GLA_DV
X_HEADS = 4
X_HEAD_DIM = D_MODEL // X_HEADS
D_FF = 4 * D_MODEL
N_BRANCH = 3
EPS = 1e-6

COL_AQ = 0
COL_HY = 512
COL_GQ = 2048
COL_GK = 2304
COL_GV = 2560
COL_GO = 3072
COL_AK = 3584
COL_AV = 3712
COL_LR = 3840
COL_GATE = 4096
N_PACK = 7168

VMEM_LIMIT_BYTES = 52 * 1024 * 1024


def _cparams(*sem):
    return pltpu.CompilerParams(dimension_semantics=sem, vmem_limit_bytes=VMEM_LIMIT_BYTES)


def _tile(n, t):
    t = min(n, t)
    assert n % t == 0, (n, t)
    return t


def _dot(a, b):
    return jnp.dot(a, b, preferred_element_type=F32)


def _dot_nt(a, b):
    return lax.dot_general(a, b, (((1,), (1,)), ((), ())), preferred_element_type=F32)


def _dot_tn(a, b):
    return lax.dot_general(a, b, (((0,), (0,)), ((), ())), preferred_element_type=F32)


def _sigmoid(x):
    return 1.0 / (1.0 + jnp.exp(-x))


def _norm_mm_kernel(x_ref, g_ref, w_ref, o_ref, xn_ref):
    @pl.when(pl.program_id(1) == 0)
    def _():
        x = x_ref[...]
        ms = jnp.mean(x * x, axis=-1, keepdims=True)
        xn_ref[...] = (x * lax.rsqrt(ms + EPS) * g_ref[...]).astype(BF16)

    o_ref[...] = _dot(xn_ref[...], w_ref[...]).astype(o_ref.dtype)


def norm_matmul(x, g, w, out_dtype, tm=1024, tn=3584):
    M, K = x.shape
    N = w.shape[1]
    tm, tn = _tile(M, tm), _tile(N, tn)
    return pl.pallas_call(
        _norm_mm_kernel,
        grid=(M // tm, N // tn),
        in_specs=[
            pl.BlockSpec((tm, K), lambda i, j: (i, 0)),
            pl.BlockSpec((1, K), lambda i, j: (0, 0)),
            pl.BlockSpec((K, tn), lambda i, j: (0, j)),
        ],
        out_specs=pl.BlockSpec((tm, tn), lambda i, j: (i, j)),
        out_shape=jax.ShapeDtypeStruct((M, N), out_dtype),
        scratch_shapes=[pltpu.VMEM((tm, K), BF16)],
        compiler_params=_cparams("parallel", "arbitrary"),
        name="norm_matmul",
    )(x, g.reshape(1, K), w)


def _group_mean_sq(x, gm_ref):
    return _dot((x * x).astype(BF16), gm_ref[...])


def _rope(x, c_ref, sa_ref, sb_ref):
    w = x.shape[-1]
    return (x * c_ref[...] + pltpu.roll(x, w - HEAD_DIM // 4, 1) * sa_ref[...]
            + pltpu.roll(x, HEAD_DIM // 4, 1) * sb_ref[...])


def _attn_kernel(q_ref, k_ref, v_ref, qn_ref, kn_ref, cq_ref, saq_ref, sbq_ref,
                 ck_ref, sak_ref, sbk_ref, gmq_ref, gmk_ref, o_ref, kp_ref, vx_ref):
    @pl.when(pl.program_id(1) == 0)
    def _():
        k = k_ref[0].astype(F32)
        kh = k * lax.rsqrt(_group_mean_sq(k, gmk_ref) + EPS) * kn_ref[...]
        kp_ref[...] = _rope(kh, ck_ref, sak_ref, sbk_ref).astype(BF16)
        v = v_ref[0]
        ones = jnp.ones((v.shape[0], HEAD_DIM), BF16)
        vx_ref[...] = jnp.concatenate(
            [piece for kv in range(ATT_KV_HEADS)
             for piece in (v[:, kv * HEAD_DIM:(kv + 1) * HEAD_DIM], ones)], axis=-1)

    q = q_ref[0].astype(F32)
    qh = q * lax.rsqrt(_group_mean_sq(q, gmq_ref) + EPS) * qn_ref[...]
    qb = (_rope(qh, cq_ref, saq_ref, sbq_ref) * (HEAD_DIM ** -0.5 * math.log2(math.e))).astype(BF16)
    group = ATT_Q_HEADS // ATT_KV_HEADS
    outs = []
    for h in range(ATT_Q_HEADS):
        kv = h // group
        kk = kp_ref[:, kv * HEAD_DIM:(kv + 1) * HEAD_DIM]
        s = _dot_nt(qb[:, h * HEAD_DIM:(h + 1) * HEAD_DIM], kk)
        p = jnp.exp2(s - jnp.max(s, axis=-1, keepdims=True))
        ox = _dot(p.astype(BF16), vx_ref[:, kv * 2 * HEAD_DIM:(kv + 1) * 2 * HEAD_DIM])
        outs.append(ox[:, :HEAD_DIM] / ox[:, HEAD_DIM:])
    o_ref[0] = jnp.concatenate(outs, axis=-1).astype(o_ref.dtype)


def _rope_tables(L):
    rows = L // GRID_W
    r, c = jnp.meshgrid(jnp.arange(rows), jnp.arange(GRID_W), indexing="ij")
    n_freq = HEAD_DIM // 4
    inv = ROPE_THETA ** (-jnp.arange(n_freq, dtype=F32) / n_freq)
    pos = jnp.stack([r.reshape(-1), c.reshape(-1)], axis=1).astype(F32)
    ang = pos[:, :, None] * inv
    cos, sin = jnp.cos(ang), jnp.sin(ang)
    zero = jnp.zeros_like(sin)
    c64 = jnp.concatenate([cos, cos], axis=-1).reshape(L, HEAD_DIM)
    sa64 = jnp.concatenate([-sin, zero], axis=-1).reshape(L, HEAD_DIM)
    sb64 = jnp.concatenate([zero, sin], axis=-1).reshape(L, HEAD_DIM)
    return c64, sa64, sb64


def attention(p3, qn, kn, tabs, tq=1024):
    B, L, _ = p3.shape
    tq = _tile(L, tq)
    c64, sa64, sb64 = tabs
    tq_tabs = [jnp.tile(t, (1, ATT_Q_HEADS)) for t in (c64, sa64, sb64)]
    tk_tabs = [jnp.tile(t, (1, ATT_KV_HEADS)) for t in (c64, sa64, sb64)]
    qn_t = jnp.tile(qn.astype(F32), ATT_Q_HEADS).reshape(1, ATT_Q_W)
    kn_t = jnp.tile(kn.astype(F32), ATT_KV_HEADS).reshape(1, ATT_KV_W)

    def group_mean(width):
        g = jnp.arange(width) // HEAD_DIM
        return jnp.where(g[:, None] == g[None, :], 1.0 / HEAD_DIM, 0.0).astype(BF16)

    qtab = pl.BlockSpec((tq, ATT_Q_W), lambda b, i: (i, 0))
    ktab = pl.BlockSpec((L, ATT_KV_W), lambda b, i: (0, 0))
    return pl.pallas_call(
        _attn_kernel,
        grid=(B, L // tq),
        in_specs=[
            pl.BlockSpec((1, tq, ATT_Q_W), lambda b, i: (b, i, COL_AQ // ATT_Q_W)),
            pl.BlockSpec((1, L, ATT_KV_W), lambda b, i: (b, 0, COL_AK // ATT_KV_W)),
            pl.BlockSpec((1, L, ATT_KV_W), lambda b, i: (b, 0, COL_AV // ATT_KV_W)),
            pl.BlockSpec((1, ATT_Q_W), lambda b, i: (0, 0)),
            pl.BlockSpec((1, ATT_KV_W), lambda b, i: (0, 0)),
            qtab, qtab, qtab, ktab, ktab, ktab,
            pl.BlockSpec((ATT_Q_W, ATT_Q_W), lambda b, i: (0, 0)),
            pl.BlockSpec((ATT_KV_W, ATT_KV_W), lambda b, i: (0, 0)),
        ],
        out_specs=pl.BlockSpec((1, tq, ATT_Q_W), lambda b, i: (b, i, 0)),
        out_shape=jax.ShapeDtypeStruct((B, L, ATT_Q_W), BF16),
        scratch_shapes=[pltpu.VMEM((L, ATT_KV_W), BF16), pltpu.VMEM((L, 2 * ATT_KV_W), BF16)],
        compiler_params=_cparams("parallel", "arbitrary"),
        name="gqa_attention",
    )(p3, p3, p3, qn_t, kn_t, *tq_tabs, *tk_tabs, group_mean(ATT_Q_W), group_mean(ATT_KV_W))


def _dft_tables(L):
    n = 2 * L
    k = jnp.arange(L, dtype=jnp.int32)[:, None]
    j = jnp.arange(L, dtype=jnp.int32)[None, :]
    step = 64
    ja = jnp.arange(0, L, step, dtype=jnp.int32)[None, :]
    jb = jnp.arange(step, dtype=jnp.int32)[None, :]
    ang_a = ((k * ja) % n).astype(F32) * (2.0 * math.pi / n)
    ang_b = ((k * jb) % n).astype(F32) * (2.0 * math.pi / n)
    ca, sa = jnp.cos(ang_a)[:, :, None], jnp.sin(ang_a)[:, :, None]
    cb, sb = jnp.cos(ang_b)[:, None, :], jnp.sin(ang_b)[:, None, :]
    c = (ca * cb - sa * sb).reshape(L, L)
    s = -(sa * cb + ca * sb).reshape(L, L)
    bottom = jnp.where(k == 0, jnp.where(j % 2 == 0, 1.0, -1.0), s)
    bottom_t = jnp.where(j == 0, jnp.where(k % 2 == 0, 1.0, -1.0), s)
    fm = jnp.stack([c, bottom]).astype(BF16)
    g = jnp.concatenate([c, bottom_t], axis=1).astype(BF16)
    return fm, g


def _hyena_pos_features(L, t):
    t_norm = t / max(L - 1, 1)
    w = 2.0 * math.pi * t / L
    f = jnp.linspace(1e-4, HY_BANDS - 1, HY_BANDS, dtype=F32)
    fw = w[:, None] * f
    z = jnp.concatenate([t_norm[:, None], jnp.cos(fw), -jnp.sin(fw)], axis=-1)
    deltas = jnp.abs(jnp.linspace(math.log(HY_TARGET) / HY_FAST_DECAY,
                                  math.log(HY_TARGET) / HY_SLOW_DECAY, HY_WIDTH, dtype=F32))
    window = jnp.exp(-t_norm[:, None] * deltas)
    return z, window


def _sconv_kernel(u_ref, w_ref, o_ref):
    u = u_ref[0].astype(F32)
    L = u.shape[0]
    row = lax.broadcasted_iota(jnp.int32, u.shape, 0)
    prev = jnp.where(row == 0, 0.0, pltpu.roll(u, 1, 0))
    nxt = jnp.where(row == L - 1, 0.0, pltpu.roll(u, L - 1, 0))
    o_ref[0] = (prev * w_ref[0:1, :] + u * w_ref[1:2, :] + nxt * w_ref[2:3, :]).astype(o_ref.dtype)


def short_conv(p3, w):
    B, L, _ = p3.shape
    nblk = 3
    return pl.pallas_call(
        _sconv_kernel,
        grid=(B, nblk),
        in_specs=[
            pl.BlockSpec((1, L, HY_WIDTH), lambda b, c: (b, 0, COL_HY // HY_WIDTH + c)),
            pl.BlockSpec((3, HY_WIDTH), lambda b, c: (0, c)),
        ],
        out_specs=pl.BlockSpec((1, L, HY_WIDTH), lambda b, c: (b, 0, c)),
        out_shape=jax.ShapeDtypeStruct((B, L, nblk * HY_WIDTH), BF16),
        compiler_params=_cparams("parallel", "parallel"),
        name="hyena_short_conv",
    )(p3, w)


def _hyfilter_kernel(z_ref, w1_ref, b1_ref, w2_ref, b2_ref, w3_ref, b3_ref, fr_ref, win_ref,
                     f_ref):
    L = z_ref.shape[0]
    hp = lax.Precision.HIGHEST
    h = jnp.sin(fr_ref[0:1, :] * (jnp.dot(z_ref[...], w1_ref[...], precision=hp,
                                          preferred_element_type=F32) + b1_ref[...]))
    h = jnp.sin(fr_ref[1:2, :] * (jnp.dot(h, w2_ref[...], precision=hp,
                                          preferred_element_type=F32) + b2_ref[...]))
    f = (jnp.dot(h, w3_ref[0], precision=hp, preferred_element_type=F32)
         + b3_ref[0]) * win_ref[...]
    col = jnp.sum(jnp.abs(f), axis=0, keepdims=True)
    norm = col[:, :HY_WIDTH] + col[:, HY_WIDTH:] + EPS
    f_ref[:L, :] = (f[:, :HY_WIDTH] / norm).astype(f_ref.dtype)
    f_ref[L:, :] = (f[:, HY_WIDTH:] / norm).astype(f_ref.dtype)


def _block_diag2(w):
    z = jnp.zeros_like(w)
    return jnp.concatenate([jnp.concatenate([w, z], axis=1), jnp.concatenate([z, w], axis=1)], axis=0)


def hyena_filters(z_pair, win_pair, w1, b1, w2, b2, w3, b3, freq):
    L = z_pair.shape[0]
    pad_w1 = jnp.pad(w1, ((0, HY_FFN - HY_EMB), (0, 0)))
    two = lambda v: jnp.concatenate([v, v], axis=-1)
    w3r = w3.reshape(HY_FFN, HY_ORDER, 2, HY_WIDTH)
    zero = jnp.zeros((HY_ORDER, HY_FFN, HY_WIDTH), F32)
    w3n = jnp.moveaxis(w3r[:, :, 1], 1, 0)
    w3p = jnp.moveaxis(w3r[:, :, 0], 1, 0)
    w3bd = jnp.concatenate([jnp.concatenate([w3n, zero], axis=2),
                            jnp.concatenate([zero, w3p], axis=2)], axis=1)
    b3r = b3.reshape(HY_ORDER, 1, 2, HY_WIDTH)
    b3c = jnp.concatenate([b3r[:, :, 1], b3r[:, :, 0]], axis=-1)
    full = lambda shape: pl.BlockSpec(shape, lambda o: (0,) * len(shape))
    H2 = 2 * HY_FFN
    return pl.pallas_call(
        _hyfilter_kernel,
        grid=(HY_ORDER,),
        in_specs=[
            full((L, H2)), full((H2, H2)), full((1, H2)), full((H2, H2)), full((1, H2)),
            pl.BlockSpec((1, H2, 2 * HY_WIDTH), lambda o: (o, 0, 0)),
            pl.BlockSpec((1, 1, 2 * HY_WIDTH), lambda o: (o, 0, 0)),
            full((2, H2)), full((L, 2 * HY_WIDTH)),
        ],
        out_specs=pl.BlockSpec((2 * L, HY_WIDTH), lambda o: (0, o)),
        out_shape=jax.ShapeDtypeStruct((2 * L, HY_ORDER * HY_WIDTH), BF16),
        compiler_params=_cparams("parallel"),
        name="hyena_filter_mlp",
    )(z_pair, _block_diag2(pad_w1), two(b1.reshape(1, -1)), _block_diag2(w2),
      two(b2.reshape(1, -1)), w3bd, b3c, two(freq), win_pair)


def _spec_kernel(fm_ref, lo_ref, hi_ref, h_ref, *, n):
    P = fm_ref.shape[1]
    row = lax.broadcasted_iota(jnp.int32, (P, hi_ref.shape[1]), 0)
    sign = jnp.where(row % 2 == 0, 1.0, -1.0)
    scale = jnp.where(row == 0, 1.0 / n, 2.0 / n)
    for half in range(2):
        h_ref[0, half] = (_dot(fm_ref[half], hi_ref[...])
                          + sign * _dot(fm_ref[half], lo_ref[...])) * scale


def hyena_spectra(fm, f_all):
    P = fm.shape[1]
    W = f_all.shape[1]
    n_lag = f_all.shape[0] // P - 1
    return pl.pallas_call(
        functools.partial(_spec_kernel, n=2 * P),
        grid=(n_lag,),
        in_specs=[
            pl.BlockSpec((2, P, P), lambda i: (0, 0, 0)),
            pl.BlockSpec((P, W), lambda i: (i, 0)),
            pl.BlockSpec((P, W), lambda i: (i + 1, 0)),
        ],
        out_specs=pl.BlockSpec((1, 2, P, W), lambda i: (i, 0, 0, 0)),
        out_shape=jax.ShapeDtypeStruct((n_lag, 2, P, W), F32),
        compiler_params=_cparams("parallel"),
        name="hyena_filter_spectrum",
    )(fm, f_all, f_all)


HY_BLOCK = 512
HY_CONV_ROWS = 32


def _lconv_kernel(fm_ref, g_ref, u_ref, gate_ref, skip_ref, h_ref, o_ref, x_ref, yt_ref, yb_ref):
    P = fm_ref.shape[1]
    nb = u_ref.shape[1] // P
    ct = u_ref.shape[2]
    for j in range(nb):
        uj = u_ref[0, j * P:(j + 1) * P, :]
        x_ref[j, 0] = _dot(fm_ref[0], uj)
        x_ref[j, 1] = _dot(fm_ref[1], uj)
    first = lax.broadcasted_iota(jnp.int32, (HY_CONV_ROWS, ct), 0) == 0
    for i in range(nb):
        for r in range(P // HY_CONV_ROWS):
            rows = slice(r * HY_CONV_ROWS, (r + 1) * HY_CONV_ROWS)
            at = jnp.zeros((HY_CONV_ROWS, ct), F32)
            ab = jnp.zeros((HY_CONV_ROWS, ct), F32)
            for j in range(nb):
                d = i - j + nb - 1
                xt, xb = x_ref[j, 0, rows, :], x_ref[j, 1, rows, :]
                ht, hb = h_ref[d, 0, rows, :], h_ref[d, 1, rows, :]
                bb = xb * hb
                if r == 0:
                    at += xt * ht - jnp.where(first, 0.0, bb)
                    ab += jnp.where(first, bb, xt * hb + xb * ht)
                else:
                    at += xt * ht - bb
                    ab += xt * hb + xb * ht
            yt_ref[i, rows, :] = at.astype(BF16)
            yb_ref[i, rows, :] = ab.astype(BF16)
        y = _dot(g_ref[:, :P], yt_ref[i]) + _dot(g_ref[:, P:], yb_ref[i])
        blk = slice(i * P, (i + 1) * P)
        ui = u_ref[0, blk, :].astype(F32)
        o_ref[0, blk, :] = ((y + ui * skip_ref[...]) * gate_ref[0, blk, :].astype(F32)
                            ).astype(o_ref.dtype)


def long_conv(fm, g, spec, order, u, u_col, gate, gate_col, skip, ct=256):
    B, L, _ = u.shape
    P = fm.shape[1]
    nb = L // P
    n_lag = spec.shape[0]
    per = HY_WIDTH // ct
    return pl.pallas_call(
        _lconv_kernel,
        grid=(per, B),
        in_specs=[
            pl.BlockSpec((2, P, P), lambda c, b: (0, 0, 0)),
            pl.BlockSpec((P, 2 * P), lambda c, b: (0, 0)),
            pl.BlockSpec((1, L, ct), lambda c, b: (b, 0, u_col * per + c)),
            pl.BlockSpec((1, L, ct), lambda c, b: (b, 0, gate_col * per + c)),
            pl.BlockSpec((1, ct), lambda c, b: (0, c)),
            pl.BlockSpec((n_lag, 2, P, ct), lambda c, b: (0, 0, 0, order * per + c)),
        ],
        out_specs=pl.BlockSpec((1, L, ct), lambda c, b: (b, 0, c)),
        out_shape=jax.ShapeDtypeStruct((B, L, HY_WIDTH), BF16),
        scratch_shapes=[pltpu.VMEM((nb, 2, P, ct), F32), pltpu.VMEM((nb, P, ct), BF16),
                        pltpu.VMEM((nb, P, ct), BF16)],
        compiler_params=_cparams("parallel", "parallel"),
        name="hyena_long_conv",
    )(fm, g, u, gate, skip.reshape(1, HY_WIDTH).astype(F32), spec)


def hyena_mixer(p3, conv_w, spec, skip, fm, g):
    uc = short_conv(p3, conv_w.astype(F32))
    z = long_conv(fm, g, spec, 0, uc, 0, uc, 1, skip[0])
    return long_conv(fm, g, spec, 1, z, 0, uc, 2, skip[1])


def _log_sigmoid(x):
    return -(jnp.maximum(-x, 0.0) + jnp.log(1.0 + jnp.exp(-jnp.abs(x))))


def _dot_hilo(m, x):
    hi = x.astype(BF16)
    lo = (x - hi.astype(F32)).astype(BF16)
    return _dot(m, hi) + _dot(m, lo)


GLA_PREP_ROWS = 256


def _gla_kernel(q_ref, k_ref, v_ref, og_ref, lr_ref, wlr_ref, blr_ref, on_ref, o_ref,
                qs_ref, ki_ref, ks_ref, dec_ref, acc_ref):
    L = q_ref.shape[1]
    C = GLA_CHUNK
    H = GLA_HEADS
    RB = GLA_PREP_ROWS
    n_chunks = L // C
    KW = GLA_K_W

    pr = lax.broadcasted_iota(jnp.int32, (RB, RB), 0)
    pc = lax.broadcasted_iota(jnp.int32, (RB, RB), 1)
    same = (pr // C) == (pc // C)
    cum_f = jnp.where(same & (pc <= pr), 1.0, 0.0).astype(BF16)
    cum_b = jnp.where(same & (pc >= pr), 1.0, 0.0).astype(BF16)
    tot_m = jnp.where(same, 1.0, 0.0).astype(BF16)
    lane_head = lax.broadcasted_iota(jnp.int32, (C, KW), 1) // GLA_DK

    def prep(i, carry):
        r0 = pl.multiple_of(i * RB, RB)
        logit = _dot(lr_ref[0, pl.ds(r0, RB), :], wlr_ref[...]) + blr_ref[...]
        g = _log_sigmoid(logit) * (1.0 / GLA_NORMALIZER)
        q = q_ref[0, pl.ds(r0, RB), :].astype(F32) * (GLA_DK ** -0.5)
        k = k_ref[0, pl.ds(r0, RB), :].astype(F32)
        for d, cum in ((0, cum_f), (1, cum_b)):
            gd = g[:, d * KW:(d + 1) * KW]
            b = _dot_hilo(cum, gd)
            tot = _dot_hilo(tot_m, gd)
            qd = q * jnp.exp(b)
            ki_ref[d, pl.ds(r0, RB), :] = (k * jnp.exp(-b)).astype(BF16)
            ks_ref[d, pl.ds(r0, RB), :] = (k * jnp.exp(tot - b)).astype(BF16)
            dec = jnp.exp(tot)
            for c in range(RB // C):
                n = i * (RB // C) + c
                dec_ref[d, pl.ds(n, 1), :] = dec[c * C:c * C + 1, :]
                qc = qd[c * C:(c + 1) * C, :]
                for h in range(H):
                    row = pl.multiple_of((n * H + h) * C, C)
                    qs_ref[d, pl.ds(row, C), :] = jnp.where(lane_head == h, qc, 0.0).astype(BF16)
        return carry

    lax.fori_loop(0, L // RB, prep, 0)

    st_lane_head = lax.broadcasted_iota(jnp.int32, (GLA_DV, KW), 1) // GLA_DK
    arow = lax.broadcasted_iota(jnp.int32, (H * C, C), 0) % C
    acol = lax.broadcasted_iota(jnp.int32, (H * C, C), 1)
    amasks = (acol <= arow, acol > arow)

    def body(it, sts):
        new = []
        for d in (0, 1):
            st = sts[d]
            n = it if d == 0 else n_chunks - 1 - it
            r0 = pl.multiple_of(n * C, C)
            rq = pl.multiple_of(n * (H * C), H * C)
            qs = qs_ref[d, pl.ds(rq, H * C), :]
            ki = ki_ref[d, pl.ds(r0, C), :]
            ks = ks_ref[d, pl.ds(r0, C), :]
            v = v_ref[0, pl.ds(r0, C), :]
            a = jnp.where(amasks[d], _dot_nt(qs, ki), 0.0).astype(BF16)
            inter = _dot_nt(qs, st.astype(BF16))
            outs = []
            for h in range(H):
                oh = _dot(a[h * C:(h + 1) * C, :], v[:, h * GLA_DV:(h + 1) * GLA_DV])
                outs.append(oh + inter[h * C:(h + 1) * C, :])
            acc_ref[d, pl.ds(r0, C), :] = jnp.concatenate(outs, axis=-1)
            dst = _dot_tn(v, ks)
            upd = jnp.zeros((GLA_DV, KW), F32)
            for h in range(H):
                upd = jnp.where(st_lane_head == h, dst[h * GLA_DV:(h + 1) * GLA_DV, :], upd)
            new.append(st * dec_ref[d, pl.ds(n, 1), :] + upd)
        return tuple(new)

    zero = jnp.zeros((GLA_DV, KW), F32)
    lax.fori_loop(0, n_chunks, body, (zero, zero), unroll=2)

    def finalize(i, carry):
        r0 = pl.multiple_of(i * RB, RB)
        o = acc_ref[0, pl.ds(r0, RB), :] + acc_ref[1, pl.ds(r0, RB), :]
        og = og_ref[0, pl.ds(r0, RB), :].astype(F32)
        outs = []
        for h in range(H):
            oh = o[:, h * GLA_DV:(h + 1) * GLA_DV]
            ms = jnp.mean(oh * oh, axis=-1, keepdims=True)
            outs.append(oh * lax.rsqrt(ms + EPS) * on_ref[...])
        y = jnp.concatenate(outs, axis=-1) * (og * _sigmoid(og))
        o_ref[0, pl.ds(r0, RB), :] = y.astype(o_ref.dtype)
        return carry

    lax.fori_loop(0, L // RB, finalize, 0)


def gla_mixer(p3, w_lr, b_lr, onorm):
    B, L, _ = p3.shape
    lr_w = 256
    wl = jnp.zeros((lr_w, 2 * GLA_K_W), F32)
    wl = wl.at[0:GLA_RANK, 0:GLA_K_W].set(w_lr[0].astype(F32))
    wl = wl.at[GLA_RANK:2 * GLA_RANK, GLA_K_W:].set(w_lr[1].astype(F32))
    bl = b_lr.astype(F32).reshape(1, 2 * GLA_K_W)
    col = lambda width, c: pl.BlockSpec((1, L, width), lambda b: (b, 0, c // width))
    return pl.pallas_call(
        _gla_kernel,
        grid=(B,),
        in_specs=[
            col(GLA_K_W, COL_GQ), col(GLA_K_W, COL_GK), col(GLA_V_W, COL_GV),
            col(GLA_V_W, COL_GO), col(lr_w, COL_LR),
            pl.BlockSpec((lr_w, 2 * GLA_K_W), lambda b: (0, 0)),
            pl.BlockSpec((1, 2 * GLA_K_W), lambda b: (0, 0)),
            pl.BlockSpec((1, GLA_DV), lambda b: (0, 0)),
        ],
        out_specs=pl.BlockSpec((1, L, GLA_V_W), lambda b: (b, 0, 0)),
        out_shape=jax.ShapeDtypeStruct((B, L, GLA_V_W), BF16),
        scratch_shapes=[
            pltpu.VMEM((2, L * GLA_HEADS, GLA_K_W), BF16),
            pltpu.VMEM((2, L, GLA_K_W), BF16),
            pltpu.VMEM((2, L, GLA_K_W), BF16),
            pltpu.VMEM((2, L // GLA_CHUNK, GLA_K_W), F32),
            pltpu.VMEM((2, L, GLA_V_W), F32),
        ],
        compiler_params=_cparams("parallel"),
        name="gla_mixer",
    )(p3, p3, p3, p3, p3, wl.astype(BF16), bl, onorm.astype(F32).reshape(1, GLA_DV))


def _merge_kernel(a_ref, b_ref, c_ref, ga_ref, gb_ref, gc_ref, h_ref, wa_ref, wb_ref, wc_ref,
                  wo_ref, o_ref):
    mixed = (_sigmoid(ga_ref[...].astype(F32)) * _dot(a_ref[...], wa_ref[...])
             + _sigmoid(gb_ref[...].astype(F32)) * _dot(b_ref[...], wb_ref[...])
             + _sigmoid(gc_ref[...].astype(F32)) * _dot(c_ref[...], wc_ref[...]))
    o_ref[...] = h_ref[...] + _dot(mixed.astype(BF16), wo_ref[...])


def merge_out(ya, yb, yc, p2, h, wa, wb, wc, wo, tm=512):
    M, D = h.shape
    tm = _tile(M, tm)
    br = lambda w: pl.BlockSpec((tm, w), lambda i: (i, 0))
    gate = lambda c: pl.BlockSpec((tm, D), lambda i: (i, COL_GATE // D + c))
    wfull = lambda r: pl.BlockSpec((r, D), lambda i: (0, 0))
    return pl.pallas_call(
        _merge_kernel,
        grid=(M // tm,),
        in_specs=[br(ATT_Q_W), br(HY_WIDTH), br(GLA_V_W), gate(0), gate(1), gate(2),
                  pl.BlockSpec((tm, D), lambda i: (i, 0)),
                  wfull(ATT_Q_W), wfull(HY_WIDTH), wfull(GLA_V_W), wfull(D)],
        out_specs=pl.BlockSpec((tm, D), lambda i: (i, 0)),
        out_shape=jax.ShapeDtypeStruct((M, D), F32),
        compiler_params=_cparams("parallel"),
        name="gated_merge_out_proj",
    )(ya, yb, yc, p2, p2, p2, h, wa, wb, wc, wo)


def _head_rmsnorm(x, g):
    ms = jnp.mean(x * x, axis=-1, keepdims=True)
    return x * lax.rsqrt(ms + EPS) * g


def _xattn_kernel(h_ref, ln_ref, wq_ref, k_ref, v_ref, qn_ref, kn_ref, wo_ref, o_ref, att_ref):
    h = h_ref[0]
    hn = _head_rmsnorm(h, ln_ref[...]).astype(BF16)
    q = _dot(hn, wq_ref[...])
    for hd in range(X_HEADS):
        sl = slice(hd * X_HEAD_DIM, (hd + 1) * X_HEAD_DIM)
        qh = (_head_rmsnorm(q[:, sl], qn_ref[...]) * (X_HEAD_DIM ** -0.5)).astype(BF16)
        kh = _head_rmsnorm(k_ref[0, :, sl].astype(F32), kn_ref[...]).astype(BF16)
        s = _dot_nt(qh, kh)
        p = jnp.exp(s - jnp.max(s, axis=-1, keepdims=True))
        l = jnp.sum(p, axis=-1, keepdims=True)
        att_ref[:, sl] = (_dot(p.astype(BF16), v_ref[0, :, sl]) / l).astype(BF16)
    o_ref[0] = h + _dot(att_ref[...], wo_ref[...])


def cross_attention(h3, ln, wq, kv3, qn, kn, wo, tl=1024):
    B, L, D = h3.shape
    Mm = kv3.shape[1]
    tl = _tile(L, tl)
    vec = lambda w: pl.BlockSpec((1, w), lambda b, i: (0, 0))
    mat = pl.BlockSpec((D, D), lambda b, i: (0, 0))
    return pl.pallas_call(
        _xattn_kernel,
        grid=(B, L // tl),
        in_specs=[
            pl.BlockSpec((1, tl, D), lambda b, i: (b, i, 0)),
            vec(D), mat,
            pl.BlockSpec((1, Mm, D), lambda b, i: (b, 0, 0)),
            pl.BlockSpec((1, Mm, D), lambda b, i: (b, 0, 1)),
            vec(X_HEAD_DIM), vec(X_HEAD_DIM), mat,
        ],
        out_specs=pl.BlockSpec((1, tl, D), lambda b, i: (b, i, 0)),
        out_shape=jax.ShapeDtypeStruct((B, L, D), F32),
        scratch_shapes=[pltpu.VMEM((tl, D), BF16)],
        compiler_params=_cparams("parallel", "parallel"),
        name="memory_cross_attention",
    )(h3, ln.astype(F32).reshape(1, D), wq, kv3, kv3, qn.astype(F32).reshape(1, X_HEAD_DIM),
      kn.astype(F32).reshape(1, X_HEAD_DIM), wo)


def _mlp_kernel(h_ref, g_ref, w1_ref, w2_ref, o_ref, hn_ref, acc_ref):
    j = pl.program_id(1)

    @pl.when(j == 0)
    def _():
        hn_ref[...] = _head_rmsnorm(h_ref[...], g_ref[...]).astype(BF16)
        acc_ref[...] = jnp.zeros_like(acc_ref)

    a = jnp.maximum(_dot(hn_ref[...], w1_ref[...]), 0.0)
    acc_ref[...] += _dot((a * a).astype(BF16), w2_ref[...])

    @pl.when(j == pl.num_programs(1) - 1)
    def _():
        o_ref[...] = h_ref[...] + acc_ref[...]


def mlp(h, g, w1, w2, tm=1024, tf=2048):
    M, D = h.shape
    F = w1.shape[1]
    tm, tf = _tile(M, tm), _tile(F, tf)
    return pl.pallas_call(
        _mlp_kernel,
        grid=(M // tm, F // tf),
        in_specs=[
            pl.BlockSpec((tm, D), lambda i, j: (i, 0)),
            pl.BlockSpec((1, D), lambda i, j: (0, 0)),
            pl.BlockSpec((D, tf), lambda i, j: (0, j)),
            pl.BlockSpec((tf, D), lambda i, j: (j, 0)),
        ],
        out_specs=pl.BlockSpec((tm, D), lambda i, j: (i, 0)),
        out_shape=jax.ShapeDtypeStruct((M, D), F32),
        scratch_shapes=[pltpu.VMEM((tm, D), BF16), pltpu.VMEM((tm, D), F32)],
        compiler_params=_cparams("parallel", "arbitrary"),
        name="relu2_mlp",
    )(h, g.astype(F32).reshape(1, D), w1, w2)


def _pack_w_in(w):
    kv0 = ATT_Q_W
    hy0 = kv0 + 2 * ATT_KV_W
    lr0 = hy0 + 3 * HY_WIDTH + 2 * GLA_K_W + 2 * GLA_V_W
    g0 = lr0 + 2 * GLA_RANK
    assert lr0 - hy0 == COL_AK - COL_HY and w.shape[2] - g0 == N_PACK - COL_GATE
    packed = jnp.zeros(w.shape[:2] + (N_PACK,), BF16)
    for src0, src1, dst in ((0, kv0, COL_AQ), (kv0, hy0, COL_AK), (hy0, lr0, COL_HY),
                            (lr0, g0, COL_LR), (g0, w.shape[2], COL_GATE)):
        packed = lax.dynamic_update_slice(packed, w[:, :, src0:src1].astype(BF16), (0, 0, dst))
    return packed


def kernel(x, mem, ln_mix, w_in, attn_qnorm, attn_knorm, hy_conv, hy_w1, hy_b1, hy_w2, hy_b2,
           hy_w3, hy_b3, hy_freq, hy_skip, gla_w_lr, gla_b_lr, gla_onorm, w_br_attn, w_br_hyena,
           w_br_gla, w_out, ln_x, ln_mem, x_wq, x_wk, x_wv, x_wo, x_qnorm, x_knorm, ln_mlp,
           mlp_w1, mlp_w2):
    B, L, D = x.shape
    Mm = mem.shape[1]
    depth = w_in.shape[0]
    M = B * L

    rope_tabs = _rope_tables(L)
    m = jnp.arange(L)
    z_neg, win_neg = _hyena_pos_features(L, (L - m).astype(F32))
    z_fwd, win_fwd = _hyena_pos_features(L, m.astype(F32))
    win_neg = jnp.where((m == 0)[:, None], 0.0, win_neg)
    zpad = lambda z: jnp.pad(z, ((0, 0), (0, HY_FFN - HY_EMB)))
    z_pair = jnp.concatenate([zpad(z_neg), zpad(z_fwd)], axis=1)
    win_pair = jnp.concatenate([win_neg, win_fwd], axis=1)
    fm, g_inv = _dft_tables(min(L, HY_BLOCK))
    bf = lambda a: a.astype(BF16)

    w_in_packed = _pack_w_in(w_in)
    h = x.astype(F32).reshape(M, D)
    mem2 = mem.astype(F32).reshape(B * Mm, D)
    for i in range(depth):
        p2 = norm_matmul(h, ln_mix[i].astype(F32), w_in_packed[i], BF16)
        p3 = p2.reshape(B, L, N_PACK)
        y_a = attention(p3, attn_qnorm[i], attn_knorm[i], rope_tabs)
        f_all = hyena_filters(z_pair, win_pair, hy_w1[i].astype(F32), hy_b1[i].astype(F32),
                              hy_w2[i].astype(F32), hy_b2[i].astype(F32), hy_w3[i].astype(F32),
                              hy_b3[i].astype(F32), hy_freq[i].astype(F32))
        spec = hyena_spectra(fm, f_all)
        y_b = hyena_mixer(p3, hy_conv[i], spec, hy_skip[i], fm, g_inv)
        y_c = gla_mixer(p3, gla_w_lr[i], gla_b_lr[i], gla_onorm[i])
        h = merge_out(y_a.reshape(M, ATT_Q_W), y_b.reshape(M, HY_WIDTH), y_c.reshape(M, GLA_V_W),
                      p2, h, bf(w_br_attn[i]), bf(w_br_hyena[i]), bf(w_br_gla[i]), bf(w_out[i]))
        wkv = jnp.concatenate([x_wk[i], x_wv[i]], axis=1)
        kv = norm_matmul(mem2, ln_mem[i].astype(F32), bf(wkv), BF16).reshape(B, Mm, 2 * D)
        h = cross_attention(h.reshape(B, L, D), ln_x[i], bf(x_wq[i]), kv, x_qnorm[i],
                            x_knorm[i], bf(x_wo[i])).reshape(M, D)
        h = mlp(h, ln_mlp[i], bf(mlp_w1[i]), bf(mlp_w2[i]))
    return h.reshape(B, L, D).astype(x.dtype)
```

```python
import functools
import math

import jax
import jax.numpy as jnp
from jax import lax
from jax.experimental import pallas as pl
from jax.experimental.pallas import tpu as pltpu

F32 = jnp.float32
BF16 = jnp.bfloat16

D_MODEL = 1024
GRID_W = 64
ROPE_THETA = 10000.0
HEAD_DIM = 64
ATT_Q_HEADS = 8
ATT_KV_HEADS = 2
ATT_Q_W = ATT_Q_HEADS * HEAD_DIM
ATT_KV_W = ATT_KV_HEADS * HEAD_DIM
HY_WIDTH = 512
HY_ORDER = 2
HY_BANDS = 16
HY_EMB = 1 + 2 * HY_BANDS
HY_FFN = 64
HY_FAST_DECAY = 0.3
HY_SLOW_DECAY = 1.5
HY_TARGET = 1e-2
GLA_HEADS = 4
GLA_DK = 64
GLA_DV = 128
GLA_RANK = 16
GLA_NORMALIZER = 16.0
GLA_CHUNK = 64
GLA_K_W = GLA_HEADS * GLA_DK
GLA_V_W = GLA_HEADS * GLA_DV
X_HEADS = 4
X_HEAD_DIM = D_MODEL // X_HEADS
D_FF = 4 * D_MODEL
N_BRANCH = 3
EPS = 1e-6

COL_AQ = 0
COL_HY = 512
COL_GQ = 2048
COL_GK = 2304
COL_GV = 2560
COL_GO = 3072
COL_AK = 3584
COL_AV = 3712
COL_LR = 3840
COL_GATE = 4096
N_PACK = 7168

VMEM_LIMIT_BYTES = 52 * 1024 * 1024


def _cparams(*sem):
    return pltpu.CompilerParams(dimension_semantics=sem, vmem_limit_bytes=VMEM_LIMIT_BYTES)


def _tile(n, t):
    t = min(n, t)
    assert n % t == 0, (n, t)
    return t


def _dot(a, b):
    return jnp.dot(a, b, preferred_element_type=F32)


def _dot_nt(a, b):
    return lax.dot_general(a, b, (((1,), (1,)), ((), ())), preferred_element_type=F32)


def _dot_tn(a, b):
    return lax.dot_general(a, b, (((0,), (0,)), ((), ())), preferred_element_type=F32)


def _sigmoid(x):
    return 1.0 / (1.0 + jnp.exp(-x))


def _norm_mm_kernel(x_ref, g_ref, w_ref, o_ref, xn_ref):
    @pl.when(pl.program_id(1) == 0)
    def _():
        x = x_ref[...]
        ms = jnp.mean(x * x, axis=-1, keepdims=True)
        xn_ref[...] = (x * lax.rsqrt(ms + EPS) * g_ref[...]).astype(BF16)

    o_ref[...] = _dot(xn_ref[...], w_ref[...]).astype(o_ref.dtype)


def norm_matmul(x, g, w, out_dtype, tm=1024, tn=3584):
    M, K = x.shape
    N = w.shape[1]
    tm, tn = _tile(M, tm), _tile(N, tn)
    return pl.pallas_call(
        _norm_mm_kernel,
        grid=(M // tm, N // tn),
        in_specs=[
            pl.BlockSpec((tm, K), lambda i, j: (i, 0)),
            pl.BlockSpec((1, K), lambda i, j: (0, 0)),
            pl.BlockSpec((K, tn), lambda i, j: (0, j)),
        ],
        out_specs=pl.BlockSpec((tm, tn), lambda i, j: (i, j)),
        out_shape=jax.ShapeDtypeStruct((M, N), out_dtype),
        scratch_shapes=[pltpu.VMEM((tm, K), BF16)],
        compiler_params=_cparams("parallel", "arbitrary"),
        name="norm_matmul",
    )(x, g.reshape(1, K), w)


def _group_mean_sq(x, gm_ref):
    return _dot((x * x).astype(BF16), gm_ref[...])


def _rope(x, c_ref, sa_ref, sb_ref):
    w = x.shape[-1]
    return (x * c_ref[...] + pltpu.roll(x, w - HEAD_DIM // 4, 1) * sa_ref[...]
            + pltpu.roll(x, HEAD_DIM // 4, 1) * sb_ref[...])


def _attn_kernel(q_ref, k_ref, v_ref, qn_ref, kn_ref, cq_ref, saq_ref, sbq_ref,
                 ck_ref, sak_ref, sbk_ref, gmq_ref, gmk_ref, o_ref, kp_ref, vx_ref):
    @pl.when(pl.program_id(1) == 0)
    def _():
        k = k_ref[0].astype(F32)
        kh = k * lax.rsqrt(_group_mean_sq(k, gmk_ref) + EPS) * kn_ref[...]
        kp_ref[...] = _rope(kh, ck_ref, sak_ref, sbk_ref).astype(BF16)
        v = v_ref[0]
        ones = jnp.ones((v.shape[0], HEAD_DIM), BF16)
        vx_ref[...] = jnp.concatenate(
            [piece for kv in range(ATT_KV_HEADS)
             for piece in (v[:, kv * HEAD_DIM:(kv + 1) * HEAD_DIM], ones)], axis=-1)

    q = q_ref[0].astype(F32)
    qh = q * lax.rsqrt(_group_mean_sq(q, gmq_ref) + EPS) * qn_ref[...]
    qb = (_rope(qh, cq_ref, saq_ref, sbq_ref) * (HEAD_DIM ** -0.5 * math.log2(math.e))).astype(BF16)
    group = ATT_Q_HEADS // ATT_KV_HEADS
    outs = []
    for h in range(ATT_Q_HEADS):
        kv = h // group
        kk = kp_ref[:, kv * HEAD_DIM:(kv + 1) * HEAD_DIM]
        s = _dot_nt(qb[:, h * HEAD_DIM:(h + 1) * HEAD_DIM], kk)
        p = jnp.exp2(s - jnp.max(s, axis=-1, keepdims=True))
        ox = _dot(p.astype(BF16), vx_ref[:, kv * 2 * HEAD_DIM:(kv + 1) * 2 * HEAD_DIM])
        outs.append(ox[:, :HEAD_DIM] / ox[:, HEAD_DIM:])
    o_ref[0] = jnp.concatenate(outs, axis=-1).astype(o_ref.dtype)


def _rope_tables(L):
    rows = L // GRID_W
    r, c = jnp.meshgrid(jnp.arange(rows), jnp.arange(GRID_W), indexing="ij")
    n_freq = HEAD_DIM // 4
    inv = ROPE_THETA ** (-jnp.arange(n_freq, dtype=F32) / n_freq)
    pos = jnp.stack([r.reshape(-1), c.reshape(-1)], axis=1).astype(F32)
    ang = pos[:, :, None] * inv
    cos, sin = jnp.cos(ang), jnp.sin(ang)
    zero = jnp.zeros_like(sin)
    c64 = jnp.concatenate([cos, cos], axis=-1).reshape(L, HEAD_DIM)
    sa64 = jnp.concatenate([-sin, zero], axis=-1).reshape(L, HEAD_DIM)
    sb64 = jnp.concatenate([zero, sin], axis=-1).reshape(L, HEAD_DIM)
    return c64, sa64, sb64


def attention(p3, qn, kn, tabs, tq=1024):
    B, L, _ = p3.shape
    tq = _tile(L, tq)
    c64, sa64, sb64 = tabs
    tq_tabs = [jnp.tile(t, (1, ATT_Q_HEADS)) for t in (c64, sa64, sb64)]
    tk_tabs = [jnp.tile(t, (1, ATT_KV_HEADS)) for t in (c64, sa64, sb64)]
    qn_t = jnp.tile(qn.astype(F32), ATT_Q_HEADS).reshape(1, ATT_Q_W)
    kn_t = jnp.tile(kn.astype(F32), ATT_KV_HEADS).reshape(1, ATT_KV_W)

    def group_mean(width):
        g = jnp.arange(width) // HEAD_DIM
        return jnp.where(g[:, None] == g[None, :], 1.0 / HEAD_DIM, 0.0).astype(BF16)

    qtab = pl.BlockSpec((tq, ATT_Q_W), lambda b, i: (i, 0))
    ktab = pl.BlockSpec((L, ATT_KV_W), lambda b, i: (0, 0))
    return pl.pallas_call(
        _attn_kernel,
        grid=(B, L // tq),
        in_specs=[
            pl.BlockSpec((1, tq, ATT_Q_W), lambda b, i: (b, i, COL_AQ // ATT_Q_W)),
            pl.BlockSpec((1, L, ATT_KV_W), lambda b, i: (b, 0, COL_AK // ATT_KV_W)),
            pl.BlockSpec((1, L, ATT_KV_W), lambda b, i: (b, 0, COL_AV // ATT_KV_W)),
            pl.BlockSpec((1, ATT_Q_W), lambda b, i: (0, 0)),
            pl.BlockSpec((1, ATT_KV_W), lambda b, i: (0, 0)),
            qtab, qtab, qtab, ktab, ktab, ktab,
            pl.BlockSpec((ATT_Q_W, ATT_Q_W), lambda b, i: (0, 0)),
            pl.BlockSpec((ATT_KV_W, ATT_KV_W), lambda b, i: (0, 0)),
        ],
        out_specs=pl.BlockSpec((1, tq, ATT_Q_W), lambda b, i: (b, i, 0)),
        out_shape=jax.ShapeDtypeStruct((B, L, ATT_Q_W), BF16),
        scratch_shapes=[pltpu.VMEM((L, ATT_KV_W), BF16), pltpu.VMEM((L, 2 * ATT_KV_W), BF16)],
        compiler_params=_cparams("parallel", "arbitrary"),
        name="gqa_attention",
    )(p3, p3, p3, qn_t, kn_t, *tq_tabs, *tk_tabs, group_mean(ATT_Q_W), group_mean(ATT_KV_W))


def _dft_tables(L):
    n = 2 * L
    k = jnp.arange(L, dtype=jnp.int32)[:, None]
    j = jnp.arange(L, dtype=jnp.int32)[None, :]
    step = 64
    ja = jnp.arange(0, L, step, dtype=jnp.int32)[None, :]
    jb = jnp.arange(step, dtype=jnp.int32)[None, :]
    ang_a = ((k * ja) % n).astype(F32) * (2.0 * math.pi / n)
    ang_b = ((k * jb) % n).astype(F32) * (2.0 * math.pi / n)
    ca, sa = jnp.cos(ang_a)[:, :, None], jnp.sin(ang_a)[:, :, None]
    cb, sb = jnp.cos(ang_b)[:, None, :], jnp.sin(ang_b)[:, None, :]
    c = (ca * cb - sa * sb).reshape(L, L)
    s = -(sa * cb + ca * sb).reshape(L, L)
    bottom = jnp.where(k == 0, jnp.where(j % 2 == 0, 1.0, -1.0), s)
    bottom_t = jnp.where(j == 0, jnp.where(k % 2 == 0, 1.0, -1.0), s)
    fm = jnp.stack([c, bottom]).astype(BF16)
    g = jnp.concatenate([c, bottom_t], axis=1).astype(BF16)
    return fm, g


def _hyena_pos_features(L, t):
    t_norm = t / max(L - 1, 1)
    w = 2.0 * math.pi * t / L
    f = jnp.linspace(1e-4, HY_BANDS - 1, HY_BANDS, dtype=F32)
    fw = w[:, None] * f
    z = jnp.concatenate([t_norm[:, None], jnp.cos(fw), -jnp.sin(fw)], axis=-1)
    deltas = jnp.abs(jnp.linspace(math.log(HY_TARGET) / HY_FAST_DECAY,
                                  math.log(HY_TARGET) / HY_SLOW_DECAY, HY_WIDTH, dtype=F32))
    window = jnp.exp(-t_norm[:, None] * deltas)
    return z, window


def _sconv_kernel(u_ref, w_ref, o_ref):
    u = u_ref[0].astype(F32)
    L = u.shape[0]
    row = lax.broadcasted_iota(jnp.int32, u.shape, 0)
    prev = jnp.where(row == 0, 0.0, pltpu.roll(u, 1, 0))
    nxt = jnp.where(row == L - 1, 0.0, pltpu.roll(u, L - 1, 0))
    o_ref[0] = (prev * w_ref[0:1, :] + u * w_ref[1:2, :] + nxt * w_ref[2:3, :]).astype(o_ref.dtype)


def short_conv(p3, w):
    B, L, _ = p3.shape
    nblk = 3
    return pl.pallas_call(
        _sconv_kernel,
        grid=(B, nblk),
        in_specs=[
            pl.BlockSpec((1, L, HY_WIDTH), lambda b, c: (b, 0, COL_HY // HY_WIDTH + c)),
            pl.BlockSpec((3, HY_WIDTH), lambda b, c: (0, c)),
        ],
        out_specs=pl.BlockSpec((1, L, HY_WIDTH), lambda b, c: (b, 0, c)),
        out_shape=jax.ShapeDtypeStruct((B, L, nblk * HY_WIDTH), BF16),
        compiler_params=_cparams("parallel", "parallel"),
        name="hyena_short_conv",
    )(p3, w)


def _hyfilter_kernel(z_ref, w1_ref, b1_ref, w2_ref, b2_ref, w3_ref, b3_ref, fr_ref, win_ref,
                     f_ref):
    L = z_ref.shape[0]
    hp = lax.Precision.HIGHEST
    h = jnp.sin(fr_ref[0:1, :] * (jnp.dot(z_ref[...], w1_ref[...], precision=hp,
                                          preferred_element_type=F32) + b1_ref[...]))
    h = jnp.sin(fr_ref[1:2, :] * (jnp.dot(h, w2_ref[...], precision=hp,
                                          preferred_element_type=F32) + b2_ref[...]))
    f = (jnp.dot(h, w3_ref[0], precision=hp, preferred_element_type=F32)
         + b3_ref[0]) * win_ref[...]
    col = jnp.sum(jnp.abs(f), axis=0, keepdims=True)
    norm = col[:, :HY_WIDTH] + col[:, HY_WIDTH:] + EPS
    f_ref[:L, :] = (f[:, :HY_WIDTH] / norm).astype(f_ref.dtype)
    f_ref[L:, :] = (f[:, HY_WIDTH:] / norm).astype(f_ref.dtype)


def _block_diag2(w):
    z = jnp.zeros_like(w)
    return jnp.concatenate([jnp.concatenate([w, z], axis=1), jnp.concatenate([z, w], axis=1)], axis=0)


def hyena_filters(z_pair, win_pair, w1, b1, w2, b2, w3, b3, freq):
    L = z_pair.shape[0]
    pad_w1 = jnp.pad(w1, ((0, HY_FFN - HY_EMB), (0, 0)))
    two = lambda v: jnp.concatenate([v, v], axis=-1)
    w3r = w3.reshape(HY_FFN, HY_ORDER, 2, HY_WIDTH)
    zero = jnp.zeros((HY_ORDER, HY_FFN, HY_WIDTH), F32)
    w3n = jnp.moveaxis(w3r[:, :, 1], 1, 0)
    w3p = jnp.moveaxis(w3r[:, :, 0], 1, 0)
    w3bd = jnp.concatenate([jnp.concatenate([w3n, zero], axis=2),
                            jnp.concatenate([zero, w3p], axis=2)], axis=1)
    b3r = b3.reshape(HY_ORDER, 1, 2, HY_WIDTH)
    b3c = jnp.concatenate([b3r[:, :, 1], b3r[:, :, 0]], axis=-1)
    full = lambda shape: pl.BlockSpec(shape, lambda o: (0,) * len(shape))
    H2 = 2 * HY_FFN
    return pl.pallas_call(
        _hyfilter_kernel,
        grid=(HY_ORDER,),
        in_specs=[
            full((L, H2)), full((H2, H2)), full((1, H2)), full((H2, H2)), full((1, H2)),
            pl.BlockSpec((1, H2, 2 * HY_WIDTH), lambda o: (o, 0, 0)),
            pl.BlockSpec((1, 1, 2 * HY_WIDTH), lambda o: (o, 0, 0)),
            full((2, H2)), full((L, 2 * HY_WIDTH)),
        ],
        out_specs=pl.BlockSpec((2 * L, HY_WIDTH), lambda o: (0, o)),
        out_shape=jax.ShapeDtypeStruct((2 * L, HY_ORDER * HY_WIDTH), BF16),
        compiler_params=_cparams("parallel"),
        name="hyena_filter_mlp",
    )(z_pair, _block_diag2(pad_w1), two(b1.reshape(1, -1)), _block_diag2(w2),
      two(b2.reshape(1, -1)), w3bd, b3c, two(freq), win_pair)


def _spec_kernel(fm_ref, lo_ref, hi_ref, h_ref, *, n):
    P = fm_ref.shape[1]
    row = lax.broadcasted_iota(jnp.int32, (P, hi_ref.shape[1]), 0)
    sign = jnp.where(row % 2 == 0, 1.0, -1.0)
    scale = jnp.where(row == 0, 1.0 / n, 2.0 / n)
    for half in range(2):
        h_ref[0, half] = (_dot(fm_ref[half], hi_ref[...])
                          + sign * _dot(fm_ref[half], lo_ref[...])) * scale


def hyena_spectra(fm, f_all):
    P = fm.shape[1]
    W = f_all.shape[1]
    n_lag = f_all.shape[0] // P - 1
    return pl.pallas_call(
        functools.partial(_spec_kernel, n=2 * P),
        grid=(n_lag,),
        in_specs=[
            pl.BlockSpec((2, P, P), lambda i: (0, 0, 0)),
            pl.BlockSpec((P, W), lambda i: (i, 0)),
            pl.BlockSpec((P, W), lambda i: (i + 1, 0)),
        ],
        out_specs=pl.BlockSpec((1, 2, P, W), lambda i: (i, 0, 0, 0)),
        out_shape=jax.ShapeDtypeStruct((n_lag, 2, P, W), F32),
        compiler_params=_cparams("parallel"),
        name="hyena_filter_spectrum",
    )(fm, f_all, f_all)


HY_BLOCK = 512
HY_CONV_ROWS = 32


def _lconv_kernel(fm_ref, g_ref, u_ref, gate_ref, skip_ref, h_ref, o_ref, x_ref, yt_ref, yb_ref):
    P = fm_ref.shape[1]
    nb = u_ref.shape[1] // P
    ct = u_ref.shape[2]
    for j in range(nb):
        uj = u_ref[0, j * P:(j + 1) * P, :]
        x_ref[j, 0] = _dot(fm_ref[0], uj)
        x_ref[j, 1] = _dot(fm_ref[1], uj)
    first = lax.broadcasted_iota(jnp.int32, (HY_CONV_ROWS, ct), 0) == 0
    for i in range(nb):
        for r in range(P // HY_CONV_ROWS):
            rows = slice(r * HY_CONV_ROWS, (r + 1) * HY_CONV_ROWS)
            at = jnp.zeros((HY_CONV_ROWS, ct), F32)
            ab = jnp.zeros((HY_CONV_ROWS, ct), F32)
            for j in range(nb):
                d = i - j + nb - 1
                xt, xb = x_ref[j, 0, rows, :], x_ref[j, 1, rows, :]
                ht, hb = h_ref[d, 0, rows, :], h_ref[d, 1, rows, :]
                bb = xb * hb
                if r == 0:
                    at += xt * ht - jnp.where(first, 0.0, bb)
                    ab += jnp.where(first, bb, xt * hb + xb * ht)
                else:
                    at += xt * ht - bb
                    ab += xt * hb + xb * ht
            yt_ref[i, rows, :] = at.astype(BF16)
            yb_ref[i, rows, :] = ab.astype(BF16)
        y = _dot(g_ref[:, :P], yt_ref[i]) + _dot(g_ref[:, P:], yb_ref[i])
        blk = slice(i * P, (i + 1) * P)
        ui = u_ref[0, blk, :].astype(F32)
        o_ref[0, blk, :] = ((y + ui * skip_ref[...]) * gate_ref[0, blk, :].astype(F32)
                            ).astype(o_ref.dtype)


def long_conv(fm, g, spec, order, u, u_col, gate, gate_col, skip, ct=256):
    B, L, _ = u.shape
    P = fm.shape[1]
    nb = L // P
    n_lag = spec.shape[0]
    per = HY_WIDTH // ct
    return pl.pallas_call(
        _lconv_kernel,
        grid=(per, B),
        in_specs=[
            pl.BlockSpec((2, P, P), lambda c, b: (0, 0, 0)),
            pl.BlockSpec((P, 2 * P), lambda c, b: (0, 0)),
            pl.BlockSpec((1, L, ct), lambda c, b: (b, 0, u_col * per + c)),
            pl.BlockSpec((1, L, ct), lambda c, b: (b, 0, gate_col * per + c)),
            pl.BlockSpec((1, ct), lambda c, b: (0, c)),
            pl.BlockSpec((n_lag, 2, P, ct), lambda c, b: (0, 0, 0, order * per + c)),
        ],
        out_specs=pl.BlockSpec((1, L, ct), lambda c, b: (b, 0, c)),
        out_shape=jax.ShapeDtypeStruct((B, L, HY_WIDTH), BF16),
        scratch_shapes=[pltpu.VMEM((nb, 2, P, ct), F32), pltpu.VMEM((nb, P, ct), BF16),
                        pltpu.VMEM((nb, P, ct), BF16)],
        compiler_params=_cparams("parallel", "parallel"),
        name="hyena_long_conv",
    )(fm, g, u, gate, skip.reshape(1, HY_WIDTH).astype(F32), spec)


def hyena_mixer(p3, conv_w, spec, skip, fm, g):
    uc = short_conv(p3, conv_w.astype(F32))
    z = long_conv(fm, g, spec, 0, uc, 0, uc, 1, skip[0])
    return long_conv(fm, g, spec, 1, z, 0, uc, 2, skip[1])


def _log_sigmoid(x):
    return -(jnp.maximum(-x, 0.0) + jnp.log(1.0 + jnp.exp(-jnp.abs(x))))


def _dot_hilo(m, x):
    hi = x.astype(BF16)
    lo = (x - hi.astype(F32)).astype(BF16)
    return _dot(m, hi) + _dot(m, lo)


GLA_PREP_ROWS = 256


def _gla_kernel(q_ref, k_ref, v_ref, og_ref, lr_ref, wlr_ref, blr_ref, on_ref, o_ref,
                qd_ref, kit_ref, kst_ref, dcol_ref, acc_ref):
    L = q_ref.shape[1]
    C = GLA_CHUNK
    H = GLA_HEADS
    RB = GLA_PREP_ROWS
    n_chunks = L // C
    KW, VW, DV = GLA_K_W, GLA_V_W, GLA_DV

    pr = lax.broadcasted_iota(jnp.int32, (RB, RB), 0)
    pc = lax.broadcasted_iota(jnp.int32, (RB, RB), 1)
    same = (pr // C) == (pc // C)
    cum_f = jnp.where(same & (pc <= pr), 1.0, 0.0).astype(BF16)
    cum_b = jnp.where(same & (pc >= pr), 1.0, 0.0).astype(BF16)
    tot_m = jnp.where(same, 1.0, 0.0).astype(BF16)
    PW = 2 * C
    n_pairs = L // PW

    def prep(i, carry):
        r0 = pl.multiple_of(i * RB, RB)
        logit = _dot(lr_ref[0, pl.ds(r0, RB), :], wlr_ref[...]) + blr_ref[...]
        g = _log_sigmoid(logit) * (1.0 / GLA_NORMALIZER)
        q = q_ref[0, pl.ds(r0, RB), :].astype(F32) * (GLA_DK ** -0.5)
        k = k_ref[0, pl.ds(r0, RB), :].astype(F32)
        for d, cum in ((0, cum_f), (1, cum_b)):
            gd = g[:, d * KW:(d + 1) * KW]
            b = _dot_hilo(cum, gd)
            tot = _dot_hilo(tot_m, gd)
            qd_ref[d, pl.ds(r0, RB), :] = (q * jnp.exp(b)).astype(BF16)
            ki_t = (k * jnp.exp(-b)).T
            ks_t = (k * jnp.exp(tot - b)).T
            dec_t = jnp.exp(tot).T
            for p in range(RB // PW):
                rows = pl.ds(pl.multiple_of((i * (RB // PW) + p) * KW, KW), KW)
                cols = slice(p * PW, (p + 1) * PW)
                kit_ref[d, rows, :] = ki_t[:, cols].astype(BF16)
                kst_ref[d, rows, :] = ks_t[:, cols].astype(BF16)
                dcol_ref[d, rows, :] = dec_t[:, cols]
        return carry

    lax.fori_loop(0, L // RB, prep, 0)

    kk_blk = (lax.broadcasted_iota(jnp.int32, (KW, H * C), 0) // GLA_DK
              == lax.broadcasted_iota(jnp.int32, (KW, H * C), 1) // C)
    kv_blk = (lax.broadcasted_iota(jnp.int32, (KW, VW), 0) // GLA_DK
              == lax.broadcasted_iota(jnp.int32, (KW, VW), 1) // DV)
    arow = lax.broadcasted_iota(jnp.int32, (C, H * C), 0)
    acol = lax.broadcasted_iota(jnp.int32, (C, H * C), 1) % C
    amasks = (acol <= arow, acol > arow)
    zero_b = jnp.zeros((), BF16)

    def body(ip, sts):
        new = []
        for d in (0, 1):
            st = sts[d]
            pair = ip if d == 0 else n_pairs - 1 - ip
            rk = pl.multiple_of(pair * KW, KW)
            kit2 = kit_ref[d, pl.ds(rk, KW), :]
            kst2 = kst_ref[d, pl.ds(rk, KW), :]
            dec2 = dcol_ref[d, pl.ds(rk, KW), :]
            for half in ((0, 1) if d == 0 else (1, 0)):
                r0 = pl.multiple_of(pair * PW + half * C, C)
                cols = slice(half * C, (half + 1) * C)
                qd = qd_ref[d, pl.ds(r0, C), :]
                v = v_ref[0, pl.ds(r0, C), :]
                kk = jnp.where(kk_blk, jnp.concatenate([kit2[:, cols]] * H, axis=1), zero_b)
                a = jnp.where(amasks[d], _dot(qd, kk), 0.0).astype(BF16)
                v_bd = jnp.where(kv_blk, jnp.concatenate([v] * H, axis=0), zero_b)
                st_b = st.astype(BF16)
                s_bd = jnp.where(kv_blk, jnp.concatenate([st_b] * H, axis=1), zero_b)
                acc_ref[d, pl.ds(r0, C), :] = _dot(a, v_bd) + _dot(qd, s_bd)
                ds = jnp.concatenate(
                    [_dot(kst2[h * GLA_DK:(h + 1) * GLA_DK, cols], v[:, h * DV:(h + 1) * DV])
                     for h in range(H)], axis=0)
                st = st * jnp.concatenate([dec2[:, cols]] * (DV // C), axis=1) + ds
            new.append(st)
        return tuple(new)

    zero = jnp.zeros((KW, DV), F32)
    lax.fori_loop(0, n_pairs, body, (zero, zero), unroll=2)

    def finalize(i, carry):
        r0 = pl.multiple_of(i * RB, RB)
        o = acc_ref[0, pl.ds(r0, RB), :] + acc_ref[1, pl.ds(r0, RB), :]
        og = og_ref[0, pl.ds(r0, RB), :].astype(F32)
        outs = []
        for h in range(H):
            oh = o[:, h * GLA_DV:(h + 1) * GLA_DV]
            ms = jnp.mean(oh * oh, axis=-1, keepdims=True)
            outs.append(oh * lax.rsqrt(ms + EPS) * on_ref[...])
        y = jnp.concatenate(outs, axis=-1) * (og * _sigmoid(og))
        o_ref[0, pl.ds(r0, RB), :] = y.astype(o_ref.dtype)
        return carry

    lax.fori_loop(0, L // RB, finalize, 0)


def gla_mixer(p3, w_lr, b_lr, onorm):
    B, L, _ = p3.shape
    lr_w = 256
    wl = jnp.zeros((lr_w, 2 * GLA_K_W), F32)
    wl = wl.at[0:GLA_RANK, 0:GLA_K_W].set(w_lr[0].astype(F32))
    wl = wl.at[GLA_RANK:2 * GLA_RANK, GLA_K_W:].set(w_lr[1].astype(F32))
    bl = b_lr.astype(F32).reshape(1, 2 * GLA_K_W)
    col = lambda width, c: pl.BlockSpec((1, L, width), lambda b: (b, 0, c // width))
    return pl.pallas_call(
        _gla_kernel,
        grid=(B,),
        in_specs=[
            col(GLA_K_W, COL_GQ), col(GLA_K_W, COL_GK), col(GLA_V_W, COL_GV),
            col(GLA_V_W, COL_GO), col(lr_w, COL_LR),
            pl.BlockSpec((lr_w, 2 * GLA_K_W), lambda b: (0, 0)),
            pl.BlockSpec((1, 2 * GLA_K_W), lambda b: (0, 0)),
            pl.BlockSpec((1, GLA_DV), lambda b: (0, 0)),
        ],
        out_specs=pl.BlockSpec((1, L, GLA_V_W), lambda b: (b, 0, 0)),
        out_shape=jax.ShapeDtypeStruct((B, L, GLA_V_W), BF16),
        scratch_shapes=[
            pltpu.VMEM((2, L, GLA_K_W), BF16),
            pltpu.VMEM((2, L // (2 * GLA_CHUNK) * GLA_K_W, 2 * GLA_CHUNK), BF16),
            pltpu.VMEM((2, L // (2 * GLA_CHUNK) * GLA_K_W, 2 * GLA_CHUNK), BF16),
            pltpu.VMEM((2, L // (2 * GLA_CHUNK) * GLA_K_W, 2 * GLA_CHUNK), F32),
            pltpu.VMEM((2, L, GLA_V_W), F32),
        ],
        compiler_params=_cparams("parallel"),
        name="gla_mixer",
    )(p3, p3, p3, p3, p3, wl.astype(BF16), bl, onorm.astype(F32).reshape(1, GLA_DV))


def _merge_kernel(a_ref, b_ref, c_ref, ga_ref, gb_ref, gc_ref, h_ref, wa_ref, wb_ref, wc_ref,
                  wo_ref, o_ref):
    mixed = (_sigmoid(ga_ref[...].astype(F32)) * _dot(a_ref[...], wa_ref[...])
             + _sigmoid(gb_ref[...].astype(F32)) * _dot(b_ref[...], wb_ref[...])
             + _sigmoid(gc_ref[...].astype(F32)) * _dot(c_ref[...], wc_ref[...]))
    o_ref[...] = h_ref[...] + _dot(mixed.astype(BF16), wo_ref[...])


def merge_out(ya, yb, yc, p2, h, wa, wb, wc, wo, tm=512):
    M, D = h.shape
    tm = _tile(M, tm)
    br = lambda w: pl.BlockSpec((tm, w), lambda i: (i, 0))
    gate = lambda c: pl.BlockSpec((tm, D), lambda i: (i, COL_GATE // D + c))
    wfull = lambda r: pl.BlockSpec((r, D), lambda i: (0, 0))
    return pl.pallas_call(
        _merge_kernel,
        grid=(M // tm,),
        in_specs=[br(ATT_Q_W), br(HY_WIDTH), br(GLA_V_W), gate(0), gate(1), gate(2),
                  pl.BlockSpec((tm, D), lambda i: (i, 0)),
                  wfull(ATT_Q_W), wfull(HY_WIDTH), wfull(GLA_V_W), wfull(D)],
        out_specs=pl.BlockSpec((tm, D), lambda i: (i, 0)),
        out_shape=jax.ShapeDtypeStruct((M, D), F32),
        compiler_params=_cparams("parallel"),
        name="gated_merge_out_proj",
    )(ya, yb, yc, p2, p2, p2, h, wa, wb, wc, wo)


def _head_rmsnorm(x, g):
    ms = jnp.mean(x * x, axis=-1, keepdims=True)
    return x * lax.rsqrt(ms + EPS) * g


def _xattn_kernel(h_ref, ln_ref, wq_ref, k_ref, v_ref, qn_ref, kn_ref, wo_ref, o_ref, att_ref):
    h = h_ref[0]
    hn = _head_rmsnorm(h, ln_ref[...]).astype(BF16)
    q = _dot(hn, wq_ref[...])
    for hd in range(X_HEADS):
        sl = slice(hd * X_HEAD_DIM, (hd + 1) * X_HEAD_DIM)
        qh = (_head_rmsnorm(q[:, sl], qn_ref[...]) * (X_HEAD_DIM ** -0.5)).astype(BF16)
        kh = _head_rmsnorm(k_ref[0, :, sl].astype(F32), kn_ref[...]).astype(BF16)
        s = _dot_nt(qh, kh)
        p = jnp.exp(s - jnp.max(s, axis=-1, keepdims=True))
        l = jnp.sum(p, axis=-1, keepdims=True)
        att_ref[:, sl] = (_dot(p.astype(BF16), v_ref[0, :, sl]) / l).astype(BF16)
    o_ref[0] = h + _dot(att_ref[...], wo_ref[...])


def cross_attention(h3, ln, wq, kv3, qn, kn, wo, tl=1024):
    B, L, D = h3.shape
    Mm = kv3.shape[1]
    tl = _tile(L, tl)
    vec = lambda w: pl.BlockSpec((1, w), lambda b, i: (0, 0))
    mat = pl.BlockSpec((D, D), lambda b, i: (0, 0))
    return pl.pallas_call(
        _xattn_kernel,
        grid=(B, L // tl),
        in_specs=[
            pl.BlockSpec((1, tl, D), lambda b, i: (b, i, 0)),
            vec(D), mat,
            pl.BlockSpec((1, Mm, D), lambda b, i: (b, 0, 0)),
            pl.BlockSpec((1, Mm, D), lambda b, i: (b, 0, 1)),
            vec(X_HEAD_DIM), vec(X_HEAD_DIM), mat,
        ],
        out_specs=pl.BlockSpec((1, tl, D), lambda b, i: (b, i, 0)),
        out_shape=jax.ShapeDtypeStruct((B, L, D), F32),
        scratch_shapes=[pltpu.VMEM((tl, D), BF16)],
        compiler_params=_cparams("parallel", "parallel"),
        name="memory_cross_attention",
    )(h3, ln.astype(F32).reshape(1, D), wq, kv3, kv3, qn.astype(F32).reshape(1, X_HEAD_DIM),
      kn.astype(F32).reshape(1, X_HEAD_DIM), wo)


def _mlp_kernel(h_ref, g_ref, w1_ref, w2_ref, o_ref, hn_ref, acc_ref):
    j = pl.program_id(1)

    @pl.when(j == 0)
    def _():
        hn_ref[...] = _head_rmsnorm(h_ref[...], g_ref[...]).astype(BF16)
        acc_ref[...] = jnp.zeros_like(acc_ref)

    a = jnp.maximum(_dot(hn_ref[...], w1_ref[...]), 0.0)
    acc_ref[...] += _dot((a * a).astype(BF16), w2_ref[...])

    @pl.when(j == pl.num_programs(1) - 1)
    def _():
        o_ref[...] = h_ref[...] + acc_ref[...]


def mlp(h, g, w1, w2, tm=1024, tf=2048):
    M, D = h.shape
    F = w1.shape[1]
    tm, tf = _tile(M, tm), _tile(F, tf)
    return pl.pallas_call(
        _mlp_kernel,
        grid=(M // tm, F // tf),
        in_specs=[
            pl.BlockSpec((tm, D), lambda i, j: (i, 0)),
            pl.BlockSpec((1, D), lambda i, j: (0, 0)),
            pl.BlockSpec((D, tf), lambda i, j: (0, j)),
            pl.BlockSpec((tf, D), lambda i, j: (j, 0)),
        ],
        out_specs=pl.BlockSpec((tm, D), lambda i, j: (i, 0)),
        out_shape=jax.ShapeDtypeStruct((M, D), F32),
        scratch_shapes=[pltpu.VMEM((tm, D), BF16), pltpu.VMEM((tm, D), F32)],
        compiler_params=_cparams("parallel", "arbitrary"),
        name="relu2_mlp",
    )(h, g.astype(F32).reshape(1, D), w1, w2)


def _pack_w_in(w):
    kv0 = ATT_Q_W
    hy0 = kv0 + 2 * ATT_KV_W
    lr0 = hy0 + 3 * HY_WIDTH + 2 * GLA_K_W + 2 * GLA_V_W
    g0 = lr0 + 2 * GLA_RANK
    assert lr0 - hy0 == COL_AK - COL_HY and w.shape[2] - g0 == N_PACK - COL_GATE
    segments = ((0, kv0, COL_AQ), (kv0, hy0, COL_AK), (hy0, lr0, COL_HY),
                (lr0, g0, COL_LR), (g0, w.shape[2], COL_GATE))
    n_blk = N_PACK // PACK_BLOCK
    blk_a, blk_b, shifted, valid = [0] * n_blk, [0] * n_blk, [0] * n_blk, [0] * n_blk
    for src0, src1, dst in segments:
        for col in range(dst, dst + src1 - src0, PACK_BLOCK):
            src = src0 + col - dst
            assert col % PACK_BLOCK == 0 and src % PACK_BLOCK in (0, PACK_SHIFT)
            j = col // PACK_BLOCK
            blk_a[j] = src // PACK_BLOCK
            shifted[j] = int(src % PACK_BLOCK == PACK_SHIFT)
            blk_b[j] = blk_a[j] + 1 if shifted[j] else 0
            valid[j] = min(PACK_BLOCK, src1 - src)
    depth, D, _ = w.shape
    src_spec = lambda which: pl.BlockSpec(
        (1, D, PACK_BLOCK), lambda l, j, ba, bb, sh, vw: (l, 0, (ba, bb)[which][j]))
    return pl.pallas_call(
        _pack_kernel,
        grid_spec=pltpu.PrefetchScalarGridSpec(
            num_scalar_prefetch=4,
            grid=(depth, n_blk),
            in_specs=[src_spec(0), src_spec(1)],
            out_specs=pl.BlockSpec((1, D, PACK_BLOCK), lambda l, j, ba, bb, sh, vw: (l, 0, j)),
        ),
        out_shape=jax.ShapeDtypeStruct((depth, D, N_PACK), BF16),
        compiler_params=_cparams("parallel", "arbitrary"),
        name="pack_w_in",
    )(*(jnp.asarray(t, jnp.int32) for t in (blk_a, blk_b, shifted, valid)), w, w)


PACK_BLOCK = 256
PACK_SHIFT = 32


def _pack_kernel(ba_ref, bb_ref, sh_ref, vw_ref, a_ref, b_ref, o_ref):
    j = pl.program_id(1)
    a = a_ref[0]
    moved = jnp.concatenate([a[:, PACK_SHIFT:], b_ref[0, :, :PACK_SHIFT]], axis=1)
    x = jnp.where(sh_ref[j] == 1, moved, a)
    lane = lax.broadcasted_iota(jnp.int32, x.shape, 1)
    o_ref[0] = jnp.where(lane < vw_ref[j], x, 0.0).astype(o_ref.dtype)


def kernel(x, mem, ln_mix, w_in, attn_qnorm, attn_knorm, hy_conv, hy_w1, hy_b1, hy_w2, hy_b2,
           hy_w3, hy_b3, hy_freq, hy_skip, gla_w_lr, gla_b_lr, gla_onorm, w_br_attn, w_br_hyena,
           w_br_gla, w_out, ln_x, ln_mem, x_wq, x_wk, x_wv, x_wo, x_qnorm, x_knorm, ln_mlp,
           mlp_w1, mlp_w2):
    B, L, D = x.shape
    Mm = mem.shape[1]
    depth = w_in.shape[0]
    M = B * L

    rope_tabs = _rope_tables(L)
    m = jnp.arange(L)
    z_neg, win_neg = _hyena_pos_features(L, (L - m).astype(F32))
    z_fwd, win_fwd = _hyena_pos_features(L, m.astype(F32))
    win_neg = jnp.where((m == 0)[:, None], 0.0, win_neg)
    zpad = lambda z: jnp.pad(z, ((0, 0), (0, HY_FFN - HY_EMB)))
    z_pair = jnp.concatenate([zpad(z_neg), zpad(z_fwd)], axis=1)
    win_pair = jnp.concatenate([win_neg, win_fwd], axis=1)
    fm, g_inv = _dft_tables(min(L, HY_BLOCK))
    bf = lambda a: a.astype(BF16)

    w_in_packed = _pack_w_in(w_in)
    h = x.astype(F32).reshape(M, D)
    mem2 = mem.astype(F32).reshape(B * Mm, D)
    for i in range(depth):
        p2 = norm_matmul(h, ln_mix[i].astype(F32), w_in_packed[i], BF16)
        p3 = p2.reshape(B, L, N_PACK)
        y_a = attention(p3, attn_qnorm[i], attn_knorm[i], rope_tabs)
        f_all = hyena_filters(z_pair, win_pair, hy_w1[i].astype(F32), hy_b1[i].astype(F32),
                              hy_w2[i].astype(F32), hy_b2[i].astype(F32), hy_w3[i].astype(F32),
                              hy_b3[i].astype(F32), hy_freq[i].astype(F32))
        spec = hyena_spectra(fm, f_all)
        y_b = hyena_mixer(p3, hy_conv[i], spec, hy_skip[i], fm, g_inv)
        y_c = gla_mixer(p3, gla_w_lr[i], gla_b_lr[i], gla_onorm[i])
        h = merge_out(y_a.reshape(M, ATT_Q_W), y_b.reshape(M, HY_WIDTH), y_c.reshape(M, GLA_V_W),
                      p2, h, bf(w_br_attn[i]), bf(w_br_hyena[i]), bf(w_br_gla[i]), bf(w_out[i]))
        wkv = jnp.concatenate([x_wk[i], x_wv[i]], axis=1)
        kv = norm_matmul(mem2, ln_mem[i].astype(F32), bf(wkv), BF16).reshape(B, Mm, 2 * D)
        h = cross_attention(h.reshape(B, L, D), ln_x[i], bf(x_wq[i]), kv, x_qnorm[i],
                            x_knorm[i], bf(x_wo[i])).reshape(M, D)
        h = mlp(h, ln_mlp[i], bf(mlp_w1[i]), bf(mlp_w2[i]))
    return h.reshape(B, L, D).astype(x.dtype)
```

```python
import functools
import math

import jax
import jax.numpy as jnp
from jax import lax
from jax.experimental import pallas as pl
from jax.experimental.pallas import tpu as pltpu

F32 = jnp.float32
BF16 = jnp.bfloat16

D_MODEL = 1024
GRID_W = 64
ROPE_THETA = 10000.0
HEAD_DIM = 64
ATT_Q_HEADS = 8
ATT_KV_HEADS = 2
ATT_Q_W = ATT_Q_HEADS * HEAD_DIM
ATT_KV_W = ATT_KV_HEADS * HEAD_DIM
HY_WIDTH = 512
HY_ORDER = 2
HY_BANDS = 16
HY_EMB = 1 + 2 * HY_BANDS
HY_FFN = 64
HY_FAST_DECAY = 0.3
HY_SLOW_DECAY = 1.5
HY_TARGET = 1e-2
GLA_HEADS = 4
GLA_DK = 64
GLA_DV = 128
GLA_RANK = 16
GLA_NORMALIZER = 16.0
GLA_CHUNK = 64
GLA_K_W = GLA_HEADS * GLA_DK
GLA_V_W = GLA_HEADS * GLA_DV
X_HEADS = 4
X_HEAD_DIM = D_MODEL // X_HEADS
D_FF = 4 * D_MODEL
N_BRANCH = 3
EPS = 1e-6

COL_AQ = 0
COL_HY = 512
COL_GQ = 2048
COL_GK = 2304
COL_GV = 2560
COL_GO = 3072
COL_AK = 3584
COL_AV = 3712
COL_LR = 3840
COL_GATE = 4096
N_PACK = 7168

VMEM_LIMIT_BYTES = 52 * 1024 * 1024


def _cparams(*sem):
    return pltpu.CompilerParams(dimension_semantics=sem, vmem_limit_bytes=VMEM_LIMIT_BYTES)


def _tile(n, t):
    t = min(n, t)
    assert n % t == 0, (n, t)
    return t


def _dot(a, b):
    return jnp.dot(a, b, preferred_element_type=F32)


def _dot_nt(a, b):
    return lax.dot_general(a, b, (((1,), (1,)), ((), ())), preferred_element_type=F32)


def _dot_tn(a, b):
    return lax.dot_general(a, b, (((0,), (0,)), ((), ())), preferred_element_type=F32)


def _sigmoid(x):
    return 1.0 / (1.0 + jnp.exp(-x))


def _norm_mm_kernel(x_ref, g_ref, w_ref, o_ref, xn_ref):
    @pl.when(pl.program_id(1) == 0)
    def _():
        x = x_ref[...]
        ms = jnp.mean(x * x, axis=-1, keepdims=True)
        xn_ref[...] = (x * lax.rsqrt(ms + EPS) * g_ref[...]).astype(BF16)

    o_ref[...] = _dot(xn_ref[...], w_ref[...]).astype(o_ref.dtype)


def norm_matmul(x, g, w, layer, out_dtype, tm=1024, tn=3584):
    M, K = x.shape
    N = w.shape[2]
    tm, tn = _tile(M, tm), _tile(N, tn)
    return pl.pallas_call(
        _norm_mm_kernel,
        grid=(M // tm, N // tn),
        in_specs=[
            pl.BlockSpec((tm, K), lambda i, j: (i, 0)),
            pl.BlockSpec((1, K), lambda i, j: (0, 0)),
            pl.BlockSpec((None, K, tn), lambda i, j: (layer, 0, j)),
        ],
        out_specs=pl.BlockSpec((tm, tn), lambda i, j: (i, j)),
        out_shape=jax.ShapeDtypeStruct((M, N), out_dtype),
        scratch_shapes=[pltpu.VMEM((tm, K), BF16)],
        compiler_params=_cparams("parallel", "arbitrary"),
        name="norm_matmul",
    )(x, g.reshape(1, K), w)


def _group_mean_sq(x, gm_ref):
    return _dot((x * x).astype(BF16), gm_ref[...])


def _rope(x, c_ref, sa_ref, sb_ref):
    w = x.shape[-1]
    return (x * c_ref[...] + pltpu.roll(x, w - HEAD_DIM // 4, 1) * sa_ref[...]
            + pltpu.roll(x, HEAD_DIM // 4, 1) * sb_ref[...])


def _attn_kernel(q_ref, k_ref, v_ref, qn_ref, kn_ref, cq_ref, saq_ref, sbq_ref,
                 ck_ref, sak_ref, sbk_ref, gmq_ref, gmk_ref, o_ref, kp_ref, vx_ref):
    @pl.when(pl.program_id(1) == 0)
    def _():
        k = k_ref[0].astype(F32)
        kh = k * lax.rsqrt(_group_mean_sq(k, gmk_ref) + EPS) * kn_ref[...]
        kp_ref[...] = _rope(kh, ck_ref, sak_ref, sbk_ref).astype(BF16)
        v = v_ref[0]
        ones = jnp.ones((v.shape[0], HEAD_DIM), BF16)
        vx_ref[...] = jnp.concatenate(
            [piece for kv in range(ATT_KV_HEADS)
             for piece in (v[:, kv * HEAD_DIM:(kv + 1) * HEAD_DIM], ones)], axis=-1)

    q = q_ref[0].astype(F32)
    qh = q * lax.rsqrt(_group_mean_sq(q, gmq_ref) + EPS) * qn_ref[...]
    qb = (_rope(qh, cq_ref, saq_ref, sbq_ref) * (HEAD_DIM ** -0.5 * math.log2(math.e))).astype(BF16)
    group = ATT_Q_HEADS // ATT_KV_HEADS
    outs = []
    for h in range(ATT_Q_HEADS):
        kv = h // group
        kk = kp_ref[:, kv * HEAD_DIM:(kv + 1) * HEAD_DIM]
        s = _dot_nt(qb[:, h * HEAD_DIM:(h + 1) * HEAD_DIM], kk)
        p = jnp.exp2(s - jnp.max(s, axis=-1, keepdims=True))
        ox = _dot(p.astype(BF16), vx_ref[:, kv * 2 * HEAD_DIM:(kv + 1) * 2 * HEAD_DIM])
        outs.append(ox[:, :HEAD_DIM] / ox[:, HEAD_DIM:])
    o_ref[0] = jnp.concatenate(outs, axis=-1).astype(o_ref.dtype)


def _rope_tables(L):
    rows = L // GRID_W
    r, c = jnp.meshgrid(jnp.arange(rows), jnp.arange(GRID_W), indexing="ij")
    n_freq = HEAD_DIM // 4
    inv = ROPE_THETA ** (-jnp.arange(n_freq, dtype=F32) / n_freq)
    pos = jnp.stack([r.reshape(-1), c.reshape(-1)], axis=1).astype(F32)
    ang = pos[:, :, None] * inv
    cos, sin = jnp.cos(ang), jnp.sin(ang)
    zero = jnp.zeros_like(sin)
    c64 = jnp.concatenate([cos, cos], axis=-1).reshape(L, HEAD_DIM)
    sa64 = jnp.concatenate([-sin, zero], axis=-1).reshape(L, HEAD_DIM)
    sb64 = jnp.concatenate([zero, sin], axis=-1).reshape(L, HEAD_DIM)
    return c64, sa64, sb64


def attention(p3, qn, kn, tabs, tq=1024):
    B, L, _ = p3.shape
    tq = _tile(L, tq)
    c64, sa64, sb64 = tabs
    tq_tabs = [jnp.tile(t, (1, ATT_Q_HEADS)) for t in (c64, sa64, sb64)]
    tk_tabs = [jnp.tile(t, (1, ATT_KV_HEADS)) for t in (c64, sa64, sb64)]
    qn_t = jnp.tile(qn.astype(F32), ATT_Q_HEADS).reshape(1, ATT_Q_W)
    kn_t = jnp.tile(kn.astype(F32), ATT_KV_HEADS).reshape(1, ATT_KV_W)

    def group_mean(width):
        g = jnp.arange(width) // HEAD_DIM
        return jnp.where(g[:, None] == g[None, :], 1.0 / HEAD_DIM, 0.0).astype(BF16)

    qtab = pl.BlockSpec((tq, ATT_Q_W), lambda b, i: (i, 0))
    ktab = pl.BlockSpec((L, ATT_KV_W), lambda b, i: (0, 0))
    return pl.pallas_call(
        _attn_kernel,
        grid=(B, L // tq),
        in_specs=[
            pl.BlockSpec((1, tq, ATT_Q_W), lambda b, i: (b, i, COL_AQ // ATT_Q_W)),
            pl.BlockSpec((1, L, ATT_KV_W), lambda b, i: (b, 0, COL_AK // ATT_KV_W)),
            pl.BlockSpec((1, L, ATT_KV_W), lambda b, i: (b, 0, COL_AV // ATT_KV_W)),
            pl.BlockSpec((1, ATT_Q_W), lambda b, i: (0, 0)),
            pl.BlockSpec((1, ATT_KV_W), lambda b, i: (0, 0)),
            qtab, qtab, qtab, ktab, ktab, ktab,
            pl.BlockSpec((ATT_Q_W, ATT_Q_W), lambda b, i: (0, 0)),
            pl.BlockSpec((ATT_KV_W, ATT_KV_W), lambda b, i: (0, 0)),
        ],
        out_specs=pl.BlockSpec((1, tq, ATT_Q_W), lambda b, i: (b, i, 0)),
        out_shape=jax.ShapeDtypeStruct((B, L, ATT_Q_W), BF16),
        scratch_shapes=[pltpu.VMEM((L, ATT_KV_W), BF16), pltpu.VMEM((L, 2 * ATT_KV_W), BF16)],
        compiler_params=_cparams("parallel", "arbitrary"),
        name="gqa_attention",
    )(p3, p3, p3, qn_t, kn_t, *tq_tabs, *tk_tabs, group_mean(ATT_Q_W), group_mean(ATT_KV_W))


def _dft_tables(L):
    n = 2 * L
    k = jnp.arange(L, dtype=jnp.int32)[:, None]
    j = jnp.arange(L, dtype=jnp.int32)[None, :]
    step = 64
    ja = jnp.arange(0, L, step, dtype=jnp.int32)[None, :]
    jb = jnp.arange(step, dtype=jnp.int32)[None, :]
    ang_a = ((k * ja) % n).astype(F32) * (2.0 * math.pi / n)
    ang_b = ((k * jb) % n).astype(F32) * (2.0 * math.pi / n)
    ca, sa = jnp.cos(ang_a)[:, :, None], jnp.sin(ang_a)[:, :, None]
    cb, sb = jnp.cos(ang_b)[:, None, :], jnp.sin(ang_b)[:, None, :]
    c = (ca * cb - sa * sb).reshape(L, L)
    s = -(sa * cb + ca * sb).reshape(L, L)
    bottom = jnp.where(k == 0, jnp.where(j % 2 == 0, 1.0, -1.0), s)
    bottom_t = jnp.where(j == 0, jnp.where(k % 2 == 0, 1.0, -1.0), s)
    fm = jnp.stack([c, bottom]).astype(BF16)
    g = jnp.concatenate([c, bottom_t], axis=1).astype(BF16)
    return fm, g


def _hyena_pos_features(L, t):
    t_norm = t / max(L - 1, 1)
    w = 2.0 * math.pi * t / L
    f = jnp.linspace(1e-4, HY_BANDS - 1, HY_BANDS, dtype=F32)
    fw = w[:, None] * f
    z = jnp.concatenate([t_norm[:, None], jnp.cos(fw), -jnp.sin(fw)], axis=-1)
    deltas = jnp.abs(jnp.linspace(math.log(HY_TARGET) / HY_FAST_DECAY,
                                  math.log(HY_TARGET) / HY_SLOW_DECAY, HY_WIDTH, dtype=F32))
    window = jnp.exp(-t_norm[:, None] * deltas)
    return z, window


def _sconv_kernel(u_ref, w_ref, o_ref):
    u = u_ref[0].astype(F32)
    L = u.shape[0]
    row = lax.broadcasted_iota(jnp.int32, u.shape, 0)
    prev = jnp.where(row == 0, 0.0, pltpu.roll(u, 1, 0))
    nxt = jnp.where(row == L - 1, 0.0, pltpu.roll(u, L - 1, 0))
    o_ref[0] = (prev * w_ref[0:1, :] + u * w_ref[1:2, :] + nxt * w_ref[2:3, :]).astype(o_ref.dtype)


def short_conv(p3, w):
    B, L, _ = p3.shape
    nblk = 3
    return pl.pallas_call(
        _sconv_kernel,
        grid=(B, nblk),
        in_specs=[
            pl.BlockSpec((1, L, HY_WIDTH), lambda b, c: (b, 0, COL_HY // HY_WIDTH + c)),
            pl.BlockSpec((3, HY_WIDTH), lambda b, c: (0, c)),
        ],
        out_specs=pl.BlockSpec((1, L, HY_WIDTH), lambda b, c: (b, 0, c)),
        out_shape=jax.ShapeDtypeStruct((B, L, nblk * HY_WIDTH), BF16),
        compiler_params=_cparams("parallel", "parallel"),
        name="hyena_short_conv",
    )(p3, w)


def _hyfilter_kernel(z_ref, w1_ref, b1_ref, w2_ref, b2_ref, w3_ref, b3_ref, fr_ref, win_ref,
                     f_ref):
    L = z_ref.shape[0]
    hp = lax.Precision.HIGHEST
    h = jnp.sin(fr_ref[0:1, :] * (jnp.dot(z_ref[...], w1_ref[...], precision=hp,
                                          preferred_element_type=F32) + b1_ref[...]))
    h = jnp.sin(fr_ref[1:2, :] * (jnp.dot(h, w2_ref[...], precision=hp,
                                          preferred_element_type=F32) + b2_ref[...]))
    f = (jnp.dot(h, w3_ref[0], precision=hp, preferred_element_type=F32)
         + b3_ref[0]) * win_ref[...]
    col = jnp.sum(jnp.abs(f), axis=0, keepdims=True)
    norm = col[:, :HY_WIDTH] + col[:, HY_WIDTH:] + EPS
    f_ref[:L, :] = (f[:, :HY_WIDTH] / norm).astype(f_ref.dtype)
    f_ref[L:, :] = (f[:, HY_WIDTH:] / norm).astype(f_ref.dtype)


def _block_diag2(w):
    z = jnp.zeros_like(w)
    return jnp.concatenate([jnp.concatenate([w, z], axis=1), jnp.concatenate([z, w], axis=1)], axis=0)


def hyena_filters(z_pair, win_pair, w1, b1, w2, b2, w3, b3, freq):
    L = z_pair.shape[0]
    pad_w1 = jnp.pad(w1, ((0, HY_FFN - HY_EMB), (0, 0)))
    two = lambda v: jnp.concatenate([v, v], axis=-1)
    w3r = w3.reshape(HY_FFN, HY_ORDER, 2, HY_WIDTH)
    zero = jnp.zeros((HY_ORDER, HY_FFN, HY_WIDTH), F32)
    w3n = jnp.moveaxis(w3r[:, :, 1], 1, 0)
    w3p = jnp.moveaxis(w3r[:, :, 0], 1, 0)
    w3bd = jnp.concatenate([jnp.concatenate([w3n, zero], axis=2),
                            jnp.concatenate([zero, w3p], axis=2)], axis=1)
    b3r = b3.reshape(HY_ORDER, 1, 2, HY_WIDTH)
    b3c = jnp.concatenate([b3r[:, :, 1], b3r[:, :, 0]], axis=-1)
    full = lambda shape: pl.BlockSpec(shape, lambda o: (0,) * len(shape))
    H2 = 2 * HY_FFN
    return pl.pallas_call(
        _hyfilter_kernel,
        grid=(HY_ORDER,),
        in_specs=[
            full((L, H2)), full((H2, H2)), full((1, H2)), full((H2, H2)), full((1, H2)),
            pl.BlockSpec((1, H2, 2 * HY_WIDTH), lambda o: (o, 0, 0)),
            pl.BlockSpec((1, 1, 2 * HY_WIDTH), lambda o: (o, 0, 0)),
            full((2, H2)), full((L, 2 * HY_WIDTH)),
        ],
        out_specs=pl.BlockSpec((2 * L, HY_WIDTH), lambda o: (0, o)),
        out_shape=jax.ShapeDtypeStruct((2 * L, HY_ORDER * HY_WIDTH), BF16),
        compiler_params=_cparams("parallel"),
        name="hyena_filter_mlp",
    )(z_pair, _block_diag2(pad_w1), two(b1.reshape(1, -1)), _block_diag2(w2),
      two(b2.reshape(1, -1)), w3bd, b3c, two(freq), win_pair)


def _spec_kernel(fm_ref, lo_ref, hi_ref, h_ref, *, n):
    P = fm_ref.shape[1]
    row = lax.broadcasted_iota(jnp.int32, (P, hi_ref.shape[1]), 0)
    sign = jnp.where(row % 2 == 0, 1.0, -1.0)
    scale = jnp.where(row == 0, 1.0 / n, 2.0 / n)
    for half in range(2):
        h_ref[0, half] = (_dot(fm_ref[half], hi_ref[...])
                          + sign * _dot(fm_ref[half], lo_ref[...])) * scale


def hyena_spectra(fm, f_all):
    P = fm.shape[1]
    W = f_all.shape[1]
    n_lag = f_all.shape[0] // P - 1
    return pl.pallas_call(
        functools.partial(_spec_kernel, n=2 * P),
        grid=(n_lag,),
        in_specs=[
            pl.BlockSpec((2, P, P), lambda i: (0, 0, 0)),
            pl.BlockSpec((P, W), lambda i: (i, 0)),
            pl.BlockSpec((P, W), lambda i: (i + 1, 0)),
        ],
        out_specs=pl.BlockSpec((1, 2, P, W), lambda i: (i, 0, 0, 0)),
        out_shape=jax.ShapeDtypeStruct((n_lag, 2, P, W), F32),
        compiler_params=_cparams("parallel"),
        name="hyena_filter_spectrum",
    )(fm, f_all, f_all)


HY_BLOCK = 512
HY_CONV_ROWS = 16


def _lconv_kernel(fm_ref, g_ref, u_ref, gate_ref, skip_ref, h_ref, o_ref, x_ref, yt_ref, yb_ref):
    P = fm_ref.shape[1]
    nbat = u_ref.shape[0]
    nb = u_ref.shape[1] // P
    ct = u_ref.shape[2]
    for s in range(nbat):
        for j in range(nb):
            uj = u_ref[s, j * P:(j + 1) * P, :]
            x_ref[s, j, 0] = _dot(fm_ref[0], uj)
            x_ref[s, j, 1] = _dot(fm_ref[1], uj)
    first = lax.broadcasted_iota(jnp.int32, (HY_CONV_ROWS, ct), 0) == 0
    for i in range(nb):
        for r in range(P // HY_CONV_ROWS):
            rows = slice(r * HY_CONV_ROWS, (r + 1) * HY_CONV_ROWS)
            at = [jnp.zeros((HY_CONV_ROWS, ct), F32) for _ in range(nbat)]
            ab = [jnp.zeros((HY_CONV_ROWS, ct), F32) for _ in range(nbat)]
            for j in range(nb):
                d = i - j + nb - 1
                ht, hb = h_ref[d, 0, rows, :], h_ref[d, 1, rows, :]
                for s in range(nbat):
                    xt, xb = x_ref[s, j, 0, rows, :], x_ref[s, j, 1, rows, :]
                    bb = xb * hb
                    if r == 0:
                        at[s] += xt * ht - jnp.where(first, 0.0, bb)
                        ab[s] += jnp.where(first, bb, xt * hb + xb * ht)
                    else:
                        at[s] += xt * ht - bb
                        ab[s] += xt * hb + xb * ht
            for s in range(nbat):
                yt_ref[s, i, rows, :] = at[s].astype(BF16)
                yb_ref[s, i, rows, :] = ab[s].astype(BF16)
        blk = slice(i * P, (i + 1) * P)
        for s in range(nbat):
            y = _dot(g_ref[:, :P], yt_ref[s, i]) + _dot(g_ref[:, P:], yb_ref[s, i])
            ui = u_ref[s, blk, :].astype(F32)
            o_ref[s, blk, :] = ((y + ui * skip_ref[...]) * gate_ref[s, blk, :].astype(F32)
                                ).astype(o_ref.dtype)


HY_CONV_BATCH = 1


def long_conv(fm, g, spec, order, u, u_col, gate, gate_col, skip, ct=256):
    B, L, _ = u.shape
    P = fm.shape[1]
    nb = L // P
    n_lag = spec.shape[0]
    per = HY_WIDTH // ct
    nbat = _tile(B, HY_CONV_BATCH)
    return pl.pallas_call(
        _lconv_kernel,
        grid=(per, B // nbat),
        in_specs=[
            pl.BlockSpec((2, P, P), lambda c, b: (0, 0, 0)),
            pl.BlockSpec((P, 2 * P), lambda c, b: (0, 0)),
            pl.BlockSpec((nbat, L, ct), lambda c, b: (b, 0, u_col * per + c)),
            pl.BlockSpec((nbat, L, ct), lambda c, b: (b, 0, gate_col * per + c)),
            pl.BlockSpec((1, ct), lambda c, b: (0, c)),
            pl.BlockSpec((n_lag, 2, P, ct), lambda c, b: (0, 0, 0, order * per + c)),
        ],
        out_specs=pl.BlockSpec((nbat, L, ct), lambda c, b: (b, 0, c)),
        out_shape=jax.ShapeDtypeStruct((B, L, HY_WIDTH), BF16),
        scratch_shapes=[pltpu.VMEM((nbat, nb, 2, P, ct), F32), pltpu.VMEM((nbat, nb, P, ct), BF16),
                        pltpu.VMEM((nbat, nb, P, ct), BF16)],
        compiler_params=_cparams("parallel", "parallel"),
        name="hyena_long_conv",
    )(fm, g, u, gate, skip.reshape(1, HY_WIDTH).astype(F32), spec)


def hyena_mixer(p3, conv_w, spec, skip, fm, g):
    uc = short_conv(p3, conv_w.astype(F32))
    z = long_conv(fm, g, spec, 0, uc, 0, uc, 1, skip[0])
    return long_conv(fm, g, spec, 1, z, 0, uc, 2, skip[1])


def _log_sigmoid(x):
    return -(jnp.maximum(-x, 0.0) + jnp.log(1.0 + jnp.exp(-jnp.abs(x))))


def _dot_hilo(m, x):
    hi = x.astype(BF16)
    lo = (x - hi.astype(F32)).astype(BF16)
    return _dot(m, hi) + _dot(m, lo)


GLA_PREP_ROWS = 256


def _gla_kernel(q_ref, k_ref, v_ref, og_ref, lr_ref, wlr_ref, blr_ref, on_ref, o_ref,
                qd_ref, kit_ref, kst_ref, dcol_ref, acc_ref):
    L = q_ref.shape[1]
    C = GLA_CHUNK
    H = GLA_HEADS
    RB = GLA_PREP_ROWS
    n_chunks = L // C
    KW, VW, DV = GLA_K_W, GLA_V_W, GLA_DV

    pr = lax.broadcasted_iota(jnp.int32, (RB, RB), 0)
    pc = lax.broadcasted_iota(jnp.int32, (RB, RB), 1)
    same = (pr // C) == (pc // C)
    cum_f = jnp.where(same & (pc <= pr), 1.0, 0.0).astype(BF16)
    cum_b = jnp.where(same & (pc >= pr), 1.0, 0.0).astype(BF16)
    tot_m = jnp.where(same, 1.0, 0.0).astype(BF16)
    PW = 2 * C
    n_pairs = L // PW

    def prep(i, carry):
        r0 = pl.multiple_of(i * RB, RB)
        logit = _dot(lr_ref[0, pl.ds(r0, RB), :], wlr_ref[...]) + blr_ref[...]
        g = _log_sigmoid(logit) * (1.0 / GLA_NORMALIZER)
        q = q_ref[0, pl.ds(r0, RB), :].astype(F32) * (GLA_DK ** -0.5)
        k = k_ref[0, pl.ds(r0, RB), :].astype(F32)
        for d, cum in ((0, cum_f), (1, cum_b)):
            gd = g[:, d * KW:(d + 1) * KW]
            b = _dot_hilo(cum, gd)
            tot = _dot_hilo(tot_m, gd)
            qd_ref[d, pl.ds(r0, RB), :] = (q * jnp.exp(b)).astype(BF16)
            ki_t = (k * jnp.exp(-b)).T
            ks_t = (k * jnp.exp(tot - b)).T
            dec_t = jnp.exp(tot).T
            for p in range(RB // PW):
                rows = pl.ds(pl.multiple_of((i * (RB // PW) + p) * KW, KW), KW)
                cols = slice(p * PW, (p + 1) * PW)
                kit_ref[d, rows, :] = ki_t[:, cols].astype(BF16)
                kst_ref[d, rows, :] = ks_t[:, cols].astype(BF16)
                dcol_ref[d, rows, :] = dec_t[:, cols]
        return carry

    lax.fori_loop(0, L // RB, prep, 0)

    kk_blk = (lax.broadcasted_iota(jnp.int32, (KW, H * C), 0) // GLA_DK
              == lax.broadcasted_iota(jnp.int32, (KW, H * C), 1) // C)
    kv_blk = (lax.broadcasted_iota(jnp.int32, (KW, VW), 0) // GLA_DK
              == lax.broadcasted_iota(jnp.int32, (KW, VW), 1) // DV)
    arow = lax.broadcasted_iota(jnp.int32, (C, H * C), 0)
    acol = lax.broadcasted_iota(jnp.int32, (C, H * C), 1) % C
    amasks = (acol <= arow, acol > arow)
    zero_b = jnp.zeros((), BF16)

    def body(ip, sts):
        new = []
        for d in (0, 1):
            st = sts[d]
            pair = ip if d == 0 else n_pairs - 1 - ip
            rk = pl.multiple_of(pair * KW, KW)
            kit2 = kit_ref[d, pl.ds(rk, KW), :]
            kst2 = kst_ref[d, pl.ds(rk, KW), :]
            dec2 = dcol_ref[d, pl.ds(rk, KW), :]
            for half in ((0, 1) if d == 0 else (1, 0)):
                r0 = pl.multiple_of(pair * PW + half * C, C)
                cols = slice(half * C, (half + 1) * C)
                qd = qd_ref[d, pl.ds(r0, C), :]
                v = v_ref[0, pl.ds(r0, C), :]
                kk = jnp.where(kk_blk, jnp.concatenate([kit2[:, cols]] * H, axis=1), zero_b)
                a = jnp.where(amasks[d], _dot(qd, kk), 0.0).astype(BF16)
                v_bd = jnp.where(kv_blk, jnp.concatenate([v] * H, axis=0), zero_b)
                st_b = st.astype(BF16)
                s_bd = jnp.where(kv_blk, jnp.concatenate([st_b] * H, axis=1), zero_b)
                acc_ref[d, pl.ds(r0, C), :] = _dot(a, v_bd) + _dot(qd, s_bd)
                ds = jnp.concatenate(
                    [_dot(kst2[h * GLA_DK:(h + 1) * GLA_DK, cols], v[:, h * DV:(h + 1) * DV])
                     for h in range(H)], axis=0)
                st = st * jnp.concatenate([dec2[:, cols]] * (DV // C), axis=1) + ds
            new.append(st)
        return tuple(new)

    zero = jnp.zeros((KW, DV), F32)
    lax.fori_loop(0, n_pairs, body, (zero, zero), unroll=2)

    def finalize(i, carry):
        r0 = pl.multiple_of(i * RB, RB)
        o = acc_ref[0, pl.ds(r0, RB), :] + acc_ref[1, pl.ds(r0, RB), :]
        og = og_ref[0, pl.ds(r0, RB), :].astype(F32)
        outs = []
        for h in range(H):
            oh = o[:, h * GLA_DV:(h + 1) * GLA_DV]
            ms = jnp.mean(oh * oh, axis=-1, keepdims=True)
            outs.append(oh * lax.rsqrt(ms + EPS) * on_ref[...])
        y = jnp.concatenate(outs, axis=-1) * (og * _sigmoid(og))
        o_ref[0, pl.ds(r0, RB), :] = y.astype(o_ref.dtype)
        return carry

    lax.fori_loop(0, L // RB, finalize, 0)


def gla_mixer(p3, w_lr, b_lr, onorm):
    B, L, _ = p3.shape
    lr_w = 256
    wl = jnp.zeros((lr_w, 2 * GLA_K_W), F32)
    wl = wl.at[0:GLA_RANK, 0:GLA_K_W].set(w_lr[0].astype(F32))
    wl = wl.at[GLA_RANK:2 * GLA_RANK, GLA_K_W:].set(w_lr[1].astype(F32))
    bl = b_lr.astype(F32).reshape(1, 2 * GLA_K_W)
    col = lambda width, c: pl.BlockSpec((1, L, width), lambda b: (b, 0, c // width))
    return pl.pallas_call(
        _gla_kernel,
        grid=(B,),
        in_specs=[
            col(GLA_K_W, COL_GQ), col(GLA_K_W, COL_GK), col(GLA_V_W, COL_GV),
            col(GLA_V_W, COL_GO), col(lr_w, COL_LR),
            pl.BlockSpec((lr_w, 2 * GLA_K_W), lambda b: (0, 0)),
            pl.BlockSpec((1, 2 * GLA_K_W), lambda b: (0, 0)),
            pl.BlockSpec((1, GLA_DV), lambda b: (0, 0)),
        ],
        out_specs=pl.BlockSpec((1, L, GLA_V_W), lambda b: (b, 0, 0)),
        out_shape=jax.ShapeDtypeStruct((B, L, GLA_V_W), BF16),
        scratch_shapes=[
            pltpu.VMEM((2, L, GLA_K_W), BF16),
            pltpu.VMEM((2, L // (2 * GLA_CHUNK) * GLA_K_W, 2 * GLA_CHUNK), BF16),
            pltpu.VMEM((2, L // (2 * GLA_CHUNK) * GLA_K_W, 2 * GLA_CHUNK), BF16),
            pltpu.VMEM((2, L // (2 * GLA_CHUNK) * GLA_K_W, 2 * GLA_CHUNK), F32),
            pltpu.VMEM((2, L, GLA_V_W), F32),
        ],
        compiler_params=_cparams("parallel"),
        name="gla_mixer",
    )(p3, p3, p3, p3, p3, wl.astype(BF16), bl, onorm.astype(F32).reshape(1, GLA_DV))


def _merge_kernel(a_ref, b_ref, c_ref, ga_ref, gb_ref, gc_ref, h_ref, wa_ref, wb_ref, wc_ref,
                  wo_ref, o_ref):
    mixed = (_sigmoid(ga_ref[...].astype(F32)) * _dot(a_ref[...], wa_ref[...])
             + _sigmoid(gb_ref[...].astype(F32)) * _dot(b_ref[...], wb_ref[...])
             + _sigmoid(gc_ref[...].astype(F32)) * _dot(c_ref[...], wc_ref[...]))
    o_ref[...] = h_ref[...] + _dot(mixed.astype(BF16), wo_ref[...])


def merge_out(ya, yb, yc, p2, h, wa, wb, wc, wo, tm=512):
    M, D = h.shape
    tm = _tile(M, tm)
    br = lambda w: pl.BlockSpec((tm, w), lambda i: (i, 0))
    gate = lambda c: pl.BlockSpec((tm, D), lambda i: (i, COL_GATE // D + c))
    wfull = lambda r: pl.BlockSpec((r, D), lambda i: (0, 0))
    return pl.pallas_call(
        _merge_kernel,
        grid=(M // tm,),
        in_specs=[br(ATT_Q_W), br(HY_WIDTH), br(GLA_V_W), gate(0), gate(1), gate(2),
                  pl.BlockSpec((tm, D), lambda i: (i, 0)),
                  wfull(ATT_Q_W), wfull(HY_WIDTH), wfull(GLA_V_W), wfull(D)],
        out_specs=pl.BlockSpec((tm, D), lambda i: (i, 0)),
        out_shape=jax.ShapeDtypeStruct((M, D), F32),
        compiler_params=_cparams("parallel"),
        name="gated_merge_out_proj",
    )(ya, yb, yc, p2, p2, p2, h, wa, wb, wc, wo)


def _head_rmsnorm(x, g):
    ms = jnp.mean(x * x, axis=-1, keepdims=True)
    return x * lax.rsqrt(ms + EPS) * g


def _xattn_kernel(h_ref, ln_ref, wq_ref, k_ref, v_ref, qn_ref, kn_ref, wo_ref, o_ref, att_ref):
    h = h_ref[0]
    hn = _head_rmsnorm(h, ln_ref[...]).astype(BF16)
    q = _dot(hn, wq_ref[...])
    for hd in range(X_HEADS):
        sl = slice(hd * X_HEAD_DIM, (hd + 1) * X_HEAD_DIM)
        qh = (_head_rmsnorm(q[:, sl], qn_ref[...]) * (X_HEAD_DIM ** -0.5)).astype(BF16)
        kh = _head_rmsnorm(k_ref[0, :, sl].astype(F32), kn_ref[...]).astype(BF16)
        s = _dot_nt(qh, kh)
        p = jnp.exp(s - jnp.max(s, axis=-1, keepdims=True))
        l = jnp.sum(p, axis=-1, keepdims=True)
        att_ref[:, sl] = (_dot(p.astype(BF16), v_ref[0, :, sl]) / l).astype(BF16)
    o_ref[0] = h + _dot(att_ref[...], wo_ref[...])


def cross_attention(h3, ln, wq, kv3, qn, kn, wo, tl=1024):
    B, L, D = h3.shape
    Mm = kv3.shape[1]
    tl = _tile(L, tl)
    vec = lambda w: pl.BlockSpec((1, w), lambda b, i: (0, 0))
    mat = pl.BlockSpec((D, D), lambda b, i: (0, 0))
    return pl.pallas_call(
        _xattn_kernel,
        grid=(B, L // tl),
        in_specs=[
            pl.BlockSpec((1, tl, D), lambda b, i: (b, i, 0)),
            vec(D), mat,
            pl.BlockSpec((1, Mm, D), lambda b, i: (b, 0, 0)),
            pl.BlockSpec((1, Mm, D), lambda b, i: (b, 0, 1)),
            vec(X_HEAD_DIM), vec(X_HEAD_DIM), mat,
        ],
        out_specs=pl.BlockSpec((1, tl, D), lambda b, i: (b, i, 0)),
        out_shape=jax.ShapeDtypeStruct((B, L, D), F32),
        scratch_shapes=[pltpu.VMEM((tl, D), BF16)],
        compiler_params=_cparams("parallel", "parallel"),
        name="memory_cross_attention",
    )(h3, ln.astype(F32).reshape(1, D), wq, kv3, kv3, qn.astype(F32).reshape(1, X_HEAD_DIM),
      kn.astype(F32).reshape(1, X_HEAD_DIM), wo)


def _mlp_kernel(h_ref, g_ref, w1_ref, w2_ref, o_ref, hn_ref, acc_ref):
    j = pl.program_id(1)

    @pl.when(j == 0)
    def _():
        hn_ref[...] = _head_rmsnorm(h_ref[...], g_ref[...]).astype(BF16)
        acc_ref[...] = jnp.zeros_like(acc_ref)

    a = jnp.maximum(_dot(hn_ref[...], w1_ref[...]), 0.0)
    acc_ref[...] += _dot((a * a).astype(BF16), w2_ref[...])

    @pl.when(j == pl.num_programs(1) - 1)
    def _():
        o_ref[...] = h_ref[...] + acc_ref[...]


def mlp(h, g, w1, w2, tm=1024, tf=2048):
    M, D = h.shape
    F = w1.shape[1]
    tm, tf = _tile(M, tm), _tile(F, tf)
    return pl.pallas_call(
        _mlp_kernel,
        grid=(M // tm, F // tf),
        in_specs=[
            pl.BlockSpec((tm, D), lambda i, j: (i, 0)),
            pl.BlockSpec((1, D), lambda i, j: (0, 0)),
            pl.BlockSpec((D, tf), lambda i, j: (0, j)),
            pl.BlockSpec((tf, D), lambda i, j: (j, 0)),
        ],
        out_specs=pl.BlockSpec((tm, D), lambda i, j: (i, 0)),
        out_shape=jax.ShapeDtypeStruct((M, D), F32),
        scratch_shapes=[pltpu.VMEM((tm, D), BF16), pltpu.VMEM((tm, D), F32)],
        compiler_params=_cparams("parallel", "arbitrary"),
        name="relu2_mlp",
    )(h, g.astype(F32).reshape(1, D), w1, w2)


def _pack_w_in(w):
    kv0 = ATT_Q_W
    hy0 = kv0 + 2 * ATT_KV_W
    lr0 = hy0 + 3 * HY_WIDTH + 2 * GLA_K_W + 2 * GLA_V_W
    g0 = lr0 + 2 * GLA_RANK
    assert lr0 - hy0 == COL_AK - COL_HY and w.shape[2] - g0 == N_PACK - COL_GATE
    packed = jnp.zeros(w.shape[:2] + (N_PACK,), BF16)
    for src0, src1, dst in ((0, kv0, COL_AQ), (kv0, hy0, COL_AK), (hy0, lr0, COL_HY),
                            (lr0, g0, COL_LR), (g0, w.shape[2], COL_GATE)):
        packed = lax.dynamic_update_slice(packed, w[:, :, src0:src1].astype(BF16), (0, 0, dst))
    return packed


def kernel(x, mem, ln_mix, w_in, attn_qnorm, attn_knorm, hy_conv, hy_w1, hy_b1, hy_w2, hy_b2,
           hy_w3, hy_b3, hy_freq, hy_skip, gla_w_lr, gla_b_lr, gla_onorm, w_br_attn, w_br_hyena,
           w_br_gla, w_out, ln_x, ln_mem, x_wq, x_wk, x_wv, x_wo, x_qnorm, x_knorm, ln_mlp,
           mlp_w1, mlp_w2):
    B, L, D = x.shape
    Mm = mem.shape[1]
    depth = w_in.shape[0]
    M = B * L

    rope_tabs = _rope_tables(L)
    m = jnp.arange(L)
    z_neg, win_neg = _hyena_pos_features(L, (L - m).astype(F32))
    z_fwd, win_fwd = _hyena_pos_features(L, m.astype(F32))
    win_neg = jnp.where((m == 0)[:, None], 0.0, win_neg)
    zpad = lambda z: jnp.pad(z, ((0, 0), (0, HY_FFN - HY_EMB)))
    z_pair = jnp.concatenate([zpad(z_neg), zpad(z_fwd)], axis=1)
    win_pair = jnp.concatenate([win_neg, win_fwd], axis=1)
    fm, g_inv = _dft_tables(min(L, HY_BLOCK))
    bf = lambda a: a.astype(BF16)

    w_in_packed = _pack_w_in(w_in)
    wkv_all = jnp.concatenate([bf(x_wk), bf(x_wv)], axis=2)
    h = x.astype(F32).reshape(M, D)
    mem2 = mem.astype(F32).reshape(B * Mm, D)
    for i in range(depth):
        p2 = norm_matmul(h, ln_mix[i].astype(F32), w_in_packed, i, BF16)
        p3 = p2.reshape(B, L, N_PACK)
        y_a = attention(p3, attn_qnorm[i], attn_knorm[i], rope_tabs)
        f_all = hyena_filters(z_pair, win_pair, hy_w1[i].astype(F32), hy_b1[i].astype(F32),
                              hy_w2[i].astype(F32), hy_b2[i].astype(F32), hy_w3[i].astype(F32),
                              hy_b3[i].astype(F32), hy_freq[i].astype(F32))
        spec = hyena_spectra(fm, f_all)
        y_b = hyena_mixer(p3, hy_conv[i], spec, hy_skip[i], fm, g_inv)
        y_c = gla_mixer(p3, gla_w_lr[i], gla_b_lr[i], gla_onorm[i])
        h = merge_out(y_a.reshape(M, ATT_Q_W), y_b.reshape(M, HY_WIDTH), y_c.reshape(M, GLA_V_W),
                      p2, h, bf(w_br_attn[i]), bf(w_br_hyena[i]), bf(w_br_gla[i]), bf(w_out[i]))
        kv = norm_matmul(mem2, ln_mem[i].astype(F32), wkv_all, i, BF16).reshape(B, Mm, 2 * D)
        h = cross_attention(h.reshape(B, L, D), ln_x[i], bf(x_wq[i]), kv, x_qnorm[i],
                            x_knorm[i], bf(x_wo[i])).reshape(M, D)
        h = mlp(h, ln_mlp[i], bf(mlp_w1[i]), bf(mlp_w2[i]))
    return h.reshape(B, L, D).astype(x.dtype)
```

```python
import functools
import math

import jax
import jax.numpy as jnp
from jax import lax
from jax.experimental import pallas as pl
from jax.experimental.pallas import tpu as pltpu

F32 = jnp.float32
BF16 = jnp.bfloat16

D_MODEL = 1024
GRID_W = 64
ROPE_THETA = 10000.0
HEAD_DIM = 64
ATT_Q_HEADS = 8
ATT_KV_HEADS = 2
ATT_Q_W = ATT_Q_HEADS * HEAD_DIM
ATT_KV_W = ATT_KV_HEADS * HEAD_DIM
HY_WIDTH = 512
HY_ORDER = 2
HY_BANDS = 16
HY_EMB = 1 + 2 * HY_BANDS
HY_FFN = 64
HY_FAST_DECAY = 0.3
HY_SLOW_DECAY = 1.5
HY_TARGET = 1e-2
GLA_HEADS = 4
GLA_DK = 64
GLA_DV = 128
GLA_RANK = 16
GLA_NORMALIZER = 16.0
GLA_CHUNK = 64
GLA_K_W = GLA_HEADS * GLA_DK
GLA_V_W = GLA_HEADS * GLA_DV
X_HEADS = 4
X_HEAD_DIM = D_MODEL // X_HEADS
D_FF = 4 * D_MODEL
N_BRANCH = 3
EPS = 1e-6

COL_AQ = 0
COL_HY = 512
COL_GQ = 2048
COL_GK = 2304
COL_GV = 2560
COL_GO = 3072
COL_AK = 3584
COL_AV = 3712
COL_LR = 3840
COL_GATE = 4096
N_PACK = 7168

VMEM_LIMIT_BYTES = 52 * 1024 * 1024


def _cparams(*sem):
    return pltpu.CompilerParams(dimension_semantics=sem, vmem_limit_bytes=VMEM_LIMIT_BYTES)


def _tile(n, t):
    t = min(n, t)
    assert n % t == 0, (n, t)
    return t


def _dot(a, b):
    return jnp.dot(a, b, preferred_element_type=F32)


def _dot_nt(a, b):
    return lax.dot_general(a, b, (((1,), (1,)), ((), ())), preferred_element_type=F32)


def _dot_tn(a, b):
    return lax.dot_general(a, b, (((0,), (0,)), ((), ())), preferred_element_type=F32)


def _sigmoid(x):
    return 1.0 / (1.0 + jnp.exp(-x))


def _norm_mm_kernel(x_ref, g_ref, w_ref, o_ref, xn_ref):
    @pl.when(pl.program_id(1) == 0)
    def _():
        x = x_ref[...]
        ms = jnp.mean(x * x, axis=-1, keepdims=True)
        xn_ref[...] = (x * lax.rsqrt(ms + EPS) * g_ref[...]).astype(BF16)

    o_ref[...] = _dot(xn_ref[...], w_ref[...]).astype(o_ref.dtype)


def norm_matmul(x, g, w, layer, out_dtype, tm=1024, tn=3584):
    M, K = x.shape
    N = w.shape[2]
    tm, tn = _tile(M, tm), _tile(N, tn)
    return pl.pallas_call(
        _norm_mm_kernel,
        grid=(M // tm, N // tn),
        in_specs=[
            pl.BlockSpec((tm, K), lambda i, j: (i, 0)),
            pl.BlockSpec((1, K), lambda i, j: (0, 0)),
            pl.BlockSpec((None, K, tn), lambda i, j: (layer, 0, j)),
        ],
        out_specs=pl.BlockSpec((tm, tn), lambda i, j: (i, j)),
        out_shape=jax.ShapeDtypeStruct((M, N), out_dtype),
        scratch_shapes=[pltpu.VMEM((tm, K), BF16)],
        compiler_params=_cparams("parallel", "arbitrary"),
        name="norm_matmul",
    )(x, g.reshape(1, K), w)


def _group_mean_sq(x, gm_ref):
    return _dot((x * x).astype(BF16), gm_ref[...])


def _rope(x, c_ref, sa_ref, sb_ref):
    w = x.shape[-1]
    return (x * c_ref[...] + pltpu.roll(x, w - HEAD_DIM // 4, 1) * sa_ref[...]
            + pltpu.roll(x, HEAD_DIM // 4, 1) * sb_ref[...])


def _attn_kernel(q_ref, k_ref, v_ref, qn_ref, kn_ref, cq_ref, saq_ref, sbq_ref,
                 ck_ref, sak_ref, sbk_ref, gmq_ref, gmk_ref, o_ref, kp_ref, vx_ref):
    @pl.when(pl.program_id(1) == 0)
    def _():
        k = k_ref[0].astype(F32)
        kh = k * lax.rsqrt(_group_mean_sq(k, gmk_ref) + EPS) * kn_ref[...]
        kp_ref[...] = _rope(kh, ck_ref, sak_ref, sbk_ref).astype(BF16)
        v = v_ref[0]
        ones = jnp.ones((v.shape[0], HEAD_DIM), BF16)
        vx_ref[...] = jnp.concatenate(
            [piece for kv in range(ATT_KV_HEADS)
             for piece in (v[:, kv * HEAD_DIM:(kv + 1) * HEAD_DIM], ones)], axis=-1)

    q = q_ref[0].astype(F32)
    qh = q * lax.rsqrt(_group_mean_sq(q, gmq_ref) + EPS) * qn_ref[...]
    qb = (_rope(qh, cq_ref, saq_ref, sbq_ref) * (HEAD_DIM ** -0.5 * math.log2(math.e))).astype(BF16)
    group = ATT_Q_HEADS // ATT_KV_HEADS
    outs = []
    for h in range(ATT_Q_HEADS):
        kv = h // group
        kk = kp_ref[:, kv * HEAD_DIM:(kv + 1) * HEAD_DIM]
        s = _dot_nt(qb[:, h * HEAD_DIM:(h + 1) * HEAD_DIM], kk)
        p = jnp.exp2(s - jnp.max(s, axis=-1, keepdims=True))
        ox = _dot(p.astype(BF16), vx_ref[:, kv * 2 * HEAD_DIM:(kv + 1) * 2 * HEAD_DIM])
        outs.append(ox[:, :HEAD_DIM] / ox[:, HEAD_DIM:])
    o_ref[0] = jnp.concatenate(outs, axis=-1).astype(o_ref.dtype)


def _rope_tables(L):
    rows = L // GRID_W
    r, c = jnp.meshgrid(jnp.arange(rows), jnp.arange(GRID_W), indexing="ij")
    n_freq = HEAD_DIM // 4
    inv = ROPE_THETA ** (-jnp.arange(n_freq, dtype=F32) / n_freq)
    pos = jnp.stack([r.reshape(-1), c.reshape(-1)], axis=1).astype(F32)
    ang = pos[:, :, None] * inv
    cos, sin = jnp.cos(ang), jnp.sin(ang)
    zero = jnp.zeros_like(sin)
    c64 = jnp.concatenate([cos, cos], axis=-1).reshape(L, HEAD_DIM)
    sa64 = jnp.concatenate([-sin, zero], axis=-1).reshape(L, HEAD_DIM)
    sb64 = jnp.concatenate([zero, sin], axis=-1).reshape(L, HEAD_DIM)
    return c64, sa64, sb64


def attention(p3, qn, kn, tabs, tq=1024):
    B, L, _ = p3.shape
    tq = _tile(L, tq)
    c64, sa64, sb64 = tabs
    tq_tabs = [jnp.tile(t, (1, ATT_Q_HEADS)) for t in (c64, sa64, sb64)]
    tk_tabs = [jnp.tile(t, (1, ATT_KV_HEADS)) for t in (c64, sa64, sb64)]
    qn_t = jnp.tile(qn.astype(F32), ATT_Q_HEADS).reshape(1, ATT_Q_W)
    kn_t = jnp.tile(kn.astype(F32), ATT_KV_HEADS).reshape(1, ATT_KV_W)

    def group_mean(width):
        g = jnp.arange(width) // HEAD_DIM
        return jnp.where(g[:, None] == g[None, :], 1.0 / HEAD_DIM, 0.0).astype(BF16)

    qtab = pl.BlockSpec((tq, ATT_Q_W), lambda b, i: (i, 0))
    ktab = pl.BlockSpec((L, ATT_KV_W), lambda b, i: (0, 0))
    return pl.pallas_call(
        _attn_kernel,
        grid=(B, L // tq),
        in_specs=[
            pl.BlockSpec((1, tq, ATT_Q_W), lambda b, i: (b, i, COL_AQ // ATT_Q_W)),
            pl.BlockSpec((1, L, ATT_KV_W), lambda b, i: (b, 0, COL_AK // ATT_KV_W)),
            pl.BlockSpec((1, L, ATT_KV_W), lambda b, i: (b, 0, COL_AV // ATT_KV_W)),
            pl.BlockSpec((1, ATT_Q_W), lambda b, i: (0, 0)),
            pl.BlockSpec((1, ATT_KV_W), lambda b, i: (0, 0)),
            qtab, qtab, qtab, ktab, ktab, ktab,
            pl.BlockSpec((ATT_Q_W, ATT_Q_W), lambda b, i: (0, 0)),
            pl.BlockSpec((ATT_KV_W, ATT_KV_W), lambda b, i: (0, 0)),
        ],
        out_specs=pl.BlockSpec((1, tq, ATT_Q_W), lambda b, i: (b, i, 0)),
        out_shape=jax.ShapeDtypeStruct((B, L, ATT_Q_W), BF16),
        scratch_shapes=[pltpu.VMEM((L, ATT_KV_W), BF16), pltpu.VMEM((L, 2 * ATT_KV_W), BF16)],
        compiler_params=_cparams("parallel", "arbitrary"),
        name="gqa_attention",
    )(p3, p3, p3, qn_t, kn_t, *tq_tabs, *tk_tabs, group_mean(ATT_Q_W), group_mean(ATT_KV_W))


def _dft_tables(L):
    n = 2 * L
    k = jnp.arange(L, dtype=jnp.int32)[:, None]
    j = jnp.arange(L, dtype=jnp.int32)[None, :]
    step = 64
    ja = jnp.arange(0, L, step, dtype=jnp.int32)[None, :]
    jb = jnp.arange(step, dtype=jnp.int32)[None, :]
    ang_a = ((k * ja) % n).astype(F32) * (2.0 * math.pi / n)
    ang_b = ((k * jb) % n).astype(F32) * (2.0 * math.pi / n)
    ca, sa = jnp.cos(ang_a)[:, :, None], jnp.sin(ang_a)[:, :, None]
    cb, sb = jnp.cos(ang_b)[:, None, :], jnp.sin(ang_b)[:, None, :]
    c = (ca * cb - sa * sb).reshape(L, L)
    s = -(sa * cb + ca * sb).reshape(L, L)
    bottom = jnp.where(k == 0, jnp.where(j % 2 == 0, 1.0, -1.0), s)
    bottom_t = jnp.where(j == 0, jnp.where(k % 2 == 0, 1.0, -1.0), s)
    fm = jnp.stack([c, bottom]).astype(BF16)
    g = jnp.concatenate([c, bottom_t], axis=1).astype(BF16)
    return fm, g


def _hyena_pos_features(L, t):
    t_norm = t / max(L - 1, 1)
    w = 2.0 * math.pi * t / L
    f = jnp.linspace(1e-4, HY_BANDS - 1, HY_BANDS, dtype=F32)
    fw = w[:, None] * f
    z = jnp.concatenate([t_norm[:, None], jnp.cos(fw), -jnp.sin(fw)], axis=-1)
    deltas = jnp.abs(jnp.linspace(math.log(HY_TARGET) / HY_FAST_DECAY,
                                  math.log(HY_TARGET) / HY_SLOW_DECAY, HY_WIDTH, dtype=F32))
    window = jnp.exp(-t_norm[:, None] * deltas)
    return z, window


def _sconv_kernel(u_ref, w_ref, o_ref):
    u = u_ref[0].astype(F32)
    L = u.shape[0]
    row = lax.broadcasted_iota(jnp.int32, u.shape, 0)
    prev = jnp.where(row == 0, 0.0, pltpu.roll(u, 1, 0))
    nxt = jnp.where(row == L - 1, 0.0, pltpu.roll(u, L - 1, 0))
    o_ref[0] = (prev * w_ref[0:1, :] + u * w_ref[1:2, :] + nxt * w_ref[2:3, :]).astype(o_ref.dtype)


def short_conv(p3, w):
    B, L, _ = p3.shape
    nblk = 3
    return pl.pallas_call(
        _sconv_kernel,
        grid=(B, nblk),
        in_specs=[
            pl.BlockSpec((1, L, HY_WIDTH), lambda b, c: (b, 0, COL_HY // HY_WIDTH + c)),
            pl.BlockSpec((3, HY_WIDTH), lambda b, c: (0, c)),
        ],
        out_specs=pl.BlockSpec((1, L, HY_WIDTH), lambda b, c: (b, 0, c)),
        out_shape=jax.ShapeDtypeStruct((B, L, nblk * HY_WIDTH), BF16),
        compiler_params=_cparams("parallel", "parallel"),
        name="hyena_short_conv",
    )(p3, w)


def _hyfilter_kernel(z_ref, w1_ref, b1_ref, w2_ref, b2_ref, w3_ref, b3_ref, fr_ref, win_ref,
                     f_ref):
    L = z_ref.shape[0]
    hp = lax.Precision.HIGHEST
    h = jnp.sin(fr_ref[0:1, :] * (jnp.dot(z_ref[...], w1_ref[...], precision=hp,
                                          preferred_element_type=F32) + b1_ref[...]))
    h = jnp.sin(fr_ref[1:2, :] * (jnp.dot(h, w2_ref[...], precision=hp,
                                          preferred_element_type=F32) + b2_ref[...]))
    f = (_dot(h.astype(BF16), w3_ref[0].astype(BF16)) + b3_ref[0]) * win_ref[...]
    col = jnp.sum(jnp.abs(f), axis=0, keepdims=True)
    norm = col[:, :HY_WIDTH] + col[:, HY_WIDTH:] + EPS
    f_ref[:L, :] = (f[:, :HY_WIDTH] / norm).astype(f_ref.dtype)
    f_ref[L:, :] = (f[:, HY_WIDTH:] / norm).astype(f_ref.dtype)


def _block_diag2(w):
    z = jnp.zeros_like(w)
    return jnp.concatenate([jnp.concatenate([w, z], axis=1), jnp.concatenate([z, w], axis=1)], axis=0)


def hyena_filters(z_pair, win_pair, w1, b1, w2, b2, w3, b3, freq):
    L = z_pair.shape[0]
    pad_w1 = jnp.pad(w1, ((0, HY_FFN - HY_EMB), (0, 0)))
    two = lambda v: jnp.concatenate([v, v], axis=-1)
    w3r = w3.reshape(HY_FFN, HY_ORDER, 2, HY_WIDTH)
    zero = jnp.zeros((HY_ORDER, HY_FFN, HY_WIDTH), F32)
    w3n = jnp.moveaxis(w3r[:, :, 1], 1, 0)
    w3p = jnp.moveaxis(w3r[:, :, 0], 1, 0)
    w3bd = jnp.concatenate([jnp.concatenate([w3n, zero], axis=2),
                            jnp.concatenate([zero, w3p], axis=2)], axis=1)
    b3r = b3.reshape(HY_ORDER, 1, 2, HY_WIDTH)
    b3c = jnp.concatenate([b3r[:, :, 1], b3r[:, :, 0]], axis=-1)
    full = lambda shape: pl.BlockSpec(shape, lambda o: (0,) * len(shape))
    H2 = 2 * HY_FFN
    return pl.pallas_call(
        _hyfilter_kernel,
        grid=(HY_ORDER,),
        in_specs=[
            full((L, H2)), full((H2, H2)), full((1, H2)), full((H2, H2)), full((1, H2)),
            pl.BlockSpec((1, H2, 2 * HY_WIDTH), lambda o: (o, 0, 0)),
            pl.BlockSpec((1, 1, 2 * HY_WIDTH), lambda o: (o, 0, 0)),
            full((2, H2)), full((L, 2 * HY_WIDTH)),
        ],
        out_specs=pl.BlockSpec((2 * L, HY_WIDTH), lambda o: (0, o)),
        out_shape=jax.ShapeDtypeStruct((2 * L, HY_ORDER * HY_WIDTH), BF16),
        compiler_params=_cparams("parallel"),
        name="hyena_filter_mlp",
    )(z_pair, _block_diag2(pad_w1), two(b1.reshape(1, -1)), _block_diag2(w2),
      two(b2.reshape(1, -1)), w3bd, b3c, two(freq), win_pair)


def _spec_kernel(fm_ref, blk_ref, h_ref, prev_ref, *, n):
    P = fm_ref.shape[1]
    row = lax.broadcasted_iota(jnp.int32, (P, blk_ref.shape[1]), 0)
    sign = jnp.where(row % 2 == 0, 1.0, -1.0)
    scale = jnp.where(row == 0, 1.0 / n, 2.0 / n)
    @pl.when(pl.program_id(0) == 0)
    def _():
        prev_ref[...] = jnp.zeros_like(prev_ref)

    for half in range(2):
        phi = _dot(fm_ref[half], blk_ref[...])
        h_ref[0, half] = (phi + sign * prev_ref[half]) * scale
        prev_ref[half] = phi


def hyena_spectra(fm, f_all):
    P = fm.shape[1]
    W = f_all.shape[1]
    n_blk = f_all.shape[0] // P
    return pl.pallas_call(
        functools.partial(_spec_kernel, n=2 * P),
        grid=(n_blk,),
        in_specs=[
            pl.BlockSpec((2, P, P), lambda e: (0, 0, 0)),
            pl.BlockSpec((P, W), lambda e: (e, 0)),
        ],
        out_specs=pl.BlockSpec((1, 2, P, W), lambda e: (jnp.maximum(e - 1, 0), 0, 0, 0)),
        out_shape=jax.ShapeDtypeStruct((n_blk - 1, 2, P, W), F32),
        scratch_shapes=[pltpu.VMEM((2, P, W), F32)],
        compiler_params=_cparams("arbitrary"),
        name="hyena_filter_spectrum",
    )(fm, f_all)


HY_BLOCK = 512
HY_CONV_ROWS = 16


def _lconv_kernel(fm_ref, g_ref, u_ref, gate_ref, skip_ref, h_ref, o_ref, x_ref, yt_ref, yb_ref):
    P = fm_ref.shape[1]
    nbat = u_ref.shape[0]
    nb = u_ref.shape[1] // P
    ct = u_ref.shape[2]
    for s in range(nbat):
        for j in range(nb):
            uj = u_ref[s, j * P:(j + 1) * P, :]
            x_ref[s, j, 0] = _dot(fm_ref[0], uj)
            x_ref[s, j, 1] = _dot(fm_ref[1], uj)
    first = lax.broadcasted_iota(jnp.int32, (HY_CONV_ROWS, ct), 0) == 0
    for i in range(nb):
        for r in range(P // HY_CONV_ROWS):
            rows = slice(r * HY_CONV_ROWS, (r + 1) * HY_CONV_ROWS)
            at = [jnp.zeros((HY_CONV_ROWS, ct), F32) for _ in range(nbat)]
            ab = [jnp.zeros((HY_CONV_ROWS, ct), F32) for _ in range(nbat)]
            for j in range(nb):
                d = i - j + nb - 1
                ht, hb = h_ref[d, 0, rows, :], h_ref[d, 1, rows, :]
                for s in range(nbat):
                    xt, xb = x_ref[s, j, 0, rows, :], x_ref[s, j, 1, rows, :]
                    bb = xb * hb
                    if r == 0:
                        at[s] += xt * ht - jnp.where(first, 0.0, bb)
                        ab[s] += jnp.where(first, bb, xt * hb + xb * ht)
                    else:
                        at[s] += xt * ht - bb
                        ab[s] += xt * hb + xb * ht
            for s in range(nbat):
                yt_ref[s, i, rows, :] = at[s].astype(BF16)
                yb_ref[s, i, rows, :] = ab[s].astype(BF16)
        blk = slice(i * P, (i + 1) * P)
        for s in range(nbat):
            y = _dot(g_ref[:, :P], yt_ref[s, i]) + _dot(g_ref[:, P:], yb_ref[s, i])
            ui = u_ref[s, blk, :].astype(F32)
            o_ref[s, blk, :] = ((y + ui * skip_ref[...]) * gate_ref[s, blk, :].astype(F32)
                                ).astype(o_ref.dtype)


HY_CONV_BATCH = 1


def long_conv(fm, g, spec, order, u, u_col, gate, gate_col, skip, ct=256):
    B, L, _ = u.shape
    P = fm.shape[1]
    nb = L // P
    n_lag = spec.shape[0]
    per = HY_WIDTH // ct
    nbat = _tile(B, HY_CONV_BATCH)
    return pl.pallas_call(
        _lconv_kernel,
        grid=(per, B // nbat),
        in_specs=[
            pl.BlockSpec((2, P, P), lambda c, b: (0, 0, 0)),
            pl.BlockSpec((P, 2 * P), lambda c, b: (0, 0)),
            pl.BlockSpec((nbat, L, ct), lambda c, b: (b, 0, u_col * per + c)),
            pl.BlockSpec((nbat, L, ct), lambda c, b: (b, 0, gate_col * per + c)),
            pl.BlockSpec((1, ct), lambda c, b: (0, c)),
            pl.BlockSpec((n_lag, 2, P, ct), lambda c, b: (0, 0, 0, order * per + c)),
        ],
        out_specs=pl.BlockSpec((nbat, L, ct), lambda c, b: (b, 0, c)),
        out_shape=jax.ShapeDtypeStruct((B, L, HY_WIDTH), BF16),
        scratch_shapes=[pltpu.VMEM((nbat, nb, 2, P, ct), F32), pltpu.VMEM((nbat, nb, P, ct), BF16),
                        pltpu.VMEM((nbat, nb, P, ct), BF16)],
        compiler_params=_cparams("parallel", "parallel"),
        name="hyena_long_conv",
    )(fm, g, u, gate, skip.reshape(1, HY_WIDTH).astype(F32), spec)


def hyena_mixer(p3, conv_w, spec, skip, fm, g):
    uc = short_conv(p3, conv_w.astype(F32))
    z = long_conv(fm, g, spec, 0, uc, 0, uc, 1, skip[0])
    return long_conv(fm, g, spec, 1, z, 0, uc, 2, skip[1])


def _log_sigmoid(x):
    return -(jnp.maximum(-x, 0.0) + jnp.log(1.0 + jnp.exp(-jnp.abs(x))))


def _dot_hilo(m, x):
    hi = x.astype(BF16)
    lo = (x - hi.astype(F32)).astype(BF16)
    return _dot(m, hi) + _dot(m, lo)


GLA_PREP_ROWS = 256


def _gla_kernel(q_ref, k_ref, v_ref, og_ref, lr_ref, wlr_ref, blr_ref, on_ref, o_ref,
                qd_ref, kit_ref, kst_ref, dcol_ref, acc_ref):
    L = q_ref.shape[1]
    C = GLA_CHUNK
    H = GLA_HEADS
    RB = GLA_PREP_ROWS
    n_chunks = L // C
    KW, VW, DV = GLA_K_W, GLA_V_W, GLA_DV

    pr = lax.broadcasted_iota(jnp.int32, (RB, RB), 0)
    pc = lax.broadcasted_iota(jnp.int32, (RB, RB), 1)
    same = (pr // C) == (pc // C)
    cum_f = jnp.where(same & (pc <= pr), 1.0, 0.0).astype(BF16)
    cum_b = jnp.where(same & (pc >= pr), 1.0, 0.0).astype(BF16)
    tot_m = jnp.where(same, 1.0, 0.0).astype(BF16)
    PW = 2 * C
    n_pairs = L // PW

    def prep(i, carry):
        r0 = pl.multiple_of(i * RB, RB)
        logit = _dot(lr_ref[0, pl.ds(r0, RB), :], wlr_ref[...]) + blr_ref[...]
        g = _log_sigmoid(logit) * (1.0 / GLA_NORMALIZER)
        q = q_ref[0, pl.ds(r0, RB), :].astype(F32) * (GLA_DK ** -0.5)
        k = k_ref[0, pl.ds(r0, RB), :].astype(F32)
        for d, cum in ((0, cum_f), (1, cum_b)):
            gd = g[:, d * KW:(d + 1) * KW]
            b = _dot_hilo(cum, gd)
            tot = _dot_hilo(tot_m, gd)
            qd_ref[d, pl.ds(r0, RB), :] = (q * jnp.exp(b)).astype(BF16)
            ki_t = (k * jnp.exp(-b)).T
            ks_t = (k * jnp.exp(tot - b)).T
            dec_t = jnp.exp(tot).T
            for p in range(RB // PW):
                rows = pl.ds(pl.multiple_of((i * (RB // PW) + p) * KW, KW), KW)
                cols = slice(p * PW, (p + 1) * PW)
                kit_ref[d, rows, :] = ki_t[:, cols].astype(BF16)
                kst_ref[d, rows, :] = ks_t[:, cols].astype(BF16)
                dcol_ref[d, rows, :] = dec_t[:, cols]
        return carry

    lax.fori_loop(0, L // RB, prep, 0)

    kk_blk = (lax.broadcasted_iota(jnp.int32, (KW, H * C), 0) // GLA_DK
              == lax.broadcasted_iota(jnp.int32, (KW, H * C), 1) // C)
    kv_blk = (lax.broadcasted_iota(jnp.int32, (KW, VW), 0) // GLA_DK
              == lax.broadcasted_iota(jnp.int32, (KW, VW), 1) // DV)
    arow = lax.broadcasted_iota(jnp.int32, (C, H * C), 0)
    acol = lax.broadcasted_iota(jnp.int32, (C, H * C), 1) % C
    amasks = (acol <= arow, acol > arow)
    zero_b = jnp.zeros((), BF16)

    def body(ip, sts):
        new = []
        for d in (0, 1):
            st = sts[d]
            pair = ip if d == 0 else n_pairs - 1 - ip
            rk = pl.multiple_of(pair * KW, KW)
            kit2 = kit_ref[d, pl.ds(rk, KW), :]
            kst2 = kst_ref[d, pl.ds(rk, KW), :]
            dec2 = dcol_ref[d, pl.ds(rk, KW), :]
            for half in ((0, 1) if d == 0 else (1, 0)):
                r0 = pl.multiple_of(pair * PW + half * C, C)
                cols = slice(half * C, (half + 1) * C)
                qd = qd_ref[d, pl.ds(r0, C), :]
                v = v_ref[0, pl.ds(r0, C), :]
                kk = jnp.where(kk_blk, jnp.concatenate([kit2[:, cols]] * H, axis=1), zero_b)
                a = jnp.where(amasks[d], _dot(qd, kk), 0.0).astype(BF16)
                v_bd = jnp.where(kv_blk, jnp.concatenate([v] * H, axis=0), zero_b)
                st_b = st.astype(BF16)
                s_bd = jnp.where(kv_blk, jnp.concatenate([st_b] * H, axis=1), zero_b)
                acc_ref[d, pl.ds(r0, C), :] = _dot(a, v_bd) + _dot(qd, s_bd)
                ds = jnp.concatenate(
                    [_dot(kst2[h * GLA_DK:(h + 1) * GLA_DK, cols], v[:, h * DV:(h + 1) * DV])
                     for h in range(H)], axis=0)
                st = st * jnp.concatenate([dec2[:, cols]] * (DV // C), axis=1) + ds
            new.append(st)
        return tuple(new)

    zero = jnp.zeros((KW, DV), F32)
    lax.fori_loop(0, n_pairs, body, (zero, zero), unroll=4)

    def finalize(i, carry):
        r0 = pl.multiple_of(i * RB, RB)
        o = acc_ref[0, pl.ds(r0, RB), :] + acc_ref[1, pl.ds(r0, RB), :]
        og = og_ref[0, pl.ds(r0, RB), :].astype(F32)
        outs = []
        for h in range(H):
            oh = o[:, h * GLA_DV:(h + 1) * GLA_DV]
            ms = jnp.mean(oh * oh, axis=-1, keepdims=True)
            outs.append(oh * lax.rsqrt(ms + EPS) * on_ref[...])
        y = jnp.concatenate(outs, axis=-1) * (og * _sigmoid(og))
        o_ref[0, pl.ds(r0, RB), :] = y.astype(o_ref.dtype)
        return carry

    lax.fori_loop(0, L // RB, finalize, 0, unroll=2)


def gla_mixer(p3, w_lr, b_lr, onorm):
    B, L, _ = p3.shape
    lr_w = 256
    wl = jnp.zeros((lr_w, 2 * GLA_K_W), F32)
    wl = wl.at[0:GLA_RANK, 0:GLA_K_W].set(w_lr[0].astype(F32))
    wl = wl.at[GLA_RANK:2 * GLA_RANK, GLA_K_W:].set(w_lr[1].astype(F32))
    bl = b_lr.astype(F32).reshape(1, 2 * GLA_K_W)
    col = lambda width, c: pl.BlockSpec((1, L, width), lambda b: (b, 0, c // width))
    return pl.pallas_call(
        _gla_kernel,
        grid=(B,),
        in_specs=[
            col(GLA_K_W, COL_GQ), col(GLA_K_W, COL_GK), col(GLA_V_W, COL_GV),
            col(GLA_V_W, COL_GO), col(lr_w, COL_LR),
            pl.BlockSpec((lr_w, 2 * GLA_K_W), lambda b: (0, 0)),
            pl.BlockSpec((1, 2 * GLA_K_W), lambda b: (0, 0)),
            pl.BlockSpec((1, GLA_DV), lambda b: (0, 0)),
        ],
        out_specs=pl.BlockSpec((1, L, GLA_V_W), lambda b: (b, 0, 0)),
        out_shape=jax.ShapeDtypeStruct((B, L, GLA_V_W), BF16),
        scratch_shapes=[
            pltpu.VMEM((2, L, GLA_K_W), BF16),
            pltpu.VMEM((2, L // (2 * GLA_CHUNK) * GLA_K_W, 2 * GLA_CHUNK), BF16),
            pltpu.VMEM((2, L // (2 * GLA_CHUNK) * GLA_K_W, 2 * GLA_CHUNK), BF16),
            pltpu.VMEM((2, L // (2 * GLA_CHUNK) * GLA_K_W, 2 * GLA_CHUNK), F32),
            pltpu.VMEM((2, L, GLA_V_W), F32),
        ],
        compiler_params=_cparams("parallel"),
        name="gla_mixer",
    )(p3, p3, p3, p3, p3, wl.astype(BF16), bl, onorm.astype(F32).reshape(1, GLA_DV))


def _merge_kernel(a_ref, b_ref, c_ref, ga_ref, gb_ref, gc_ref, h_ref, wa_ref, wb_ref, wc_ref,
                  wo_ref, o_ref):
    mixed = (_sigmoid(ga_ref[...].astype(F32)) * _dot(a_ref[...], wa_ref[...])
             + _sigmoid(gb_ref[...].astype(F32)) * _dot(b_ref[...], wb_ref[...])
             + _sigmoid(gc_ref[...].astype(F32)) * _dot(c_ref[...], wc_ref[...]))
    o_ref[...] = h_ref[...] + _dot(mixed.astype(BF16), wo_ref[...])


def merge_out(ya, yb, yc, p2, h, wa, wb, wc, wo, tm=512):
    M, D = h.shape
    tm = _tile(M, tm)
    br = lambda w: pl.BlockSpec((tm, w), lambda i: (i, 0))
    gate = lambda c: pl.BlockSpec((tm, D), lambda i: (i, COL_GATE // D + c))
    wfull = lambda r: pl.BlockSpec((r, D), lambda i: (0, 0))
    return pl.pallas_call(
        _merge_kernel,
        grid=(M // tm,),
        in_specs=[br(ATT_Q_W), br(HY_WIDTH), br(GLA_V_W), gate(0), gate(1), gate(2),
                  pl.BlockSpec((tm, D), lambda i: (i, 0)),
                  wfull(ATT_Q_W), wfull(HY_WIDTH), wfull(GLA_V_W), wfull(D)],
        out_specs=pl.BlockSpec((tm, D), lambda i: (i, 0)),
        out_shape=jax.ShapeDtypeStruct((M, D), F32),
        compiler_params=_cparams("parallel"),
        name="gated_merge_out_proj",
    )(ya, yb, yc, p2, p2, p2, h, wa, wb, wc, wo)


def _head_rmsnorm(x, g):
    ms = jnp.mean(x * x, axis=-1, keepdims=True)
    return x * lax.rsqrt(ms + EPS) * g


def _xattn_kernel(h_ref, ln_ref, wq_ref, k_ref, v_ref, qn_ref, kn_ref, wo_ref, o_ref, att_ref):
    h = h_ref[0]
    hn = _head_rmsnorm(h, ln_ref[...]).astype(BF16)
    q = _dot(hn, wq_ref[...])
    for hd in range(X_HEADS):
        sl = slice(hd * X_HEAD_DIM, (hd + 1) * X_HEAD_DIM)
        qh = (_head_rmsnorm(q[:, sl], qn_ref[...]) * (X_HEAD_DIM ** -0.5)).astype(BF16)
        kh = _head_rmsnorm(k_ref[0, :, sl].astype(F32), kn_ref[...]).astype(BF16)
        s = _dot_nt(qh, kh)
        p = jnp.exp(s - jnp.max(s, axis=-1, keepdims=True))
        l = jnp.sum(p, axis=-1, keepdims=True)
        att_ref[:, sl] = (_dot(p.astype(BF16), v_ref[0, :, sl]) / l).astype(BF16)
    o_ref[0] = h + _dot(att_ref[...], wo_ref[...])


def cross_attention(h3, ln, wq, kv3, qn, kn, wo, tl=1024):
    B, L, D = h3.shape
    Mm = kv3.shape[1]
    tl = _tile(L, tl)
    vec = lambda w: pl.BlockSpec((1, w), lambda b, i: (0, 0))
    mat = pl.BlockSpec((D, D), lambda b, i: (0, 0))
    return pl.pallas_call(
        _xattn_kernel,
        grid=(B, L // tl),
        in_specs=[
            pl.BlockSpec((1, tl, D), lambda b, i: (b, i, 0)),
            vec(D), mat,
            pl.BlockSpec((1, Mm, D), lambda b, i: (b, 0, 0)),
            pl.BlockSpec((1, Mm, D), lambda b, i: (b, 0, 1)),
            vec(X_HEAD_DIM), vec(X_HEAD_DIM), mat,
        ],
        out_specs=pl.BlockSpec((1, tl, D), lambda b, i: (b, i, 0)),
        out_shape=jax.ShapeDtypeStruct((B, L, D), F32),
        scratch_shapes=[pltpu.VMEM((tl, D), BF16)],
        compiler_params=_cparams("parallel", "parallel"),
        name="memory_cross_attention",
    )(h3, ln.astype(F32).reshape(1, D), wq, kv3, kv3, qn.astype(F32).reshape(1, X_HEAD_DIM),
      kn.astype(F32).reshape(1, X_HEAD_DIM), wo)


def _mlp_kernel(h_ref, g_ref, w1_ref, w2_ref, o_ref, hn_ref, acc_ref):
    j = pl.program_id(1)

    @pl.when(j == 0)
    def _():
        hn_ref[...] = _head_rmsnorm(h_ref[...], g_ref[...]).astype(BF16)
        acc_ref[...] = jnp.zeros_like(acc_ref)

    a = jnp.maximum(_dot(hn_ref[...], w1_ref[...]), 0.0)
    acc_ref[...] += _dot((a * a).astype(BF16), w2_ref[...])

    @pl.when(j == pl.num_programs(1) - 1)
    def _():
        o_ref[...] = h_ref[...] + acc_ref[...]


def mlp(h, g, w1, w2, tm=1024, tf=2048):
    M, D = h.shape
    F = w1.shape[1]
    tm, tf = _tile(M, tm), _tile(F, tf)
    return pl.pallas_call(
        _mlp_kernel,
        grid=(M // tm, F // tf),
        in_specs=[
            pl.BlockSpec((tm, D), lambda i, j: (i, 0)),
            pl.BlockSpec((1, D), lambda i, j: (0, 0)),
            pl.BlockSpec((D, tf), lambda i, j: (0, j)),
            pl.BlockSpec((tf, D), lambda i, j: (j, 0)),
        ],
        out_specs=pl.BlockSpec((tm, D), lambda i, j: (i, 0)),
        out_shape=jax.ShapeDtypeStruct((M, D), F32),
        scratch_shapes=[pltpu.VMEM((tm, D), BF16), pltpu.VMEM((tm, D), F32)],
        compiler_params=_cparams("parallel", "arbitrary"),
        name="relu2_mlp",
    )(h, g.astype(F32).reshape(1, D), w1, w2)


def _pack_w_in(w):
    kv0 = ATT_Q_W
    hy0 = kv0 + 2 * ATT_KV_W
    lr0 = hy0 + 3 * HY_WIDTH + 2 * GLA_K_W + 2 * GLA_V_W
    g0 = lr0 + 2 * GLA_RANK
    assert lr0 - hy0 == COL_AK - COL_HY and w.shape[2] - g0 == N_PACK - COL_GATE
    packed = jnp.zeros(w.shape[:2] + (N_PACK,), BF16)
    for src0, src1, dst in ((0, kv0, COL_AQ), (kv0, hy0, COL_AK), (hy0, lr0, COL_HY),
                            (lr0, g0, COL_LR), (g0, w.shape[2], COL_GATE)):
        packed = lax.dynamic_update_slice(packed, w[:, :, src0:src1].astype(BF16), (0, 0, dst))
    return packed


def kernel(x, mem, ln_mix, w_in, attn_qnorm, attn_knorm, hy_conv, hy_w1, hy_b1, hy_w2, hy_b2,
           hy_w3, hy_b3, hy_freq, hy_skip, gla_w_lr, gla_b_lr, gla_onorm, w_br_attn, w_br_hyena,
           w_br_gla, w_out, ln_x, ln_mem, x_wq, x_wk, x_wv, x_wo, x_qnorm, x_knorm, ln_mlp,
           mlp_w1, mlp_w2):
    B, L, D = x.shape
    Mm = mem.shape[1]
    depth = w_in.shape[0]
    M = B * L

    rope_tabs = _rope_tables(L)
    m = jnp.arange(L)
    z_neg, win_neg = _hyena_pos_features(L, (L - m).astype(F32))
    z_fwd, win_fwd = _hyena_pos_features(L, m.astype(F32))
    win_neg = jnp.where((m == 0)[:, None], 0.0, win_neg)
    zpad = lambda z: jnp.pad(z, ((0, 0), (0, HY_FFN - HY_EMB)))
    z_pair = jnp.concatenate([zpad(z_neg), zpad(z_fwd)], axis=1)
    win_pair = jnp.concatenate([win_neg, win_fwd], axis=1)
    fm, g_inv = _dft_tables(min(L, HY_BLOCK))
    bf = lambda a: a.astype(BF16)

    w_in_packed = _pack_w_in(w_in)
    wkv_all = jnp.concatenate([bf(x_wk), bf(x_wv)], axis=2)
    h = x.astype(F32).reshape(M, D)
    mem2 = mem.astype(F32).reshape(B * Mm, D)
    for i in range(depth):
        p2 = norm_matmul(h, ln_mix[i].astype(F32), w_in_packed, i, BF16)
        p3 = p2.reshape(B, L, N_PACK)
        y_a = attention(p3, attn_qnorm[i], attn_knorm[i], rope_tabs)
        f_all = hyena_filters(z_pair, win_pair, hy_w1[i].astype(F32), hy_b1[i].astype(F32),
                              hy_w2[i].astype(F32), hy_b2[i].astype(F32), hy_w3[i].astype(F32),
                              hy_b3[i].astype(F32), hy_freq[i].astype(F32))
        spec = hyena_spectra(fm, f_all)
        y_b = hyena_mixer(p3, hy_conv[i], spec, hy_skip[i], fm, g_inv)
        y_c = gla_mixer(p3, gla_w_lr[i], gla_b_lr[i], gla_onorm[i])
        h = merge_out(y_a.reshape(M, ATT_Q_W), y_b.reshape(M, HY_WIDTH), y_c.reshape(M, GLA_V_W),
                      p2, h, bf(w_br_attn[i]), bf(w_br_hyena[i]), bf(w_br_gla[i]), bf(w_out[i]))
        kv = norm_matmul(mem2, ln_mem[i].astype(F32), wkv_all, i, BF16).reshape(B, Mm, 2 * D)
        h = cross_attention(h.reshape(B, L, D), ln_x[i], bf(x_wq[i]), kv, x_qnorm[i],
                            x_knorm[i], bf(x_wo[i])).reshape(M, D)
        h = mlp(h, ln_mlp[i], bf(mlp_w1[i]), bf(mlp_w2[i]))
    return h.reshape(B, L, D).astype(x.dtype)
```

```python
import functools
import math

import jax
import jax.numpy as jnp
from jax import lax
from jax.experimental import pallas as pl
from jax.experimental.pallas import tpu as pltpu

F32 = jnp.float32
BF16 = jnp.bfloat16

D_MODEL = 1024
GRID_W = 64
ROPE_THETA = 10000.0
HEAD_DIM = 64
ATT_Q_HEADS = 8
ATT_KV_HEADS = 2
ATT_Q_W = ATT_Q_HEADS * HEAD_DIM
ATT_KV_W = ATT_KV_HEADS * HEAD_DIM
HY_WIDTH = 512
HY_ORDER = 2
HY_BANDS = 16
HY_EMB = 1 + 2 * HY_BANDS
HY_FFN = 64
HY_FAST_DECAY = 0.3
HY_SLOW_DECAY = 1.5
HY_TARGET = 1e-2
GLA_HEADS = 4
GLA_DK = 64
GLA_DV = 128
GLA_RANK = 16
GLA_NORMALIZER = 16.0
GLA_CHUNK = 64
GLA_K_W = GLA_HEADS * GLA_DK
GLA_V_W = GLA_HEADS * GLA_DV
X_HEADS = 4
X_HEAD_DIM = D_MODEL // X_HEADS
D_FF = 4 * D_MODEL
N_BRANCH = 3
EPS = 1e-6

COL_AQ = 0
COL_HY = 512
COL_GQ = 2048
COL_GK = 2304
COL_GV = 2560
COL_GO = 3072
COL_AK = 3584
COL_AV = 3712
COL_LR = 3840
COL_GATE = 4096
N_PACK = 7168

VMEM_LIMIT_BYTES = 52 * 1024 * 1024


def _cparams(*sem):
    return pltpu.CompilerParams(dimension_semantics=sem, vmem_limit_bytes=VMEM_LIMIT_BYTES)


def _tile(n, t):
    t = min(n, t)
    assert n % t == 0, (n, t)
    return t


def _dot(a, b):
    return jnp.dot(a, b, preferred_element_type=F32)


def _dot_nt(a, b):
    return lax.dot_general(a, b, (((1,), (1,)), ((), ())), preferred_element_type=F32)


def _dot_tn(a, b):
    return lax.dot_general(a, b, (((0,), (0,)), ((), ())), preferred_element_type=F32)


def _sigmoid(x):
    return 1.0 / (1.0 + jnp.exp(-x))


def _norm_mm_kernel(x_ref, g_ref, w_ref, o_ref, xn_ref):
    @pl.when(pl.program_id(1) == 0)
    def _():
        x = x_ref[...]
        ms = jnp.mean(x * x, axis=-1, keepdims=True)
        xn_ref[...] = (x * lax.rsqrt(ms + EPS) * g_ref[...]).astype(BF16)

    o_ref[...] = _dot(xn_ref[...], w_ref[...]).astype(o_ref.dtype)


def norm_matmul(x, g, w, layer, out_dtype, tm=1024, tn=3584):
    M, K = x.shape
    N = w.shape[2]
    tm, tn = _tile(M, tm), _tile(N, tn)
    return pl.pallas_call(
        _norm_mm_kernel,
        grid=(M // tm, N // tn),
        in_specs=[
            pl.BlockSpec((tm, K), lambda i, j: (i, 0)),
            pl.BlockSpec((1, K), lambda i, j: (0, 0)),
            pl.BlockSpec((None, K, tn), lambda i, j: (layer, 0, j)),
        ],
        out_specs=pl.BlockSpec((tm, tn), lambda i, j: (i, j)),
        out_shape=jax.ShapeDtypeStruct((M, N), out_dtype),
        scratch_shapes=[pltpu.VMEM((tm, K), BF16)],
        compiler_params=_cparams("parallel", "arbitrary"),
        name="norm_matmul",
    )(x, g.reshape(1, K), w)


def _group_mean_sq(x, gm_ref):
    return _dot((x * x).astype(BF16), gm_ref[...])


def _rope(x, c_ref, sa_ref, sb_ref):
    w = x.shape[-1]
    return (x * c_ref[...] + pltpu.roll(x, w - HEAD_DIM // 4, 1) * sa_ref[...]
            + pltpu.roll(x, HEAD_DIM // 4, 1) * sb_ref[...])


def _attn_kernel(q_ref, k_ref, v_ref, qn_ref, kn_ref, cq_ref, saq_ref, sbq_ref,
                 ck_ref, sak_ref, sbk_ref, gmq_ref, gmk_ref, o_ref, kp_ref, vx_ref):
    @pl.when(pl.program_id(1) == 0)
    def _():
        k = k_ref[0].astype(F32)
        kh = k * lax.rsqrt(_group_mean_sq(k, gmk_ref) + EPS) * kn_ref[...]
        kp_ref[...] = _rope(kh, ck_ref, sak_ref, sbk_ref).astype(BF16)
        v = v_ref[0]
        ones = jnp.ones((v.shape[0], HEAD_DIM), BF16)
        vx_ref[...] = jnp.concatenate(
            [piece for kv in range(ATT_KV_HEADS)
             for piece in (v[:, kv * HEAD_DIM:(kv + 1) * HEAD_DIM], ones)], axis=-1)

    q = q_ref[0].astype(F32)
    qh = q * lax.rsqrt(_group_mean_sq(q, gmq_ref) + EPS) * qn_ref[...]
    qb = (_rope(qh, cq_ref, saq_ref, sbq_ref) * (HEAD_DIM ** -0.5 * math.log2(math.e))).astype(BF16)
    group = ATT_Q_HEADS // ATT_KV_HEADS
    outs = []
    for h in range(ATT_Q_HEADS):
        kv = h // group
        kk = kp_ref[:, kv * HEAD_DIM:(kv + 1) * HEAD_DIM]
        s = _dot_nt(qb[:, h * HEAD_DIM:(h + 1) * HEAD_DIM], kk)
        p = jnp.exp2(s - jnp.max(s, axis=-1, keepdims=True))
        ox = _dot(p.astype(BF16), vx_ref[:, kv * 2 * HEAD_DIM:(kv + 1) * 2 * HEAD_DIM])
        outs.append(ox[:, :HEAD_DIM] / ox[:, HEAD_DIM:])
    o_ref[0] = jnp.concatenate(outs, axis=-1).astype(o_ref.dtype)


def _rope_tables(L):
    rows = L // GRID_W
    r, c = jnp.meshgrid(jnp.arange(rows), jnp.arange(GRID_W), indexing="ij")
    n_freq = HEAD_DIM // 4
    inv = ROPE_THETA ** (-jnp.arange(n_freq, dtype=F32) / n_freq)
    pos = jnp.stack([r.reshape(-1), c.reshape(-1)], axis=1).astype(F32)
    ang = pos[:, :, None] * inv
    cos, sin = jnp.cos(ang), jnp.sin(ang)
    zero = jnp.zeros_like(sin)
    c64 = jnp.concatenate([cos, cos], axis=-1).reshape(L, HEAD_DIM)
    sa64 = jnp.concatenate([-sin, zero], axis=-1).reshape(L, HEAD_DIM)
    sb64 = jnp.concatenate([zero, sin], axis=-1).reshape(L, HEAD_DIM)
    return c64, sa64, sb64


def attention(p3, qn, kn, tabs, tq=1024):
    B, L, _ = p3.shape
    tq = _tile(L, tq)
    c64, sa64, sb64 = tabs
    tq_tabs = [jnp.tile(t, (1, ATT_Q_HEADS)) for t in (c64, sa64, sb64)]
    tk_tabs = [jnp.tile(t, (1, ATT_KV_HEADS)) for t in (c64, sa64, sb64)]
    qn_t = jnp.tile(qn.astype(F32), ATT_Q_HEADS).reshape(1, ATT_Q_W)
    kn_t = jnp.tile(kn.astype(F32), ATT_KV_HEADS).reshape(1, ATT_KV_W)

    def group_mean(width):
        g = jnp.arange(width) // HEAD_DIM
        return jnp.where(g[:, None] == g[None, :], 1.0 / HEAD_DIM, 0.0).astype(BF16)

    qtab = pl.BlockSpec((tq, ATT_Q_W), lambda b, i: (i, 0))
    ktab = pl.BlockSpec((L, ATT_KV_W), lambda b, i: (0, 0))
    return pl.pallas_call(
        _attn_kernel,
        grid=(B, L // tq),
        in_specs=[
            pl.BlockSpec((1, tq, ATT_Q_W), lambda b, i: (b, i, COL_AQ // ATT_Q_W)),
            pl.BlockSpec((1, L, ATT_KV_W), lambda b, i: (b, 0, COL_AK // ATT_KV_W)),
            pl.BlockSpec((1, L, ATT_KV_W), lambda b, i: (b, 0, COL_AV // ATT_KV_W)),
            pl.BlockSpec((1, ATT_Q_W), lambda b, i: (0, 0)),
            pl.BlockSpec((1, ATT_KV_W), lambda b, i: (0, 0)),
            qtab, qtab, qtab, ktab, ktab, ktab,
            pl.BlockSpec((ATT_Q_W, ATT_Q_W), lambda b, i: (0, 0)),
            pl.BlockSpec((ATT_KV_W, ATT_KV_W), lambda b, i: (0, 0)),
        ],
        out_specs=pl.BlockSpec((1, tq, ATT_Q_W), lambda b, i: (b, i, 0)),
        out_shape=jax.ShapeDtypeStruct((B, L, ATT_Q_W), BF16),
        scratch_shapes=[pltpu.VMEM((L, ATT_KV_W), BF16), pltpu.VMEM((L, 2 * ATT_KV_W), BF16)],
        compiler_params=_cparams("parallel", "arbitrary"),
        name="gqa_attention",
    )(p3, p3, p3, qn_t, kn_t, *tq_tabs, *tk_tabs, group_mean(ATT_Q_W), group_mean(ATT_KV_W))


def _dft_tables(L):
    n = 2 * L
    k = jnp.arange(L, dtype=jnp.int32)[:, None]
    j = jnp.arange(L, dtype=jnp.int32)[None, :]
    step = 64
    ja = jnp.arange(0, L, step, dtype=jnp.int32)[None, :]
    jb = jnp.arange(step, dtype=jnp.int32)[None, :]
    ang_a = ((k * ja) % n).astype(F32) * (2.0 * math.pi / n)
    ang_b = ((k * jb) % n).astype(F32) * (2.0 * math.pi / n)
    ca, sa = jnp.cos(ang_a)[:, :, None], jnp.sin(ang_a)[:, :, None]
    cb, sb = jnp.cos(ang_b)[:, None, :], jnp.sin(ang_b)[:, None, :]
    c = (ca * cb - sa * sb).reshape(L, L)
    s = -(sa * cb + ca * sb).reshape(L, L)
    bottom = jnp.where(k == 0, jnp.where(j % 2 == 0, 1.0, -1.0), s)
    bottom_t = jnp.where(j == 0, jnp.where(k % 2 == 0, 1.0, -1.0), s)
    fm = jnp.stack([c, bottom]).astype(BF16)
    g = jnp.concatenate([c, bottom_t], axis=1).astype(BF16)
    return fm, g


def _hyena_pos_features(L, t):
    t_norm = t / max(L - 1, 1)
    w = 2.0 * math.pi * t / L
    f = jnp.linspace(1e-4, HY_BANDS - 1, HY_BANDS, dtype=F32)
    fw = w[:, None] * f
    z = jnp.concatenate([t_norm[:, None], jnp.cos(fw), -jnp.sin(fw)], axis=-1)
    deltas = jnp.abs(jnp.linspace(math.log(HY_TARGET) / HY_FAST_DECAY,
                                  math.log(HY_TARGET) / HY_SLOW_DECAY, HY_WIDTH, dtype=F32))
    window = jnp.exp(-t_norm[:, None] * deltas)
    return z, window


def _sconv_kernel(u_ref, w_ref, o_ref):
    u = u_ref[0].astype(F32)
    L = u.shape[0]
    row = lax.broadcasted_iota(jnp.int32, u.shape, 0)
    prev = jnp.where(row == 0, 0.0, pltpu.roll(u, 1, 0))
    nxt = jnp.where(row == L - 1, 0.0, pltpu.roll(u, L - 1, 0))
    o_ref[0] = (prev * w_ref[0:1, :] + u * w_ref[1:2, :] + nxt * w_ref[2:3, :]).astype(o_ref.dtype)


def short_conv(p3, w):
    B, L, _ = p3.shape
    nblk = 3
    return pl.pallas_call(
        _sconv_kernel,
        grid=(B, nblk),
        in_specs=[
            pl.BlockSpec((1, L, HY_WIDTH), lambda b, c: (b, 0, COL_HY // HY_WIDTH + c)),
            pl.BlockSpec((3, HY_WIDTH), lambda b, c: (0, c)),
        ],
        out_specs=pl.BlockSpec((1, L, HY_WIDTH), lambda b, c: (b, 0, c)),
        out_shape=jax.ShapeDtypeStruct((B, L, nblk * HY_WIDTH), BF16),
        compiler_params=_cparams("parallel", "parallel"),
        name="hyena_short_conv",
    )(p3, w)


def _hyfilter_kernel(z_ref, w1_ref, b1_ref, w2_ref, b2_ref, w3_ref, b3_ref, fr_ref, win_ref,
                     f_ref):
    L = z_ref.shape[0]
    hp = lax.Precision.HIGHEST
    h = jnp.sin(fr_ref[0:1, :] * (jnp.dot(z_ref[...], w1_ref[...], precision=hp,
                                          preferred_element_type=F32) + b1_ref[...]))
    h = jnp.sin(fr_ref[1:2, :] * (jnp.dot(h, w2_ref[...], precision=hp,
                                          preferred_element_type=F32) + b2_ref[...]))
    f = (_dot(h.astype(BF16), w3_ref[0].astype(BF16)) + b3_ref[0]) * win_ref[...]
    col = jnp.sum(jnp.abs(f), axis=0, keepdims=True)
    norm = col[:, :HY_WIDTH] + col[:, HY_WIDTH:] + EPS
    f_ref[:L, :] = (f[:, :HY_WIDTH] / norm).astype(f_ref.dtype)
    f_ref[L:, :] = (f[:, HY_WIDTH:] / norm).astype(f_ref.dtype)


def _block_diag2(w):
    z = jnp.zeros_like(w)
    return jnp.concatenate([jnp.concatenate([w, z], axis=1), jnp.concatenate([z, w], axis=1)], axis=0)


def hyena_filters(z_pair, win_pair, w1, b1, w2, b2, w3, b3, freq):
    L = z_pair.shape[0]
    pad_w1 = jnp.pad(w1, ((0, HY_FFN - HY_EMB), (0, 0)))
    two = lambda v: jnp.concatenate([v, v], axis=-1)
    w3r = w3.reshape(HY_FFN, HY_ORDER, 2, HY_WIDTH)
    zero = jnp.zeros((HY_ORDER, HY_FFN, HY_WIDTH), F32)
    w3n = jnp.moveaxis(w3r[:, :, 1], 1, 0)
    w3p = jnp.moveaxis(w3r[:, :, 0], 1, 0)
    w3bd = jnp.concatenate([jnp.concatenate([w3n, zero], axis=2),
                            jnp.concatenate([zero, w3p], axis=2)], axis=1)
    b3r = b3.reshape(HY_ORDER, 1, 2, HY_WIDTH)
    b3c = jnp.concatenate([b3r[:, :, 1], b3r[:, :, 0]], axis=-1)
    full = lambda shape: pl.BlockSpec(shape, lambda o: (0,) * len(shape))
    H2 = 2 * HY_FFN
    return pl.pallas_call(
        _hyfilter_kernel,
        grid=(HY_ORDER,),
        in_specs=[
            full((L, H2)), full((H2, H2)), full((1, H2)), full((H2, H2)), full((1, H2)),
            pl.BlockSpec((1, H2, 2 * HY_WIDTH), lambda o: (o, 0, 0)),
            pl.BlockSpec((1, 1, 2 * HY_WIDTH), lambda o: (o, 0, 0)),
            full((2, H2)), full((L, 2 * HY_WIDTH)),
        ],
        out_specs=pl.BlockSpec((2 * L, HY_WIDTH), lambda o: (0, o)),
        out_shape=jax.ShapeDtypeStruct((2 * L, HY_ORDER * HY_WIDTH), BF16),
        compiler_params=_cparams("parallel"),
        name="hyena_filter_mlp",
    )(z_pair, _block_diag2(pad_w1), two(b1.reshape(1, -1)), _block_diag2(w2),
      two(b2.reshape(1, -1)), w3bd, b3c, two(freq), win_pair)


def _spec_kernel(fm_ref, blk_ref, h_ref, prev_ref, *, n):
    P = fm_ref.shape[1]
    row = lax.broadcasted_iota(jnp.int32, (P, blk_ref.shape[1]), 0)
    sign = jnp.where(row % 2 == 0, 1.0, -1.0)
    scale = jnp.where(row == 0, 1.0 / n, 2.0 / n)
    @pl.when(pl.program_id(0) == 0)
    def _():
        prev_ref[...] = jnp.zeros_like(prev_ref)

    for half in range(2):
        phi = _dot(fm_ref[half], blk_ref[...])
        h_ref[0, half] = ((phi + sign * prev_ref[half]) * scale).astype(h_ref.dtype)
        prev_ref[half] = phi


def hyena_spectra(fm, f_all):
    P = fm.shape[1]
    W = f_all.shape[1]
    n_blk = f_all.shape[0] // P
    return pl.pallas_call(
        functools.partial(_spec_kernel, n=2 * P),
        grid=(n_blk,),
        in_specs=[
            pl.BlockSpec((2, P, P), lambda e: (0, 0, 0)),
            pl.BlockSpec((P, W), lambda e: (e, 0)),
        ],
        out_specs=pl.BlockSpec((1, 2, P, W), lambda e: (jnp.maximum(e - 1, 0), 0, 0, 0)),
        out_shape=jax.ShapeDtypeStruct((n_blk - 1, 2, P, W), BF16),
        scratch_shapes=[pltpu.VMEM((2, P, W), F32)],
        compiler_params=_cparams("arbitrary"),
        name="hyena_filter_spectrum",
    )(fm, f_all)


HY_BLOCK = 512
HY_CONV_ROWS = 16


def _lconv_kernel(fm_ref, g_ref, u_ref, gate_ref, skip_ref, h_ref, o_ref, x_ref, yt_ref, yb_ref):
    P = fm_ref.shape[1]
    nb = u_ref.shape[1] // P
    ct = u_ref.shape[2]
    for j in range(nb):
        uj = u_ref[0, j * P:(j + 1) * P, :]
        x_ref[j, 0] = _dot(fm_ref[0], uj).astype(BF16)
        x_ref[j, 1] = _dot(fm_ref[1], uj).astype(BF16)
    first = lax.broadcasted_iota(jnp.int32, (HY_CONV_ROWS, ct), 0) == 0
    zero = jnp.zeros((HY_CONV_ROWS, ct), BF16)
    for i in range(nb):
        for r in range(P // HY_CONV_ROWS):
            rows = slice(r * HY_CONV_ROWS, (r + 1) * HY_CONV_ROWS)
            at, ab = zero, zero
            for j in range(nb):
                d = i - j + nb - 1
                xt, xb = x_ref[j, 0, rows, :], x_ref[j, 1, rows, :]
                ht, hb = h_ref[d, 0, rows, :], h_ref[d, 1, rows, :]
                bb = xb * hb
                if r == 0:
                    at += xt * ht - jnp.where(first, zero, bb)
                    ab += jnp.where(first, bb, xt * hb + xb * ht)
                else:
                    at += xt * ht - bb
                    ab += xt * hb + xb * ht
            yt_ref[i, rows, :] = at
            yb_ref[i, rows, :] = ab
        y = _dot(g_ref[:, :P], yt_ref[i]) + _dot(g_ref[:, P:], yb_ref[i])
        blk = slice(i * P, (i + 1) * P)
        ui = u_ref[0, blk, :].astype(F32)
        o_ref[0, blk, :] = ((y + ui * skip_ref[...]) * gate_ref[0, blk, :].astype(F32)
                            ).astype(o_ref.dtype)


def long_conv(fm, g, spec, order, u, u_col, gate, gate_col, skip, ct=256):
    B, L, _ = u.shape
    P = fm.shape[1]
    nb = L // P
    n_lag = spec.shape[0]
    per = HY_WIDTH // ct
    return pl.pallas_call(
        _lconv_kernel,
        grid=(per, B),
        in_specs=[
            pl.BlockSpec((2, P, P), lambda c, b: (0, 0, 0)),
            pl.BlockSpec((P, 2 * P), lambda c, b: (0, 0)),
            pl.BlockSpec((1, L, ct), lambda c, b: (b, 0, u_col * per + c)),
            pl.BlockSpec((1, L, ct), lambda c, b: (b, 0, gate_col * per + c)),
            pl.BlockSpec((1, ct), lambda c, b: (0, c)),
            pl.BlockSpec((n_lag, 2, P, ct), lambda c, b: (0, 0, 0, order * per + c)),
        ],
        out_specs=pl.BlockSpec((1, L, ct), lambda c, b: (b, 0, c)),
        out_shape=jax.ShapeDtypeStruct((B, L, HY_WIDTH), BF16),
        scratch_shapes=[pltpu.VMEM((nb, 2, P, ct), BF16), pltpu.VMEM((nb, P, ct), BF16),
                        pltpu.VMEM((nb, P, ct), BF16)],
        compiler_params=_cparams("parallel", "parallel"),
        name="hyena_long_conv",
    )(fm, g, u, gate, skip.reshape(1, HY_WIDTH).astype(F32), spec)


def hyena_mixer(p3, conv_w, spec, skip, fm, g):
    uc = short_conv(p3, conv_w.astype(F32))
    z = long_conv(fm, g, spec, 0, uc, 0, uc, 1, skip[0])
    return long_conv(fm, g, spec, 1, z, 0, uc, 2, skip[1])


def _log_sigmoid(x):
    return -(jnp.maximum(-x, 0.0) + jnp.log(1.0 + jnp.exp(-jnp.abs(x))))


def _dot_hilo(m, x):
    hi = x.astype(BF16)
    lo = (x - hi.astype(F32)).astype(BF16)
    return _dot(m, hi) + _dot(m, lo)


GLA_PREP_ROWS = 256


def _gla_kernel(q_ref, k_ref, v_ref, og_ref, lr_ref, wlr_ref, blr_ref, on_ref, o_ref,
                qd_ref, kit_ref, kst_ref, dcol_ref, acc_ref):
    L = q_ref.shape[1]
    C = GLA_CHUNK
    H = GLA_HEADS
    RB = GLA_PREP_ROWS
    n_chunks = L // C
    KW, VW, DV = GLA_K_W, GLA_V_W, GLA_DV

    pr = lax.broadcasted_iota(jnp.int32, (RB, RB), 0)
    pc = lax.broadcasted_iota(jnp.int32, (RB, RB), 1)
    same = (pr // C) == (pc // C)
    cum_f = jnp.where(same & (pc <= pr), 1.0, 0.0).astype(BF16)
    cum_b = jnp.where(same & (pc >= pr), 1.0, 0.0).astype(BF16)
    tot_m = jnp.where(same, 1.0, 0.0).astype(BF16)
    PW = 2 * C
    n_pairs = L // PW

    def prep(i, carry):
        r0 = pl.multiple_of(i * RB, RB)
        logit = _dot(lr_ref[0, pl.ds(r0, RB), :], wlr_ref[...]) + blr_ref[...]
        g = _log_sigmoid(logit) * (1.0 / GLA_NORMALIZER)
        q = q_ref[0, pl.ds(r0, RB), :].astype(F32) * (GLA_DK ** -0.5)
        k = k_ref[0, pl.ds(r0, RB), :].astype(F32)
        for d, cum in ((0, cum_f), (1, cum_b)):
            gd = g[:, d * KW:(d + 1) * KW]
            b = _dot_hilo(cum, gd)
            tot = _dot_hilo(tot_m, gd)
            qd_ref[d, pl.ds(r0, RB), :] = (q * jnp.exp(b)).astype(BF16)
            ki_t = (k * jnp.exp(-b)).T
            ks_t = (k * jnp.exp(tot - b)).T
            dec_t = jnp.exp(tot).T
            for p in range(RB // PW):
                rows = pl.ds(pl.multiple_of((i * (RB // PW) + p) * KW, KW), KW)
                cols = slice(p * PW, (p + 1) * PW)
                kit_ref[d, rows, :] = ki_t[:, cols].astype(BF16)
                kst_ref[d, rows, :] = ks_t[:, cols].astype(BF16)
                dcol_ref[d, rows, :] = dec_t[:, cols]
        return carry

    lax.fori_loop(0, L // RB, prep, 0)

    kk_blk = (lax.broadcasted_iota(jnp.int32, (KW, H * C), 0) // GLA_DK
              == lax.broadcasted_iota(jnp.int32, (KW, H * C), 1) // C)
    kv_blk = (lax.broadcasted_iota(jnp.int32, (KW, VW), 0) // GLA_DK
              == lax.broadcasted_iota(jnp.int32, (KW, VW), 1) // DV)
    arow = lax.broadcasted_iota(jnp.int32, (C, H * C), 0)
    acol = lax.broadcasted_iota(jnp.int32, (C, H * C), 1) % C
    amasks = (acol <= arow, acol > arow)
    zero_b = jnp.zeros((), BF16)

    def body(ip, sts):
        new = []
        for d in (0, 1):
            st = sts[d]
            pair = ip if d == 0 else n_pairs - 1 - ip
            rk = pl.multiple_of(pair * KW, KW)
            kit2 = kit_ref[d, pl.ds(rk, KW), :]
            kst2 = kst_ref[d, pl.ds(rk, KW), :]
            dec2 = dcol_ref[d, pl.ds(rk, KW), :]
            for half in ((0, 1) if d == 0 else (1, 0)):
                r0 = pl.multiple_of(pair * PW + half * C, C)
                cols = slice(half * C, (half + 1) * C)
                qd = qd_ref[d, pl.ds(r0, C), :]
                v = v_ref[0, pl.ds(r0, C), :]
                kk = jnp.where(kk_blk, jnp.concatenate([kit2[:, cols]] * H, axis=1), zero_b)
                a = jnp.where(amasks[d], _dot(qd, kk), 0.0).astype(BF16)
                v_bd = jnp.where(kv_blk, jnp.concatenate([v] * H, axis=0), zero_b)
                st_b = st.astype(BF16)
                s_bd = jnp.where(kv_blk, jnp.concatenate([st_b] * H, axis=1), zero_b)
                acc_ref[d, pl.ds(r0, C), :] = _dot(a, v_bd) + _dot(qd, s_bd)
                ds = jnp.concatenate(
                    [_dot(kst2[h * GLA_DK:(h + 1) * GLA_DK, cols], v[:, h * DV:(h + 1) * DV])
                     for h in range(H)], axis=0)
                st = st * jnp.concatenate([dec2[:, cols]] * (DV // C), axis=1) + ds
            new.append(st)
        return tuple(new)

    zero = jnp.zeros((KW, DV), F32)
    lax.fori_loop(0, n_pairs, body, (zero, zero), unroll=4)

    def finalize(i, carry):
        r0 = pl.multiple_of(i * RB, RB)
        o = acc_ref[0, pl.ds(r0, RB), :] + acc_ref[1, pl.ds(r0, RB), :]
        og = og_ref[0, pl.ds(r0, RB), :].astype(F32)
        outs = []
        for h in range(H):
            oh = o[:, h * GLA_DV:(h + 1) * GLA_DV]
            ms = jnp.mean(oh * oh, axis=-1, keepdims=True)
            outs.append(oh * lax.rsqrt(ms + EPS) * on_ref[...])
        y = jnp.concatenate(outs, axis=-1) * (og * _sigmoid(og))
        o_ref[0, pl.ds(r0, RB), :] = y.astype(o_ref.dtype)
        return carry

    lax.fori_loop(0, L // RB, finalize, 0, unroll=2)


def gla_mixer(p3, w_lr, b_lr, onorm):
    B, L, _ = p3.shape
    lr_w = 256
    wl = jnp.zeros((lr_w, 2 * GLA_K_W), F32)
    wl = wl.at[0:GLA_RANK, 0:GLA_K_W].set(w_lr[0].astype(F32))
    wl = wl.at[GLA_RANK:2 * GLA_RANK, GLA_K_W:].set(w_lr[1].astype(F32))
    bl = b_lr.astype(F32).reshape(1, 2 * GLA_K_W)
    col = lambda width, c: pl.BlockSpec((1, L, width), lambda b: (b, 0, c // width))
    return pl.pallas_call(
        _gla_kernel,
        grid=(B,),
        in_specs=[
            col(GLA_K_W, COL_GQ), col(GLA_K_W, COL_GK), col(GLA_V_W, COL_GV),
            col(GLA_V_W, COL_GO), col(lr_w, COL_LR),
            pl.BlockSpec((lr_w, 2 * GLA_K_W), lambda b: (0, 0)),
            pl.BlockSpec((1, 2 * GLA_K_W), lambda b: (0, 0)),
            pl.BlockSpec((1, GLA_DV), lambda b: (0, 0)),
        ],
        out_specs=pl.BlockSpec((1, L, GLA_V_W), lambda b: (b, 0, 0)),
        out_shape=jax.ShapeDtypeStruct((B, L, GLA_V_W), BF16),
        scratch_shapes=[
            pltpu.VMEM((2, L, GLA_K_W), BF16),
            pltpu.VMEM((2, L // (2 * GLA_CHUNK) * GLA_K_W, 2 * GLA_CHUNK), BF16),
            pltpu.VMEM((2, L // (2 * GLA_CHUNK) * GLA_K_W, 2 * GLA_CHUNK), BF16),
            pltpu.VMEM((2, L // (2 * GLA_CHUNK) * GLA_K_W, 2 * GLA_CHUNK), F32),
            pltpu.VMEM((2, L, GLA_V_W), F32),
        ],
        compiler_params=_cparams("parallel"),
        name="gla_mixer",
    )(p3, p3, p3, p3, p3, wl.astype(BF16), bl, onorm.astype(F32).reshape(1, GLA_DV))


def _merge_kernel(a_ref, b_ref, c_ref, ga_ref, gb_ref, gc_ref, h_ref, wa_ref, wb_ref, wc_ref,
                  wo_ref, o_ref):
    mixed = (_sigmoid(ga_ref[...].astype(F32)) * _dot(a_ref[...], wa_ref[...])
             + _sigmoid(gb_ref[...].astype(F32)) * _dot(b_ref[...], wb_ref[...])
             + _sigmoid(gc_ref[...].astype(F32)) * _dot(c_ref[...], wc_ref[...]))
    o_ref[...] = h_ref[...] + _dot(mixed.astype(BF16), wo_ref[...])


def merge_out(ya, yb, yc, p2, h, wa, wb, wc, wo, tm=512):
    M, D = h.shape
    tm = _tile(M, tm)
    br = lambda w: pl.BlockSpec((tm, w), lambda i: (i, 0))
    gate = lambda c: pl.BlockSpec((tm, D), lambda i: (i, COL_GATE // D + c))
    wfull = lambda r: pl.BlockSpec((r, D), lambda i: (0, 0))
    return pl.pallas_call(
        _merge_kernel,
        grid=(M // tm,),
        in_specs=[br(ATT_Q_W), br(HY_WIDTH), br(GLA_V_W), gate(0), gate(1), gate(2),
                  pl.BlockSpec((tm, D), lambda i: (i, 0)),
                  wfull(ATT_Q_W), wfull(HY_WIDTH), wfull(GLA_V_W), wfull(D)],
        out_specs=pl.BlockSpec((tm, D), lambda i: (i, 0)),
        out_shape=jax.ShapeDtypeStruct((M, D), F32),
        compiler_params=_cparams("parallel"),
        name="gated_merge_out_proj",
    )(ya, yb, yc, p2, p2, p2, h, wa, wb, wc, wo)


def _head_rmsnorm(x, g):
    ms = jnp.mean(x * x, axis=-1, keepdims=True)
    return x * lax.rsqrt(ms + EPS) * g


def _xattn_kernel(h_ref, ln_ref, wq_ref, k_ref, v_ref, qn_ref, kn_ref, wo_ref, o_ref, att_ref):
    h = h_ref[0]
    hn = _head_rmsnorm(h, ln_ref[...]).astype(BF16)
    q = _dot(hn, wq_ref[...])
    for hd in range(X_HEADS):
        sl = slice(hd * X_HEAD_DIM, (hd + 1) * X_HEAD_DIM)
        qh = (_head_rmsnorm(q[:, sl], qn_ref[...]) * (X_HEAD_DIM ** -0.5)).astype(BF16)
        kh = _head_rmsnorm(k_ref[0, :, sl].astype(F32), kn_ref[...]).astype(BF16)
        s = _dot_nt(qh, kh)
        p = jnp.exp(s - jnp.max(s, axis=-1, keepdims=True))
        l = jnp.sum(p, axis=-1, keepdims=True)
        att_ref[:, sl] = (_dot(p.astype(BF16), v_ref[0, :, sl]) / l).astype(BF16)
    o_ref[0] = h + _dot(att_ref[...], wo_ref[...])


def cross_attention(h3, ln, wq, kv3, qn, kn, wo, tl=1024):
    B, L, D = h3.shape
    Mm = kv3.shape[1]
    tl = _tile(L, tl)
    vec = lambda w: pl.BlockSpec((1, w), lambda b, i: (0, 0))
    mat = pl.BlockSpec((D, D), lambda b, i: (0, 0))
    return pl.pallas_call(
        _xattn_kernel,
        grid=(B, L // tl),
        in_specs=[
            pl.BlockSpec((1, tl, D), lambda b, i: (b, i, 0)),
            vec(D), mat,
            pl.BlockSpec((1, Mm, D), lambda b, i: (b, 0, 0)),
            pl.BlockSpec((1, Mm, D), lambda b, i: (b, 0, 1)),
            vec(X_HEAD_DIM), vec(X_HEAD_DIM), mat,
        ],
        out_specs=pl.BlockSpec((1, tl, D), lambda b, i: (b, i, 0)),
        out_shape=jax.ShapeDtypeStruct((B, L, D), F32),
        scratch_shapes=[pltpu.VMEM((tl, D), BF16)],
        compiler_params=_cparams("parallel", "parallel"),
        name="memory_cross_attention",
    )(h3, ln.astype(F32).reshape(1, D), wq, kv3, kv3, qn.astype(F32).reshape(1, X_HEAD_DIM),
      kn.astype(F32).reshape(1, X_HEAD_DIM), wo)


def _mlp_kernel(h_ref, g_ref, w1_ref, w2_ref, o_ref, hn_ref, acc_ref):
    j = pl.program_id(1)

    @pl.when(j == 0)
    def _():
        hn_ref[...] = _head_rmsnorm(h_ref[...], g_ref[...]).astype(BF16)
        acc_ref[...] = jnp.zeros_like(acc_ref)

    a = jnp.maximum(_dot(hn_ref[...], w1_ref[...]), 0.0)
    acc_ref[...] += _dot((a * a).astype(BF16), w2_ref[...])

    @pl.when(j == pl.num_programs(1) - 1)
    def _():
        o_ref[...] = h_ref[...] + acc_ref[...]


def mlp(h, g, w1, w2, tm=1024, tf=2048):
    M, D = h.shape
    F = w1.shape[1]
    tm, tf = _tile(M, tm), _tile(F, tf)
    return pl.pallas_call(
        _mlp_kernel,
        grid=(M // tm, F // tf),
        in_specs=[
            pl.BlockSpec((tm, D), lambda i, j: (i, 0)),
            pl.BlockSpec((1, D), lambda i, j: (0, 0)),
            pl.BlockSpec((D, tf), lambda i, j: (0, j)),
            pl.BlockSpec((tf, D), lambda i, j: (j, 0)),
        ],
        out_specs=pl.BlockSpec((tm, D), lambda i, j: (i, 0)),
        out_shape=jax.ShapeDtypeStruct((M, D), F32),
        scratch_shapes=[pltpu.VMEM((tm, D), BF16), pltpu.VMEM((tm, D), F32)],
        compiler_params=_cparams("parallel", "arbitrary"),
        name="relu2_mlp",
    )(h, g.astype(F32).reshape(1, D), w1, w2)


def _pack_w_in(w):
    kv0 = ATT_Q_W
    hy0 = kv0 + 2 * ATT_KV_W
    lr0 = hy0 + 3 * HY_WIDTH + 2 * GLA_K_W + 2 * GLA_V_W
    g0 = lr0 + 2 * GLA_RANK
    assert lr0 - hy0 == COL_AK - COL_HY and w.shape[2] - g0 == N_PACK - COL_GATE
    packed = jnp.zeros(w.shape[:2] + (N_PACK,), BF16)
    for src0, src1, dst in ((0, kv0, COL_AQ), (kv0, hy0, COL_AK), (hy0, lr0, COL_HY),
                            (lr0, g0, COL_LR), (g0, w.shape[2], COL_GATE)):
        packed = lax.dynamic_update_slice(packed, w[:, :, src0:src1].astype(BF16), (0, 0, dst))
    return packed


def kernel(x, mem, ln_mix, w_in, attn_qnorm, attn_knorm, hy_conv, hy_w1, hy_b1, hy_w2, hy_b2,
           hy_w3, hy_b3, hy_freq, hy_skip, gla_w_lr, gla_b_lr, gla_onorm, w_br_attn, w_br_hyena,
           w_br_gla, w_out, ln_x, ln_mem, x_wq, x_wk, x_wv, x_wo, x_qnorm, x_knorm, ln_mlp,
           mlp_w1, mlp_w2):
    B, L, D = x.shape
    Mm = mem.shape[1]
    depth = w_in.shape[0]
    M = B * L

    rope_tabs = _rope_tables(L)
    m = jnp.arange(L)
    z_neg, win_neg = _hyena_pos_features(L, (L - m).astype(F32))
    z_fwd, win_fwd = _hyena_pos_features(L, m.astype(F32))
    win_neg = jnp.where((m == 0)[:, None], 0.0, win_neg)
    zpad = lambda z: jnp.pad(z, ((0, 0), (0, HY_FFN - HY_EMB)))
    z_pair = jnp.concatenate([zpad(z_neg), zpad(z_fwd)], axis=1)
    win_pair = jnp.concatenate([win_neg, win_fwd], axis=1)
    fm, g_inv = _dft_tables(min(L, HY_BLOCK))
    bf = lambda a: a.astype(BF16)

    w_in_packed = _pack_w_in(w_in)
    wkv_all = jnp.concatenate([bf(x_wk), bf(x_wv)], axis=2)
    h = x.astype(F32).reshape(M, D)
    mem2 = mem.astype(F32).reshape(B * Mm, D)
    for i in range(depth):
        p2 = norm_matmul(h, ln_mix[i].astype(F32), w_in_packed, i, BF16)
        p3 = p2.reshape(B, L, N_PACK)
        y_a = attention(p3, attn_qnorm[i], attn_knorm[i], rope_tabs)
        f_all = hyena_filters(z_pair, win_pair, hy_w1[i].astype(F32), hy_b1[i].astype(F32),
                              hy_w2[i].astype(F32), hy_b2[i].astype(F32), hy_w3[i].astype(F32),
                              hy_b3[i].astype(F32), hy_freq[i].astype(F32))
        spec = hyena_spectra(fm, f_all)
        y_b = hyena_mixer(p3, hy_conv[i], spec, hy_skip[i], fm, g_inv)
        y_c = gla_mixer(p3, gla_w_lr[i], gla_b_lr[i], gla_onorm[i])
        h = merge_out(y_a.reshape(M, ATT_Q_W), y_b.reshape(M, HY_WIDTH), y_c.reshape(M, GLA_V_W),
                      p2, h, bf(w_br_attn[i]), bf(w_br_hyena[i]), bf(w_br_gla[i]), bf(w_out[i]))
        kv = norm_matmul(mem2, ln_mem[i].astype(F32), wkv_all, i, BF16).reshape(B, Mm, 2 * D)
        h = cross_attention(h.reshape(B, L, D), ln_x[i], bf(x_wq[i]), kv, x_qnorm[i],
                            x_knorm[i], bf(x_wo[i])).reshape(M, D)
        h = mlp(h, ln_mlp[i], bf(mlp_w1[i]), bf(mlp_w2[i]))
    return h.reshape(B, L, D).astype(x.dtype)
```

```python
import functools
import math

import jax
import jax.numpy as jnp
from jax import lax
from jax.experimental import pallas as pl
from jax.experimental.pallas import tpu as pltpu

F32 = jnp.float32
BF16 = jnp.bfloat16

D_MODEL = 1024
GRID_W = 64
ROPE_THETA = 10000.0
HEAD_DIM = 64
ATT_Q_HEADS = 8
ATT_KV_HEADS = 2
ATT_Q_W = ATT_Q_HEADS * HEAD_DIM
ATT_KV_W = ATT_KV_HEADS * HEAD_DIM
HY_WIDTH = 512
HY_ORDER = 2
HY_BANDS = 16
HY_EMB = 1 + 2 * HY_BANDS
HY_FFN = 64
HY_FAST_DECAY = 0.3
HY_SLOW_DECAY = 1.5
HY_TARGET = 1e-2
GLA_HEADS = 4
GLA_DK = 64
GLA_DV = 128
GLA_RANK = 16
GLA_NORMALIZER = 16.0
GLA_CHUNK = 64
GLA_K_W = GLA_HEADS * GLA_DK
GLA_V_W = GLA_HEADS * GLA_DV
X_HEADS = 4
X_HEAD_DIM = D_MODEL // X_HEADS
D_FF = 4 * D_MODEL
N_BRANCH = 3
EPS = 1e-6

COL_AQ = 0
COL_HY = 512
COL_GQ = 2048
COL_GK = 2304
COL_GV = 2560
COL_GO = 3072
COL_AK = 3584
COL_AV = 3712
COL_LR = 3840
COL_GATE = 4096
N_PACK = 7168

VMEM_LIMIT_BYTES = 52 * 1024 * 1024


def _cparams(*sem):
    return pltpu.CompilerParams(dimension_semantics=sem, vmem_limit_bytes=VMEM_LIMIT_BYTES)


def _tile(n, t):
    t = min(n, t)
    assert n % t == 0, (n, t)
    return t


def _dot(a, b):
    return jnp.dot(a, b, preferred_element_type=F32)


def _dot_nt(a, b):
    return lax.dot_general(a, b, (((1,), (1,)), ((), ())), preferred_element_type=F32)


def _dot_tn(a, b):
    return lax.dot_general(a, b, (((0,), (0,)), ((), ())), preferred_element_type=F32)


def _sigmoid(x):
    return 1.0 / (1.0 + jnp.exp(-x))


def _norm_mm_kernel(x_ref, g_ref, w_ref, o_ref, xn_ref):
    @pl.when(pl.program_id(1) == 0)
    def _():
        x = x_ref[...]
        ms = jnp.mean(x * x, axis=-1, keepdims=True)
        xn_ref[...] = (x * lax.rsqrt(ms + EPS) * g_ref[...]).astype(BF16)

    o_ref[...] = _dot(xn_ref[...], w_ref[...]).astype(o_ref.dtype)


def norm_matmul(x, g, w, layer, out_dtype, tm=1024, tn=3584):
    M, K = x.shape
    N = w.shape[2]
    tm, tn = _tile(M, tm), _tile(N, tn)
    return pl.pallas_call(
        _norm_mm_kernel,
        grid=(M // tm, N // tn),
        in_specs=[
            pl.BlockSpec((tm, K), lambda i, j: (i, 0)),
            pl.BlockSpec((1, K), lambda i, j: (0, 0)),
            pl.BlockSpec((None, K, tn), lambda i, j: (layer, 0, j)),
        ],
        out_specs=pl.BlockSpec((tm, tn), lambda i, j: (i, j)),
        out_shape=jax.ShapeDtypeStruct((M, N), out_dtype),
        scratch_shapes=[pltpu.VMEM((tm, K), BF16)],
        compiler_params=_cparams("parallel", "arbitrary"),
        name="norm_matmul",
    )(x, g.reshape(1, K), w)


def _group_mean_sq(x, gm_ref):
    return _dot((x * x).astype(BF16), gm_ref[...])


def _rope(x, c_ref, sa_ref, sb_ref):
    w = x.shape[-1]
    return (x * c_ref[...] + pltpu.roll(x, w - HEAD_DIM // 4, 1) * sa_ref[...]
            + pltpu.roll(x, HEAD_DIM // 4, 1) * sb_ref[...])


def _attn_kernel(q_ref, k_ref, v_ref, qn_ref, kn_ref, cq_ref, saq_ref, sbq_ref,
                 ck_ref, sak_ref, sbk_ref, gmq_ref, gmk_ref, o_ref, kp_ref, vx_ref):
    @pl.when(pl.program_id(1) == 0)
    def _():
        k = k_ref[0].astype(F32)
        kh = k * lax.rsqrt(_group_mean_sq(k, gmk_ref) + EPS) * kn_ref[...]
        kp_ref[...] = _rope(kh, ck_ref, sak_ref, sbk_ref).astype(BF16)
        v = v_ref[0]
        ones = jnp.ones((v.shape[0], HEAD_DIM), BF16)
        vx_ref[...] = jnp.concatenate(
            [piece for kv in range(ATT_KV_HEADS)
             for piece in (v[:, kv * HEAD_DIM:(kv + 1) * HEAD_DIM], ones)], axis=-1)

    q = q_ref[0].astype(F32)
    qh = q * lax.rsqrt(_group_mean_sq(q, gmq_ref) + EPS) * qn_ref[...]
    qb = (_rope(qh, cq_ref, saq_ref, sbq_ref) * (HEAD_DIM ** -0.5 * math.log2(math.e))).astype(BF16)
    group = ATT_Q_HEADS // ATT_KV_HEADS
    outs = []
    for h in range(ATT_Q_HEADS):
        kv = h // group
        kk = kp_ref[:, kv * HEAD_DIM:(kv + 1) * HEAD_DIM]
        s = _dot_nt(qb[:, h * HEAD_DIM:(h + 1) * HEAD_DIM], kk)
        p = jnp.exp2(s - jnp.max(s, axis=-1, keepdims=True))
        ox = _dot(p.astype(BF16), vx_ref[:, kv * 2 * HEAD_DIM:(kv + 1) * 2 * HEAD_DIM])
        outs.append(ox[:, :HEAD_DIM] / ox[:, HEAD_DIM:])
    o_ref[0] = jnp.concatenate(outs, axis=-1).astype(o_ref.dtype)


def _rope_tables(L):
    rows = L // GRID_W
    r, c = jnp.meshgrid(jnp.arange(rows), jnp.arange(GRID_W), indexing="ij")
    n_freq = HEAD_DIM // 4
    inv = ROPE_THETA ** (-jnp.arange(n_freq, dtype=F32) / n_freq)
    pos = jnp.stack([r.reshape(-1), c.reshape(-1)], axis=1).astype(F32)
    ang = pos[:, :, None] * inv
    cos, sin = jnp.cos(ang), jnp.sin(ang)
    zero = jnp.zeros_like(sin)
    c64 = jnp.concatenate([cos, cos], axis=-1).reshape(L, HEAD_DIM)
    sa64 = jnp.concatenate([-sin, zero], axis=-1).reshape(L, HEAD_DIM)
    sb64 = jnp.concatenate([zero, sin], axis=-1).reshape(L, HEAD_DIM)
    return c64, sa64, sb64


def attention(p3, qn, kn, tabs, tq=1024):
    B, L, _ = p3.shape
    tq = _tile(L, tq)
    c64, sa64, sb64 = tabs
    tq_tabs = [jnp.tile(t, (1, ATT_Q_HEADS)) for t in (c64, sa64, sb64)]
    tk_tabs = [jnp.tile(t, (1, ATT_KV_HEADS)) for t in (c64, sa64, sb64)]
    qn_t = jnp.tile(qn.astype(F32), ATT_Q_HEADS).reshape(1, ATT_Q_W)
    kn_t = jnp.tile(kn.astype(F32), ATT_KV_HEADS).reshape(1, ATT_KV_W)

    def group_mean(width):
        g = jnp.arange(width) // HEAD_DIM
        return jnp.where(g[:, None] == g[None, :], 1.0 / HEAD_DIM, 0.0).astype(BF16)

    qtab = pl.BlockSpec((tq, ATT_Q_W), lambda b, i: (i, 0))
    ktab = pl.BlockSpec((L, ATT_KV_W), lambda b, i: (0, 0))
    return pl.pallas_call(
        _attn_kernel,
        grid=(B, L // tq),
        in_specs=[
            pl.BlockSpec((1, tq, ATT_Q_W), lambda b, i: (b, i, COL_AQ // ATT_Q_W)),
            pl.BlockSpec((1, L, ATT_KV_W), lambda b, i: (b, 0, COL_AK // ATT_KV_W)),
            pl.BlockSpec((1, L, ATT_KV_W), lambda b, i: (b, 0, COL_AV // ATT_KV_W)),
            pl.BlockSpec((1, ATT_Q_W), lambda b, i: (0, 0)),
            pl.BlockSpec((1, ATT_KV_W), lambda b, i: (0, 0)),
            qtab, qtab, qtab, ktab, ktab, ktab,
            pl.BlockSpec((ATT_Q_W, ATT_Q_W), lambda b, i: (0, 0)),
            pl.BlockSpec((ATT_KV_W, ATT_KV_W), lambda b, i: (0, 0)),
        ],
        out_specs=pl.BlockSpec((1, tq, ATT_Q_W), lambda b, i: (b, i, 0)),
        out_shape=jax.ShapeDtypeStruct((B, L, ATT_Q_W), BF16),
        scratch_shapes=[pltpu.VMEM((L, ATT_KV_W), BF16), pltpu.VMEM((L, 2 * ATT_KV_W), BF16)],
        compiler_params=_cparams("parallel", "arbitrary"),
        name="gqa_attention",
    )(p3, p3, p3, qn_t, kn_t, *tq_tabs, *tk_tabs, group_mean(ATT_Q_W), group_mean(ATT_KV_W))


def _dft_tables(L):
    n = 2 * L
    k = jnp.arange(L, dtype=jnp.int32)[:, None]
    j = jnp.arange(L, dtype=jnp.int32)[None, :]
    step = 64
    ja = jnp.arange(0, L, step, dtype=jnp.int32)[None, :]
    jb = jnp.arange(step, dtype=jnp.int32)[None, :]
    ang_a = ((k * ja) % n).astype(F32) * (2.0 * math.pi / n)
    ang_b = ((k * jb) % n).astype(F32) * (2.0 * math.pi / n)
    ca, sa = jnp.cos(ang_a)[:, :, None], jnp.sin(ang_a)[:, :, None]
    cb, sb = jnp.cos(ang_b)[:, None, :], jnp.sin(ang_b)[:, None, :]
    c = (ca * cb - sa * sb).reshape(L, L)
    s = -(sa * cb + ca * sb).reshape(L, L)
    bottom = jnp.where(k == 0, jnp.where(j % 2 == 0, 1.0, -1.0), s)
    bottom_t = jnp.where(j == 0, jnp.where(k % 2 == 0, 1.0, -1.0), s)
    fm = jnp.stack([c, bottom]).astype(BF16)
    g = jnp.concatenate([c, bottom_t], axis=1).astype(BF16)
    return fm, g


def _hyena_pos_features(L, t):
    t_norm = t / max(L - 1, 1)
    w = 2.0 * math.pi * t / L
    f = jnp.linspace(1e-4, HY_BANDS - 1, HY_BANDS, dtype=F32)
    fw = w[:, None] * f
    z = jnp.concatenate([t_norm[:, None], jnp.cos(fw), -jnp.sin(fw)], axis=-1)
    deltas = jnp.abs(jnp.linspace(math.log(HY_TARGET) / HY_FAST_DECAY,
                                  math.log(HY_TARGET) / HY_SLOW_DECAY, HY_WIDTH, dtype=F32))
    window = jnp.exp(-t_norm[:, None] * deltas)
    return z, window


def _sconv_kernel(u_ref, w_ref, o_ref):
    u = u_ref[0].astype(F32)
    L = u.shape[0]
    row = lax.broadcasted_iota(jnp.int32, u.shape, 0)
    prev = jnp.where(row == 0, 0.0, pltpu.roll(u, 1, 0))
    nxt = jnp.where(row == L - 1, 0.0, pltpu.roll(u, L - 1, 0))
    o_ref[0] = (prev * w_ref[0:1, :] + u * w_ref[1:2, :] + nxt * w_ref[2:3, :]).astype(o_ref.dtype)


def short_conv(p3, w):
    B, L, _ = p3.shape
    nblk = 3
    return pl.pallas_call(
        _sconv_kernel,
        grid=(B, nblk),
        in_specs=[
            pl.BlockSpec((1, L, HY_WIDTH), lambda b, c: (b, 0, COL_HY // HY_WIDTH + c)),
            pl.BlockSpec((3, HY_WIDTH), lambda b, c: (0, c)),
        ],
        out_specs=pl.BlockSpec((1, L, HY_WIDTH), lambda b, c: (b, 0, c)),
        out_shape=jax.ShapeDtypeStruct((B, L, nblk * HY_WIDTH), BF16),
        compiler_params=_cparams("parallel", "parallel"),
        name="hyena_short_conv",
    )(p3, w)


def _hyfilter_kernel(z_ref, w1_ref, b1_ref, w2_ref, b2_ref, w3_ref, b3_ref, fr_ref, win_ref,
                     f_ref):
    L = z_ref.shape[0]
    hp = lax.Precision.HIGHEST
    h = jnp.sin(fr_ref[0:1, :] * (jnp.dot(z_ref[...], w1_ref[...], precision=hp,
                                          preferred_element_type=F32) + b1_ref[...]))
    h = jnp.sin(fr_ref[1:2, :] * (jnp.dot(h, w2_ref[...], precision=hp,
                                          preferred_element_type=F32) + b2_ref[...]))
    f = (_dot(h.astype(BF16), w3_ref[0].astype(BF16)) + b3_ref[0]) * win_ref[...]
    col = jnp.sum(jnp.abs(f), axis=0, keepdims=True)
    norm = col[:, :HY_WIDTH] + col[:, HY_WIDTH:] + EPS
    f_ref[:L, :] = (f[:, :HY_WIDTH] / norm).astype(f_ref.dtype)
    f_ref[L:, :] = (f[:, HY_WIDTH:] / norm).astype(f_ref.dtype)


def _block_diag2(w):
    z = jnp.zeros_like(w)
    return jnp.concatenate([jnp.concatenate([w, z], axis=1), jnp.concatenate([z, w], axis=1)], axis=0)


def hyena_filters(z_pair, win_pair, w1, b1, w2, b2, w3, b3, freq):
    L = z_pair.shape[0]
    pad_w1 = jnp.pad(w1, ((0, HY_FFN - HY_EMB), (0, 0)))
    two = lambda v: jnp.concatenate([v, v], axis=-1)
    w3r = w3.reshape(HY_FFN, HY_ORDER, 2, HY_WIDTH)
    zero = jnp.zeros((HY_ORDER, HY_FFN, HY_WIDTH), F32)
    w3n = jnp.moveaxis(w3r[:, :, 1], 1, 0)
    w3p = jnp.moveaxis(w3r[:, :, 0], 1, 0)
    w3bd = jnp.concatenate([jnp.concatenate([w3n, zero], axis=2),
                            jnp.concatenate([zero, w3p], axis=2)], axis=1)
    b3r = b3.reshape(HY_ORDER, 1, 2, HY_WIDTH)
    b3c = jnp.concatenate([b3r[:, :, 1], b3r[:, :, 0]], axis=-1)
    full = lambda shape: pl.BlockSpec(shape, lambda o: (0,) * len(shape))
    H2 = 2 * HY_FFN
    return pl.pallas_call(
        _hyfilter_kernel,
        grid=(HY_ORDER,),
        in_specs=[
            full((L, H2)), full((H2, H2)), full((1, H2)), full((H2, H2)), full((1, H2)),
            pl.BlockSpec((1, H2, 2 * HY_WIDTH), lambda o: (o, 0, 0)),
            pl.BlockSpec((1, 1, 2 * HY_WIDTH), lambda o: (o, 0, 0)),
            full((2, H2)), full((L, 2 * HY_WIDTH)),
        ],
        out_specs=pl.BlockSpec((2 * L, HY_WIDTH), lambda o: (0, o)),
        out_shape=jax.ShapeDtypeStruct((2 * L, HY_ORDER * HY_WIDTH), BF16),
        compiler_params=_cparams("parallel"),
        name="hyena_filter_mlp",
    )(z_pair, _block_diag2(pad_w1), two(b1.reshape(1, -1)), _block_diag2(w2),
      two(b2.reshape(1, -1)), w3bd, b3c, two(freq), win_pair)


def _spec_kernel(fm_ref, blk_ref, h_ref, prev_ref, *, n):
    P = fm_ref.shape[1]
    row = lax.broadcasted_iota(jnp.int32, (P, blk_ref.shape[1]), 0)
    sign = jnp.where(row % 2 == 0, 1.0, -1.0)
    scale = jnp.where(row == 0, 1.0 / n, 2.0 / n)
    @pl.when(pl.program_id(0) == 0)
    def _():
        prev_ref[...] = jnp.zeros_like(prev_ref)

    for half in range(2):
        phi = _dot(fm_ref[half], blk_ref[...])
        h_ref[0, half] = ((phi + sign * prev_ref[half]) * scale).astype(h_ref.dtype)
        prev_ref[half] = phi


def hyena_spectra(fm, f_all):
    P = fm.shape[1]
    W = f_all.shape[1]
    n_blk = f_all.shape[0] // P
    return pl.pallas_call(
        functools.partial(_spec_kernel, n=2 * P),
        grid=(n_blk,),
        in_specs=[
            pl.BlockSpec((2, P, P), lambda e: (0, 0, 0)),
            pl.BlockSpec((P, W), lambda e: (e, 0)),
        ],
        out_specs=pl.BlockSpec((1, 2, P, W), lambda e: (jnp.maximum(e - 1, 0), 0, 0, 0)),
        out_shape=jax.ShapeDtypeStruct((n_blk - 1, 2, P, W), BF16),
        scratch_shapes=[pltpu.VMEM((2, P, W), F32)],
        compiler_params=_cparams("arbitrary"),
        name="hyena_filter_spectrum",
    )(fm, f_all)


HY_BLOCK = 512
HY_CONV_ROWS = 16


def _lconv_kernel(fm_ref, g_ref, u_ref, gate_ref, skip_ref, h_ref, o_ref, x_ref, yt_ref, yb_ref):
    P = fm_ref.shape[1]
    nb = u_ref.shape[1] // P
    ct = u_ref.shape[2]
    for j in range(nb):
        uj = u_ref[0, j * P:(j + 1) * P, :]
        x_ref[j, 0] = _dot(fm_ref[0], uj).astype(BF16)
        x_ref[j, 1] = _dot(fm_ref[1], uj).astype(BF16)
    first = lax.broadcasted_iota(jnp.int32, (HY_CONV_ROWS, ct), 0) == 0
    zero = jnp.zeros((HY_CONV_ROWS, ct), BF16)
    for i in range(nb):
        for r in range(P // HY_CONV_ROWS):
            rows = slice(r * HY_CONV_ROWS, (r + 1) * HY_CONV_ROWS)
            at, ab = zero, zero
            for j in range(nb):
                d = i - j + nb - 1
                xt, xb = x_ref[j, 0, rows, :], x_ref[j, 1, rows, :]
                ht, hb = h_ref[d, 0, rows, :], h_ref[d, 1, rows, :]
                bb = xb * hb
                if r == 0:
                    at += xt * ht - jnp.where(first, zero, bb)
                    ab += jnp.where(first, bb, xt * hb + xb * ht)
                else:
                    at += xt * ht - bb
                    ab += xt * hb + xb * ht
            yt_ref[i, rows, :] = at
            yb_ref[i, rows, :] = ab
        y = _dot(g_ref[:, :P], yt_ref[i]) + _dot(g_ref[:, P:], yb_ref[i])
        blk = slice(i * P, (i + 1) * P)
        ui = u_ref[0, blk, :].astype(F32)
        o_ref[0, blk, :] = ((y + ui * skip_ref[...]) * gate_ref[0, blk, :].astype(F32)
                            ).astype(o_ref.dtype)


def long_conv(fm, g, spec, order, u, u_col, gate, gate_col, skip, ct=256):
    B, L, _ = u.shape
    P = fm.shape[1]
    nb = L // P
    n_lag = spec.shape[0]
    per = HY_WIDTH // ct
    return pl.pallas_call(
        _lconv_kernel,
        grid=(per, B),
        in_specs=[
            pl.BlockSpec((2, P, P), lambda c, b: (0, 0, 0)),
            pl.BlockSpec((P, 2 * P), lambda c, b: (0, 0)),
            pl.BlockSpec((1, L, ct), lambda c, b: (b, 0, u_col * per + c)),
            pl.BlockSpec((1, L, ct), lambda c, b: (b, 0, gate_col * per + c)),
            pl.BlockSpec((1, ct), lambda c, b: (0, c)),
            pl.BlockSpec((n_lag, 2, P, ct), lambda c, b: (0, 0, 0, order * per + c)),
        ],
        out_specs=pl.BlockSpec((1, L, ct), lambda c, b: (b, 0, c)),
        out_shape=jax.ShapeDtypeStruct((B, L, HY_WIDTH), BF16),
        scratch_shapes=[pltpu.VMEM((nb, 2, P, ct), BF16), pltpu.VMEM((nb, P, ct), BF16),
                        pltpu.VMEM((nb, P, ct), BF16)],
        compiler_params=_cparams("parallel", "parallel"),
        name="hyena_long_conv",
    )(fm, g, u, gate, skip.reshape(1, HY_WIDTH).astype(F32), spec)


def hyena_mixer(p3, conv_w, spec, skip, fm, g):
    uc = short_conv(p3, conv_w.astype(F32))
    z = long_conv(fm, g, spec, 0, uc, 0, uc, 1, skip[0])
    return long_conv(fm, g, spec, 1, z, 0, uc, 2, skip[1])


def _log_sigmoid(x):
    return -(jnp.maximum(-x, 0.0) + jnp.log(1.0 + jnp.exp(-jnp.abs(x))))


def _dot_hilo(m, x):
    hi = x.astype(BF16)
    lo = (x - hi.astype(F32)).astype(BF16)
    return _dot(m, hi) + _dot(m, lo)


GLA_PREP_ROWS = 256


def _gla_kernel(q_ref, k_ref, v_ref, og_ref, lr_ref, wlr_ref, blr_ref, on_ref, o_ref,
                qd_ref, kit_ref, kst_ref, dcol_ref, acc_ref):
    L = q_ref.shape[1]
    C = GLA_CHUNK
    H = GLA_HEADS
    RB = GLA_PREP_ROWS
    n_chunks = L // C
    KW, VW, DV = GLA_K_W, GLA_V_W, GLA_DV

    pr = lax.broadcasted_iota(jnp.int32, (RB, RB), 0)
    pc = lax.broadcasted_iota(jnp.int32, (RB, RB), 1)
    same = (pr // C) == (pc // C)
    cum_f = jnp.where(same & (pc <= pr), 1.0, 0.0).astype(BF16)
    cum_b = jnp.where(same & (pc >= pr), 1.0, 0.0).astype(BF16)
    tot_m = jnp.where(same, 1.0, 0.0).astype(BF16)
    PW = 2 * C
    n_pairs = L // PW

    def prep(i, carry):
        r0 = pl.multiple_of(i * RB, RB)
        logit = _dot(lr_ref[0, pl.ds(r0, RB), :], wlr_ref[...]) + blr_ref[...]
        g = _log_sigmoid(logit) * (1.0 / GLA_NORMALIZER)
        q = q_ref[0, pl.ds(r0, RB), :].astype(F32) * (GLA_DK ** -0.5)
        k = k_ref[0, pl.ds(r0, RB), :].astype(F32)
        for d, cum in ((0, cum_f), (1, cum_b)):
            gd = g[:, d * KW:(d + 1) * KW]
            b = _dot_hilo(cum, gd)
            tot = _dot_hilo(tot_m, gd)
            qd_ref[d, pl.ds(r0, RB), :] = (q * jnp.exp(b)).astype(BF16)
            ki_t = (k * jnp.exp(-b)).T
            ks_t = (k * jnp.exp(tot - b)).T
            dec_t = jnp.exp(tot).T
            for p in range(RB // PW):
                rows = pl.ds(pl.multiple_of((i * (RB // PW) + p) * KW, KW), KW)
                cols = slice(p * PW, (p + 1) * PW)
                kit_ref[d, rows, :] = ki_t[:, cols].astype(BF16)
                kst_ref[d, rows, :] = ks_t[:, cols].astype(BF16)
                dcol_ref[d, rows, :] = dec_t[:, cols]
        return carry

    lax.fori_loop(0, L // RB, prep, 0, unroll=2)

    kk_blk = (lax.broadcasted_iota(jnp.int32, (KW, H * C), 0) // GLA_DK
              == lax.broadcasted_iota(jnp.int32, (KW, H * C), 1) // C)
    kv_blk = (lax.broadcasted_iota(jnp.int32, (KW, VW), 0) // GLA_DK
              == lax.broadcasted_iota(jnp.int32, (KW, VW), 1) // DV)
    arow = lax.broadcasted_iota(jnp.int32, (C, H * C), 0)
    acol = lax.broadcasted_iota(jnp.int32, (C, H * C), 1) % C
    amasks = (acol <= arow, acol > arow)
    zero_b = jnp.zeros((), BF16)

    def body(ip, sts):
        new = []
        for d in (0, 1):
            st = sts[d]
            pair = ip if d == 0 else n_pairs - 1 - ip
            rk = pl.multiple_of(pair * KW, KW)
            kit2 = kit_ref[d, pl.ds(rk, KW), :]
            kst2 = kst_ref[d, pl.ds(rk, KW), :]
            dec2 = dcol_ref[d, pl.ds(rk, KW), :]
            for half in ((0, 1) if d == 0 else (1, 0)):
                r0 = pl.multiple_of(pair * PW + half * C, C)
                cols = slice(half * C, (half + 1) * C)
                qd = qd_ref[d, pl.ds(r0, C), :]
                v = v_ref[0, pl.ds(r0, C), :]
                kk = jnp.where(kk_blk, jnp.concatenate([kit2[:, cols]] * H, axis=1), zero_b)
                a = jnp.where(amasks[d], _dot(qd, kk), 0.0).astype(BF16)
                v_bd = jnp.where(kv_blk, jnp.concatenate([v] * H, axis=0), zero_b)
                st_b = st.astype(BF16)
                s_bd = jnp.where(kv_blk, jnp.concatenate([st_b] * H, axis=1), zero_b)
                acc_ref[d, pl.ds(r0, C), :] = _dot(a, v_bd) + _dot(qd, s_bd)
                ds = jnp.concatenate(
                    [_dot(kst2[h * GLA_DK:(h + 1) * GLA_DK, cols], v[:, h * DV:(h + 1) * DV])
                     for h in range(H)], axis=0)
                st = st * jnp.concatenate([dec2[:, cols]] * (DV // C), axis=1) + ds
            new.append(st)
        return tuple(new)

    zero = jnp.zeros((KW, DV), F32)
    lax.fori_loop(0, n_pairs, body, (zero, zero), unroll=4)

    def finalize(i, carry):
        r0 = pl.multiple_of(i * RB, RB)
        o = acc_ref[0, pl.ds(r0, RB), :] + acc_ref[1, pl.ds(r0, RB), :]
        og = og_ref[0, pl.ds(r0, RB), :].astype(F32)
        outs = []
        for h in range(H):
            oh = o[:, h * GLA_DV:(h + 1) * GLA_DV]
            ms = jnp.mean(oh * oh, axis=-1, keepdims=True)
            outs.append(oh * lax.rsqrt(ms + EPS) * on_ref[...])
        y = jnp.concatenate(outs, axis=-1) * (og * _sigmoid(og))
        o_ref[0, pl.ds(r0, RB), :] = y.astype(o_ref.dtype)
        return carry

    lax.fori_loop(0, L // RB, finalize, 0, unroll=2)


def gla_mixer(p3, w_lr, b_lr, onorm):
    B, L, _ = p3.shape
    lr_w = 256
    wl = jnp.zeros((lr_w, 2 * GLA_K_W), F32)
    wl = wl.at[0:GLA_RANK, 0:GLA_K_W].set(w_lr[0].astype(F32))
    wl = wl.at[GLA_RANK:2 * GLA_RANK, GLA_K_W:].set(w_lr[1].astype(F32))
    bl = b_lr.astype(F32).reshape(1, 2 * GLA_K_W)
    col = lambda width, c: pl.BlockSpec((1, L, width), lambda b: (b, 0, c // width))
    return pl.pallas_call(
        _gla_kernel,
        grid=(B,),
        in_specs=[
            col(GLA_K_W, COL_GQ), col(GLA_K_W, COL_GK), col(GLA_V_W, COL_GV),
            col(GLA_V_W, COL_GO), col(lr_w, COL_LR),
            pl.BlockSpec((lr_w, 2 * GLA_K_W), lambda b: (0, 0)),
            pl.BlockSpec((1, 2 * GLA_K_W), lambda b: (0, 0)),
            pl.BlockSpec((1, GLA_DV), lambda b: (0, 0)),
        ],
        out_specs=pl.BlockSpec((1, L, GLA_V_W), lambda b: (b, 0, 0)),
        out_shape=jax.ShapeDtypeStruct((B, L, GLA_V_W), BF16),
        scratch_shapes=[
            pltpu.VMEM((2, L, GLA_K_W), BF16),
            pltpu.VMEM((2, L // (2 * GLA_CHUNK) * GLA_K_W, 2 * GLA_CHUNK), BF16),
            pltpu.VMEM((2, L // (2 * GLA_CHUNK) * GLA_K_W, 2 * GLA_CHUNK), BF16),
            pltpu.VMEM((2, L // (2 * GLA_CHUNK) * GLA_K_W, 2 * GLA_CHUNK), F32),
            pltpu.VMEM((2, L, GLA_V_W), F32),
        ],
        compiler_params=_cparams("parallel"),
        name="gla_mixer",
    )(p3, p3, p3, p3, p3, wl.astype(BF16), bl, onorm.astype(F32).reshape(1, GLA_DV))


def _merge_kernel(a_ref, b_ref, c_ref, ga_ref, gb_ref, gc_ref, h_ref, wa_ref, wb_ref, wc_ref,
                  wo_ref, o_ref):
    mixed = (_sigmoid(ga_ref[...].astype(F32)) * _dot(a_ref[...], wa_ref[...])
             + _sigmoid(gb_ref[...].astype(F32)) * _dot(b_ref[...], wb_ref[...])
             + _sigmoid(gc_ref[...].astype(F32)) * _dot(c_ref[...], wc_ref[...]))
    o_ref[...] = h_ref[...] + _dot(mixed.astype(BF16), wo_ref[...])


def merge_out(ya, yb, yc, p2, h, wa, wb, wc, wo, tm=512):
    M, D = h.shape
    tm = _tile(M, tm)
    br = lambda w: pl.BlockSpec((tm, w), lambda i: (i, 0))
    gate = lambda c: pl.BlockSpec((tm, D), lambda i: (i, COL_GATE // D + c))
    wfull = lambda r: pl.BlockSpec((r, D), lambda i: (0, 0))
    return pl.pallas_call(
        _merge_kernel,
        grid=(M // tm,),
        in_specs=[br(ATT_Q_W), br(HY_WIDTH), br(GLA_V_W), gate(0), gate(1), gate(2),
                  pl.BlockSpec((tm, D), lambda i: (i, 0)),
                  wfull(ATT_Q_W), wfull(HY_WIDTH), wfull(GLA_V_W), wfull(D)],
        out_specs=pl.BlockSpec((tm, D), lambda i: (i, 0)),
        out_shape=jax.ShapeDtypeStruct((M, D), F32),
        compiler_params=_cparams("parallel"),
        name="gated_merge_out_proj",
    )(ya, yb, yc, p2, p2, p2, h, wa, wb, wc, wo)


def _head_rmsnorm(x, g):
    ms = jnp.mean(x * x, axis=-1, keepdims=True)
    return x * lax.rsqrt(ms + EPS) * g


def _xattn_kernel(h_ref, ln_ref, wq_ref, k_ref, v_ref, qn_ref, kn_ref, wo_ref, o_ref, att_ref):
    h = h_ref[0]
    hn = _head_rmsnorm(h, ln_ref[...]).astype(BF16)
    q = _dot(hn, wq_ref[...])
    for hd in range(X_HEADS):
        sl = slice(hd * X_HEAD_DIM, (hd + 1) * X_HEAD_DIM)
        qh = (_head_rmsnorm(q[:, sl], qn_ref[...]) * (X_HEAD_DIM ** -0.5)).astype(BF16)
        kh = _head_rmsnorm(k_ref[0, :, sl].astype(F32), kn_ref[...]).astype(BF16)
        s = _dot_nt(qh, kh)
        p = jnp.exp(s - jnp.max(s, axis=-1, keepdims=True))
        l = jnp.sum(p, axis=-1, keepdims=True)
        att_ref[:, sl] = (_dot(p.astype(BF16), v_ref[0, :, sl]) / l).astype(BF16)
    o_ref[0] = h + _dot(att_ref[...], wo_ref[...])


def cross_attention(h3, ln, wq, kv3, qn, kn, wo, tl=1024):
    B, L, D = h3.shape
    Mm = kv3.shape[1]
    tl = _tile(L, tl)
    vec = lambda w: pl.BlockSpec((1, w), lambda b, i: (0, 0))
    mat = pl.BlockSpec((D, D), lambda b, i: (0, 0))
    return pl.pallas_call(
        _xattn_kernel,
        grid=(B, L // tl),
        in_specs=[
            pl.BlockSpec((1, tl, D), lambda b, i: (b, i, 0)),
            vec(D), mat,
            pl.BlockSpec((1, Mm, D), lambda b, i: (b, 0, 0)),
            pl.BlockSpec((1, Mm, D), lambda b, i: (b, 0, 1)),
            vec(X_HEAD_DIM), vec(X_HEAD_DIM), mat,
        ],
        out_specs=pl.BlockSpec((1, tl, D), lambda b, i: (b, i, 0)),
        out_shape=jax.ShapeDtypeStruct((B, L, D), F32),
        scratch_shapes=[pltpu.VMEM((tl, D), BF16)],
        compiler_params=_cparams("parallel", "parallel"),
        name="memory_cross_attention",
    )(h3, ln.astype(F32).reshape(1, D), wq, kv3, kv3, qn.astype(F32).reshape(1, X_HEAD_DIM),
      kn.astype(F32).reshape(1, X_HEAD_DIM), wo)


def _mlp_kernel(h_ref, g_ref, w1_ref, w2_ref, o_ref, hn_ref, acc_ref):
    j = pl.program_id(1)

    @pl.when(j == 0)
    def _():
        hn_ref[...] = _head_rmsnorm(h_ref[...], g_ref[...]).astype(BF16)
        acc_ref[...] = jnp.zeros_like(acc_ref)

    a = jnp.maximum(_dot(hn_ref[...], w1_ref[...]), 0.0)
    acc_ref[...] += _dot((a * a).astype(BF16), w2_ref[...])

    @pl.when(j == pl.num_programs(1) - 1)
    def _():
        o_ref[...] = h_ref[...] + acc_ref[...]


def mlp(h, g, w1, w2, tm=1024, tf=2048):
    M, D = h.shape
    F = w1.shape[1]
    tm, tf = _tile(M, tm), _tile(F, tf)
    return pl.pallas_call(
        _mlp_kernel,
        grid=(M // tm, F // tf),
        in_specs=[
            pl.BlockSpec((tm, D), lambda i, j: (i, 0)),
            pl.BlockSpec((1, D), lambda i, j: (0, 0)),
            pl.BlockSpec((D, tf), lambda i, j: (0, j)),
            pl.BlockSpec((tf, D), lambda i, j: (j, 0)),
        ],
        out_specs=pl.BlockSpec((tm, D), lambda i, j: (i, 0)),
        out_shape=jax.ShapeDtypeStruct((M, D), F32),
        scratch_shapes=[pltpu.VMEM((tm, D), BF16), pltpu.VMEM((tm, D), F32)],
        compiler_params=_cparams("parallel", "arbitrary"),
        name="relu2_mlp",
    )(h, g.astype(F32).reshape(1, D), w1, w2)


def _pack_w_in(w):
    kv0 = ATT_Q_W
    hy0 = kv0 + 2 * ATT_KV_W
    lr0 = hy0 + 3 * HY_WIDTH + 2 * GLA_K_W + 2 * GLA_V_W
    g0 = lr0 + 2 * GLA_RANK
    assert lr0 - hy0 == COL_AK - COL_HY and w.shape[2] - g0 == N_PACK - COL_GATE
    packed = jnp.zeros(w.shape[:2] + (N_PACK,), BF16)
    for src0, src1, dst in ((0, kv0, COL_AQ), (kv0, hy0, COL_AK), (hy0, lr0, COL_HY),
                            (lr0, g0, COL_LR), (g0, w.shape[2], COL_GATE)):
        packed = lax.dynamic_update_slice(packed, w[:, :, src0:src1].astype(BF16), (0, 0, dst))
    return packed


def kernel(x, mem, ln_mix, w_in, attn_qnorm, attn_knorm, hy_conv, hy_w1, hy_b1, hy_w2, hy_b2,
           hy_w3, hy_b3, hy_freq, hy_skip, gla_w_lr, gla_b_lr, gla_onorm, w_br_attn, w_br_hyena,
           w_br_gla, w_out, ln_x, ln_mem, x_wq, x_wk, x_wv, x_wo, x_qnorm, x_knorm, ln_mlp,
           mlp_w1, mlp_w2):
    B, L, D = x.shape
    Mm = mem.shape[1]
    depth = w_in.shape[0]
    M = B * L

    rope_tabs = _rope_tables(L)
    m = jnp.arange(L)
    z_neg, win_neg = _hyena_pos_features(L, (L - m).astype(F32))
    z_fwd, win_fwd = _hyena_pos_features(L, m.astype(F32))
    win_neg = jnp.where((m == 0)[:, None], 0.0, win_neg)
    zpad = lambda z: jnp.pad(z, ((0, 0), (0, HY_FFN - HY_EMB)))
    z_pair = jnp.concatenate([zpad(z_neg), zpad(z_fwd)], axis=1)
    win_pair = jnp.concatenate([win_neg, win_fwd], axis=1)
    fm, g_inv = _dft_tables(min(L, HY_BLOCK))
    bf = lambda a: a.astype(BF16)

    w_in_packed = _pack_w_in(w_in)
    wkv_all = jnp.concatenate([bf(x_wk), bf(x_wv)], axis=2)
    h = x.astype(F32).reshape(M, D)
    mem2 = mem.astype(F32).reshape(B * Mm, D)
    for i in range(depth):
        p2 = norm_matmul(h, ln_mix[i].astype(F32), w_in_packed, i, BF16)
        p3 = p2.reshape(B, L, N_PACK)
        y_a = attention(p3, attn_qnorm[i], attn_knorm[i], rope_tabs)
        f_all = hyena_filters(z_pair, win_pair, hy_w1[i].astype(F32), hy_b1[i].astype(F32),
                              hy_w2[i].astype(F32), hy_b2[i].astype(F32), hy_w3[i].astype(F32),
                              hy_b3[i].astype(F32), hy_freq[i].astype(F32))
        spec = hyena_spectra(fm, f_all)
        y_b = hyena_mixer(p3, hy_conv[i], spec, hy_skip[i], fm, g_inv)
        y_c = gla_mixer(p3, gla_w_lr[i], gla_b_lr[i], gla_onorm[i])
        h = merge_out(y_a.reshape(M, ATT_Q_W), y_b.reshape(M, HY_WIDTH), y_c.reshape(M, GLA_V_W),
                      p2, h, bf(w_br_attn[i]), bf(w_br_hyena[i]), bf(w_br_gla[i]), bf(w_out[i]))
        kv = norm_matmul(mem2, ln_mem[i].astype(F32), wkv_all, i, BF16).reshape(B, Mm, 2 * D)
        h = cross_attention(h.reshape(B, L, D), ln_x[i], bf(x_wq[i]), kv, x_qnorm[i],
                            x_knorm[i], bf(x_wo[i])).reshape(M, D)
        h = mlp(h, ln_mlp[i], bf(mlp_w1[i]), bf(mlp_w2[i]))
    return h.reshape(B, L, D).astype(x.dtype)
```

```python
import functools
import math

import jax
import jax.numpy as jnp
from jax import lax
from jax.experimental import pallas as pl
from jax.experimental.pallas import tpu as pltpu

F32 = jnp.float32
BF16 = jnp.bfloat16

D_MODEL = 1024
GRID_W = 64
ROPE_THETA = 10000.0
HEAD_DIM = 64
ATT_Q_HEADS = 8
ATT_KV_HEADS = 2
ATT_Q_W = ATT_Q_HEADS * HEAD_DIM
ATT_KV_W = ATT_KV_HEADS * HEAD_DIM
HY_WIDTH = 512
HY_ORDER = 2
HY_BANDS = 16
HY_EMB = 1 + 2 * HY_BANDS
HY_FFN = 64
HY_FAST_DECAY = 0.3
HY_SLOW_DECAY = 1.5
HY_TARGET = 1e-2
GLA_HEADS = 4
GLA_DK = 64
GLA_DV = 128
GLA_RANK = 16
GLA_NORMALIZER = 16.0
GLA_CHUNK = 64
GLA_K_W = GLA_HEADS * GLA_DK
GLA_V_W = GLA_HEADS * GLA_DV
X_HEADS = 4
X_HEAD_DIM = D_MODEL // X_HEADS
D_FF = 4 * D_MODEL
N_BRANCH = 3
EPS = 1e-6

COL_AQ = 0
COL_HY = 512
COL_GQ = 2048
COL_GK = 2304
COL_GV = 2560
COL_GO = 3072
COL_AK = 3584
COL_AV = 3712
COL_LR = 3840
COL_GATE = 4096
N_PACK = 7168

VMEM_LIMIT_BYTES = 52 * 1024 * 1024


def _cparams(*sem):
    return pltpu.CompilerParams(dimension_semantics=sem, vmem_limit_bytes=VMEM_LIMIT_BYTES)


def _tile(n, t):
    t = min(n, t)
    assert n % t == 0, (n, t)
    return t


def _dot(a, b):
    return jnp.dot(a, b, preferred_element_type=F32)


def _dot_nt(a, b):
    return lax.dot_general(a, b, (((1,), (1,)), ((), ())), preferred_element_type=F32)


def _dot_tn(a, b):
    return lax.dot_general(a, b, (((0,), (0,)), ((), ())), preferred_element_type=F32)


def _sigmoid(x):
    return 1.0 / (1.0 + jnp.exp(-x))


def _norm_mm_kernel(x_ref, g_ref, w_ref, o_ref, xn_ref):
    @pl.when(pl.program_id(1) == 0)
    def _():
        x = x_ref[...]
        ms = jnp.mean(x * x, axis=-1, keepdims=True)
        xn_ref[...] = (x * lax.rsqrt(ms + EPS) * g_ref[...]).astype(BF16)

    o_ref[...] = _dot(xn_ref[...], w_ref[...]).astype(o_ref.dtype)


def norm_matmul(x, g, w, layer, out_dtype, tm=1024, tn=3584):
    M, K = x.shape
    N = w.shape[2]
    tm, tn = _tile(M, tm), _tile(N, tn)
    return pl.pallas_call(
        _norm_mm_kernel,
        grid=(M // tm, N // tn),
        in_specs=[
            pl.BlockSpec((tm, K), lambda i, j: (i, 0)),
            pl.BlockSpec((1, K), lambda i, j: (0, 0)),
            pl.BlockSpec((None, K, tn), lambda i, j: (layer, 0, j)),
        ],
        out_specs=pl.BlockSpec((tm, tn), lambda i, j: (i, j)),
        out_shape=jax.ShapeDtypeStruct((M, N), out_dtype),
        scratch_shapes=[pltpu.VMEM((tm, K), BF16)],
        compiler_params=_cparams("parallel", "arbitrary"),
        name="norm_matmul",
    )(x, g.reshape(1, K), w)


def _group_mean_sq(x, gm_ref):
    return _dot((x * x).astype(BF16), gm_ref[...])


def _rope(x, c_ref, sa_ref, sb_ref):
    w = x.shape[-1]
    return (x * c_ref[...] + pltpu.roll(x, w - HEAD_DIM // 4, 1) * sa_ref[...]
            + pltpu.roll(x, HEAD_DIM // 4, 1) * sb_ref[...])


def _attn_kernel(q_ref, k_ref, v_ref, qn_ref, kn_ref, cq_ref, saq_ref, sbq_ref,
                 ck_ref, sak_ref, sbk_ref, gmq_ref, gmk_ref, o_ref, kp_ref, vx_ref):
    @pl.when(pl.program_id(1) == 0)
    def _():
        k = k_ref[0].astype(F32)
        kh = k * lax.rsqrt(_group_mean_sq(k, gmk_ref) + EPS) * kn_ref[...]
        kp_ref[...] = _rope(kh, ck_ref, sak_ref, sbk_ref).astype(BF16)
        v = v_ref[0]
        ones = jnp.ones((v.shape[0], HEAD_DIM), BF16)
        vx_ref[...] = jnp.concatenate(
            [piece for kv in range(ATT_KV_HEADS)
             for piece in (v[:, kv * HEAD_DIM:(kv + 1) * HEAD_DIM], ones)], axis=-1)

    q = q_ref[0].astype(F32)
    qh = q * lax.rsqrt(_group_mean_sq(q, gmq_ref) + EPS) * qn_ref[...]
    qb = (_rope(qh, cq_ref, saq_ref, sbq_ref) * (HEAD_DIM ** -0.5 * math.log2(math.e))).astype(BF16)
    group = ATT_Q_HEADS // ATT_KV_HEADS
    outs = []
    for h in range(ATT_Q_HEADS):
        kv = h // group
        kk = kp_ref[:, kv * HEAD_DIM:(kv + 1) * HEAD_DIM]
        s = _dot_nt(qb[:, h * HEAD_DIM:(h + 1) * HEAD_DIM], kk)
        p = jnp.exp2(s - jnp.max(s, axis=-1, keepdims=True))
        ox = _dot(p.astype(BF16), vx_ref[:, kv * 2 * HEAD_DIM:(kv + 1) * 2 * HEAD_DIM])
        outs.append(ox[:, :HEAD_DIM] / ox[:, HEAD_DIM:])
    o_ref[0] = jnp.concatenate(outs, axis=-1).astype(o_ref.dtype)


def _rope_tables(L):
    rows = L // GRID_W
    r, c = jnp.meshgrid(jnp.arange(rows), jnp.arange(GRID_W), indexing="ij")
    n_freq = HEAD_DIM // 4
    inv = ROPE_THETA ** (-jnp.arange(n_freq, dtype=F32) / n_freq)
    pos = jnp.stack([r.reshape(-1), c.reshape(-1)], axis=1).astype(F32)
    ang = pos[:, :, None] * inv
    cos, sin = jnp.cos(ang), jnp.sin(ang)
    zero = jnp.zeros_like(sin)
    c64 = jnp.concatenate([cos, cos], axis=-1).reshape(L, HEAD_DIM)
    sa64 = jnp.concatenate([-sin, zero], axis=-1).reshape(L, HEAD_DIM)
    sb64 = jnp.concatenate([zero, sin], axis=-1).reshape(L, HEAD_DIM)
    return c64, sa64, sb64


def attention(p3, qn, kn, tabs, tq=1024):
    B, L, _ = p3.shape
    tq = _tile(L, tq)
    c64, sa64, sb64 = tabs
    tq_tabs = [jnp.tile(t, (1, ATT_Q_HEADS)) for t in (c64, sa64, sb64)]
    tk_tabs = [jnp.tile(t, (1, ATT_KV_HEADS)) for t in (c64, sa64, sb64)]
    qn_t = jnp.tile(qn.astype(F32), ATT_Q_HEADS).reshape(1, ATT_Q_W)
    kn_t = jnp.tile(kn.astype(F32), ATT_KV_HEADS).reshape(1, ATT_KV_W)

    def group_mean(width):
        g = jnp.arange(width) // HEAD_DIM
        return jnp.where(g[:, None] == g[None, :], 1.0 / HEAD_DIM, 0.0).astype(BF16)

    qtab = pl.BlockSpec((tq, ATT_Q_W), lambda b, i: (i, 0))
    ktab = pl.BlockSpec((L, ATT_KV_W), lambda b, i: (0, 0))
    return pl.pallas_call(
        _attn_kernel,
        grid=(B, L // tq),
        in_specs=[
            pl.BlockSpec((1, tq, ATT_Q_W), lambda b, i: (b, i, COL_AQ // ATT_Q_W)),
            pl.BlockSpec((1, L, ATT_KV_W), lambda b, i: (b, 0, COL_AK // ATT_KV_W)),
            pl.BlockSpec((1, L, ATT_KV_W), lambda b, i: (b, 0, COL_AV // ATT_KV_W)),
            pl.BlockSpec((1, ATT_Q_W), lambda b, i: (0, 0)),
            pl.BlockSpec((1, ATT_KV_W), lambda b, i: (0, 0)),
            qtab, qtab, qtab, ktab, ktab, ktab,
            pl.BlockSpec((ATT_Q_W, ATT_Q_W), lambda b, i: (0, 0)),
            pl.BlockSpec((ATT_KV_W, ATT_KV_W), lambda b, i: (0, 0)),
        ],
        out_specs=pl.BlockSpec((1, tq, ATT_Q_W), lambda b, i: (b, i, 0)),
        out_shape=jax.ShapeDtypeStruct((B, L, ATT_Q_W), BF16),
        scratch_shapes=[pltpu.VMEM((L, ATT_KV_W), BF16), pltpu.VMEM((L, 2 * ATT_KV_W), BF16)],
        compiler_params=_cparams("parallel", "arbitrary"),
        name="gqa_attention",
    )(p3, p3, p3, qn_t, kn_t, *tq_tabs, *tk_tabs, group_mean(ATT_Q_W), group_mean(ATT_KV_W))


def _dft_tables(L):
    n = 2 * L
    k = jnp.arange(L, dtype=jnp.int32)[:, None]
    j = jnp.arange(L, dtype=jnp.int32)[None, :]
    step = 64
    ja = jnp.arange(0, L, step, dtype=jnp.int32)[None, :]
    jb = jnp.arange(step, dtype=jnp.int32)[None, :]
    ang_a = ((k * ja) % n).astype(F32) * (2.0 * math.pi / n)
    ang_b = ((k * jb) % n).astype(F32) * (2.0 * math.pi / n)
    ca, sa = jnp.cos(ang_a)[:, :, None], jnp.sin(ang_a)[:, :, None]
    cb, sb = jnp.cos(ang_b)[:, None, :], jnp.sin(ang_b)[:, None, :]
    c = (ca * cb - sa * sb).reshape(L, L)
    s = -(sa * cb + ca * sb).reshape(L, L)
    bottom = jnp.where(k == 0, jnp.where(j % 2 == 0, 1.0, -1.0), s)
    bottom_t = jnp.where(j == 0, jnp.where(k % 2 == 0, 1.0, -1.0), s)
    fm = jnp.stack([c, bottom]).astype(BF16)
    g = jnp.concatenate([c, bottom_t], axis=1).astype(BF16)
    return fm, g


def _hyena_pos_features(L, t):
    t_norm = t / max(L - 1, 1)
    w = 2.0 * math.pi * t / L
    f = jnp.linspace(1e-4, HY_BANDS - 1, HY_BANDS, dtype=F32)
    fw = w[:, None] * f
    z = jnp.concatenate([t_norm[:, None], jnp.cos(fw), -jnp.sin(fw)], axis=-1)
    deltas = jnp.abs(jnp.linspace(math.log(HY_TARGET) / HY_FAST_DECAY,
                                  math.log(HY_TARGET) / HY_SLOW_DECAY, HY_WIDTH, dtype=F32))
    window = jnp.exp(-t_norm[:, None] * deltas)
    return z, window


def _sconv_kernel(u_ref, w_ref, o_ref):
    u = u_ref[0].astype(F32)
    L = u.shape[0]
    row = lax.broadcasted_iota(jnp.int32, u.shape, 0)
    prev = jnp.where(row == 0, 0.0, pltpu.roll(u, 1, 0))
    nxt = jnp.where(row == L - 1, 0.0, pltpu.roll(u, L - 1, 0))
    o_ref[0] = (prev * w_ref[0:1, :] + u * w_ref[1:2, :] + nxt * w_ref[2:3, :]).astype(o_ref.dtype)


def short_conv(p3, w):
    B, L, _ = p3.shape
    nblk = 3
    return pl.pallas_call(
        _sconv_kernel,
        grid=(B, nblk),
        in_specs=[
            pl.BlockSpec((1, L, HY_WIDTH), lambda b, c: (b, 0, COL_HY // HY_WIDTH + c)),
            pl.BlockSpec((3, HY_WIDTH), lambda b, c: (0, c)),
        ],
        out_specs=pl.BlockSpec((1, L, HY_WIDTH), lambda b, c: (b, 0, c)),
        out_shape=jax.ShapeDtypeStruct((B, L, nblk * HY_WIDTH), BF16),
        compiler_params=_cparams("parallel", "parallel"),
        name="hyena_short_conv",
    )(p3, w)


def _hyfilter_kernel(z_ref, w1_ref, b1_ref, w2_ref, b2_ref, w3_ref, b3_ref, fr_ref, win_ref,
                     f_ref):
    L = z_ref.shape[0]
    hp = lax.Precision.HIGHEST
    h = jnp.sin(fr_ref[0:1, :] * (jnp.dot(z_ref[...], w1_ref[...], precision=hp,
                                          preferred_element_type=F32) + b1_ref[...]))
    h = jnp.sin(fr_ref[1:2, :] * (jnp.dot(h, w2_ref[...], precision=hp,
                                          preferred_element_type=F32) + b2_ref[...]))
    f = (_dot(h.astype(BF16), w3_ref[0].astype(BF16)) + b3_ref[0]) * win_ref[...]
    col = jnp.sum(jnp.abs(f), axis=0, keepdims=True)
    norm = col[:, :HY_WIDTH] + col[:, HY_WIDTH:] + EPS
    f_ref[:L, :] = (f[:, :HY_WIDTH] / norm).astype(f_ref.dtype)
    f_ref[L:, :] = (f[:, HY_WIDTH:] / norm).astype(f_ref.dtype)


def _block_diag2(w):
    z = jnp.zeros_like(w)
    return jnp.concatenate([jnp.concatenate([w, z], axis=1), jnp.concatenate([z, w], axis=1)], axis=0)


def hyena_filters(z_pair, win_pair, w1, b1, w2, b2, w3, b3, freq):
    L = z_pair.shape[0]
    pad_w1 = jnp.pad(w1, ((0, HY_FFN - HY_EMB), (0, 0)))
    two = lambda v: jnp.concatenate([v, v], axis=-1)
    w3r = w3.reshape(HY_FFN, HY_ORDER, 2, HY_WIDTH)
    zero = jnp.zeros((HY_ORDER, HY_FFN, HY_WIDTH), F32)
    w3n = jnp.moveaxis(w3r[:, :, 1], 1, 0)
    w3p = jnp.moveaxis(w3r[:, :, 0], 1, 0)
    w3bd = jnp.concatenate([jnp.concatenate([w3n, zero], axis=2),
                            jnp.concatenate([zero, w3p], axis=2)], axis=1)
    b3r = b3.reshape(HY_ORDER, 1, 2, HY_WIDTH)
    b3c = jnp.concatenate([b3r[:, :, 1], b3r[:, :, 0]], axis=-1)
    full = lambda shape: pl.BlockSpec(shape, lambda o: (0,) * len(shape))
    H2 = 2 * HY_FFN
    return pl.pallas_call(
        _hyfilter_kernel,
        grid=(HY_ORDER,),
        in_specs=[
            full((L, H2)), full((H2, H2)), full((1, H2)), full((H2, H2)), full((1, H2)),
            pl.BlockSpec((1, H2, 2 * HY_WIDTH), lambda o: (o, 0, 0)),
            pl.BlockSpec((1, 1, 2 * HY_WIDTH), lambda o: (o, 0, 0)),
            full((2, H2)), full((L, 2 * HY_WIDTH)),
        ],
        out_specs=pl.BlockSpec((2 * L, HY_WIDTH), lambda o: (0, o)),
        out_shape=jax.ShapeDtypeStruct((2 * L, HY_ORDER * HY_WIDTH), BF16),
        compiler_params=_cparams("parallel"),
        name="hyena_filter_mlp",
    )(z_pair, _block_diag2(pad_w1), two(b1.reshape(1, -1)), _block_diag2(w2),
      two(b2.reshape(1, -1)), w3bd, b3c, two(freq), win_pair)


def _spec_kernel(fm_ref, blk_ref, h_ref, prev_ref, *, n):
    P = fm_ref.shape[1]
    row = lax.broadcasted_iota(jnp.int32, (P, blk_ref.shape[1]), 0)
    sign = jnp.where(row % 2 == 0, 1.0, -1.0)
    scale = jnp.where(row == 0, 1.0 / n, 2.0 / n)
    @pl.when(pl.program_id(0) == 0)
    def _():
        prev_ref[...] = jnp.zeros_like(prev_ref)

    for half in range(2):
        phi = _dot(fm_ref[half], blk_ref[...])
        h_ref[0, half] = ((phi + sign * prev_ref[half]) * scale).astype(h_ref.dtype)
        prev_ref[half] = phi


def hyena_spectra(fm, f_all):
    P = fm.shape[1]
    W = f_all.shape[1]
    n_blk = f_all.shape[0] // P
    return pl.pallas_call(
        functools.partial(_spec_kernel, n=2 * P),
        grid=(n_blk,),
        in_specs=[
            pl.BlockSpec((2, P, P), lambda e: (0, 0, 0)),
            pl.BlockSpec((P, W), lambda e: (e, 0)),
        ],
        out_specs=pl.BlockSpec((1, 2, P, W), lambda e: (jnp.maximum(e - 1, 0), 0, 0, 0)),
        out_shape=jax.ShapeDtypeStruct((n_blk - 1, 2, P, W), BF16),
        scratch_shapes=[pltpu.VMEM((2, P, W), F32)],
        compiler_params=_cparams("arbitrary"),
        name="hyena_filter_spectrum",
    )(fm, f_all)


HY_BLOCK = 512
HY_CONV_ROWS = 16


def _lconv_kernel(fm_ref, g_ref, u_ref, gate_ref, skip_ref, h_ref, o_ref, x_ref, yt_ref, yb_ref):
    P = fm_ref.shape[1]
    nb = u_ref.shape[1] // P
    ct = u_ref.shape[2]
    for j in range(nb):
        uj = u_ref[0, j * P:(j + 1) * P, :]
        x_ref[j, 0] = _dot(fm_ref[0], uj).astype(BF16)
        x_ref[j, 1] = _dot(fm_ref[1], uj).astype(BF16)
    first = lax.broadcasted_iota(jnp.int32, (HY_CONV_ROWS, ct), 0) == 0
    zero = jnp.zeros((HY_CONV_ROWS, ct), BF16)
    for i in range(nb):
        for r in range(P // HY_CONV_ROWS):
            rows = slice(r * HY_CONV_ROWS, (r + 1) * HY_CONV_ROWS)
            at, ab = zero, zero
            for j in range(nb):
                d = i - j + nb - 1
                xt, xb = x_ref[j, 0, rows, :], x_ref[j, 1, rows, :]
                ht, hb = h_ref[d, 0, rows, :], h_ref[d, 1, rows, :]
                bb = xb * hb
                if r == 0:
                    at += xt * ht - jnp.where(first, zero, bb)
                    ab += jnp.where(first, bb, xt * hb + xb * ht)
                else:
                    at += xt * ht - bb
                    ab += xt * hb + xb * ht
            yt_ref[i, rows, :] = at
            yb_ref[i, rows, :] = ab
        y = _dot(g_ref[:, :P], yt_ref[i]) + _dot(g_ref[:, P:], yb_ref[i])
        blk = slice(i * P, (i + 1) * P)
        ui = u_ref[0, blk, :].astype(F32)
        o_ref[0, blk, :] = ((y + ui * skip_ref[...]) * gate_ref[0, blk, :].astype(F32)
                            ).astype(o_ref.dtype)


def long_conv(fm, g, spec, order, u, u_col, gate, gate_col, skip, ct=256):
    B, L, _ = u.shape
    P = fm.shape[1]
    nb = L // P
    n_lag = spec.shape[0]
    per = HY_WIDTH // ct
    return pl.pallas_call(
        _lconv_kernel,
        grid=(per, B),
        in_specs=[
            pl.BlockSpec((2, P, P), lambda c, b: (0, 0, 0)),
            pl.BlockSpec((P, 2 * P), lambda c, b: (0, 0)),
            pl.BlockSpec((1, L, ct), lambda c, b: (b, 0, u_col * per + c)),
            pl.BlockSpec((1, L, ct), lambda c, b: (b, 0, gate_col * per + c)),
            pl.BlockSpec((1, ct), lambda c, b: (0, c)),
            pl.BlockSpec((n_lag, 2, P, ct), lambda c, b: (0, 0, 0, order * per + c)),
        ],
        out_specs=pl.BlockSpec((1, L, ct), lambda c, b: (b, 0, c)),
        out_shape=jax.ShapeDtypeStruct((B, L, HY_WIDTH), BF16),
        scratch_shapes=[pltpu.VMEM((nb, 2, P, ct), BF16), pltpu.VMEM((nb, P, ct), BF16),
                        pltpu.VMEM((nb, P, ct), BF16)],
        compiler_params=_cparams("parallel", "parallel"),
        name="hyena_long_conv",
    )(fm, g, u, gate, skip.reshape(1, HY_WIDTH).astype(F32), spec)


def hyena_mixer(p3, conv_w, spec, skip, fm, g):
    uc = short_conv(p3, conv_w.astype(F32))
    z = long_conv(fm, g, spec, 0, uc, 0, uc, 1, skip[0])
    return long_conv(fm, g, spec, 1, z, 0, uc, 2, skip[1])


def _log_sigmoid(x):
    return -(jnp.maximum(-x, 0.0) + jnp.log(1.0 + jnp.exp(-jnp.abs(x))))


def _dot_hilo(m, x):
    hi = x.astype(BF16)
    lo = (x - hi.astype(F32)).astype(BF16)
    return _dot(m, hi) + _dot(m, lo)


GLA_PREP_ROWS = 256


def _gla_kernel(q_ref, k_ref, v_ref, og_ref, lr_ref, wlr_ref, blr_ref, on_ref, o_ref,
                qd_ref, kit_ref, kst_ref, dcol_ref, acc_ref):
    L = q_ref.shape[1]
    C = GLA_CHUNK
    H = GLA_HEADS
    RB = GLA_PREP_ROWS
    n_chunks = L // C
    KW, VW, DV = GLA_K_W, GLA_V_W, GLA_DV

    pr = lax.broadcasted_iota(jnp.int32, (RB, RB), 0)
    pc = lax.broadcasted_iota(jnp.int32, (RB, RB), 1)
    same = (pr // C) == (pc // C)
    cum_f = jnp.where(same & (pc <= pr), 1.0, 0.0).astype(BF16)
    cum_b = jnp.where(same & (pc >= pr), 1.0, 0.0).astype(BF16)
    tot_m = jnp.where(same, 1.0, 0.0).astype(BF16)
    PW = 2 * C
    n_pairs = L // PW

    def prep(i, carry):
        r0 = pl.multiple_of(i * RB, RB)
        logit = _dot(lr_ref[0, pl.ds(r0, RB), :], wlr_ref[...]) + blr_ref[...]
        g = _log_sigmoid(logit) * (1.0 / GLA_NORMALIZER)
        q = q_ref[0, pl.ds(r0, RB), :].astype(F32) * (GLA_DK ** -0.5)
        k = k_ref[0, pl.ds(r0, RB), :].astype(F32)
        for d, cum in ((0, cum_f), (1, cum_b)):
            gd = g[:, d * KW:(d + 1) * KW]
            b = _dot_hilo(cum, gd)
            tot = _dot_hilo(tot_m, gd)
            qd_ref[d, pl.ds(r0, RB), :] = (q * jnp.exp(b)).astype(BF16)
            ki_t = (k * jnp.exp(-b)).T
            ks_t = (k * jnp.exp(tot - b)).T
            dec_t = jnp.exp(tot).T
            for p in range(RB // PW):
                rows = pl.ds(pl.multiple_of((i * (RB // PW) + p) * KW, KW), KW)
                cols = slice(p * PW, (p + 1) * PW)
                kit_ref[d, rows, :] = ki_t[:, cols].astype(BF16)
                kst_ref[d, rows, :] = ks_t[:, cols].astype(BF16)
                dcol_ref[d, rows, :] = dec_t[:, cols]
        return carry

    lax.fori_loop(0, L // RB, prep, 0, unroll=2)

    kk_blk = (lax.broadcasted_iota(jnp.int32, (KW, H * C), 0) // GLA_DK
              == lax.broadcasted_iota(jnp.int32, (KW, H * C), 1) // C)
    kv_blk = (lax.broadcasted_iota(jnp.int32, (KW, VW), 0) // GLA_DK
              == lax.broadcasted_iota(jnp.int32, (KW, VW), 1) // DV)
    arow = lax.broadcasted_iota(jnp.int32, (C, H * C), 0)
    acol = lax.broadcasted_iota(jnp.int32, (C, H * C), 1) % C
    amasks = (acol <= arow, acol > arow)
    zero_b = jnp.zeros((), BF16)

    def body(ip, sts):
        new = []
        for d in (0, 1):
            st = sts[d]
            pair = ip if d == 0 else n_pairs - 1 - ip
            rk = pl.multiple_of(pair * KW, KW)
            kit2 = kit_ref[d, pl.ds(rk, KW), :]
            kst2 = kst_ref[d, pl.ds(rk, KW), :]
            dec2 = dcol_ref[d, pl.ds(rk, KW), :]
            for half in ((0, 1) if d == 0 else (1, 0)):
                r0 = pl.multiple_of(pair * PW + half * C, C)
                cols = slice(half * C, (half + 1) * C)
                qd = qd_ref[d, pl.ds(r0, C), :]
                v = v_ref[0, pl.ds(r0, C), :]
                kk = jnp.where(kk_blk, jnp.concatenate([kit2[:, cols]] * H, axis=1), zero_b)
                a = jnp.where(amasks[d], _dot(qd, kk), 0.0).astype(BF16)
                v_bd = jnp.where(kv_blk, jnp.concatenate([v] * H, axis=0), zero_b)
                st_b = st.astype(BF16)
                s_bd = jnp.where(kv_blk, jnp.concatenate([st_b] * H, axis=1), zero_b)
                acc_ref[d, pl.ds(r0, C), :] = _dot(a, v_bd) + _dot(qd, s_bd)
                ds = jnp.concatenate(
                    [_dot(kst2[h * GLA_DK:(h + 1) * GLA_DK, cols], v[:, h * DV:(h + 1) * DV])
                     for h in range(H)], axis=0)
                st = st * jnp.concatenate([dec2[:, cols]] * (DV // C), axis=1) + ds
            new.append(st)
        return tuple(new)

    zero = jnp.zeros((KW, DV), F32)
    lax.fori_loop(0, n_pairs, body, (zero, zero), unroll=4)

    def finalize(i, carry):
        r0 = pl.multiple_of(i * RB, RB)
        o = acc_ref[0, pl.ds(r0, RB), :] + acc_ref[1, pl.ds(r0, RB), :]
        og = og_ref[0, pl.ds(r0, RB), :].astype(F32)
        outs = []
        for h in range(H):
            oh = o[:, h * GLA_DV:(h + 1) * GLA_DV]
            ms = jnp.mean(oh * oh, axis=-1, keepdims=True)
            outs.append(oh * lax.rsqrt(ms + EPS) * on_ref[...])
        y = jnp.concatenate(outs, axis=-1) * (og * _sigmoid(og))
        o_ref[0, pl.ds(r0, RB), :] = y.astype(o_ref.dtype)
        return carry

    lax.fori_loop(0, L // RB, finalize, 0, unroll=2)


def gla_mixer(p3, w_lr, b_lr, onorm):
    B, L, _ = p3.shape
    lr_w = 256
    wl = jnp.zeros((lr_w, 2 * GLA_K_W), F32)
    wl = wl.at[0:GLA_RANK, 0:GLA_K_W].set(w_lr[0].astype(F32))
    wl = wl.at[GLA_RANK:2 * GLA_RANK, GLA_K_W:].set(w_lr[1].astype(F32))
    bl = b_lr.astype(F32).reshape(1, 2 * GLA_K_W)
    col = lambda width, c: pl.BlockSpec((1, L, width), lambda b: (b, 0, c // width))
    return pl.pallas_call(
        _gla_kernel,
        grid=(B,),
        in_specs=[
            col(GLA_K_W, COL_GQ), col(GLA_K_W, COL_GK), col(GLA_V_W, COL_GV),
            col(GLA_V_W, COL_GO), col(lr_w, COL_LR),
            pl.BlockSpec((lr_w, 2 * GLA_K_W), lambda b: (0, 0)),
            pl.BlockSpec((1, 2 * GLA_K_W), lambda b: (0, 0)),
            pl.BlockSpec((1, GLA_DV), lambda b: (0, 0)),
        ],
        out_specs=pl.BlockSpec((1, L, GLA_V_W), lambda b: (b, 0, 0)),
        out_shape=jax.ShapeDtypeStruct((B, L, GLA_V_W), BF16),
        scratch_shapes=[
            pltpu.VMEM((2, L, GLA_K_W), BF16),
            pltpu.VMEM((2, L // (2 * GLA_CHUNK) * GLA_K_W, 2 * GLA_CHUNK), BF16),
            pltpu.VMEM((2, L // (2 * GLA_CHUNK) * GLA_K_W, 2 * GLA_CHUNK), BF16),
            pltpu.VMEM((2, L // (2 * GLA_CHUNK) * GLA_K_W, 2 * GLA_CHUNK), F32),
            pltpu.VMEM((2, L, GLA_V_W), F32),
        ],
        compiler_params=_cparams("parallel"),
        name="gla_mixer",
    )(p3, p3, p3, p3, p3, wl.astype(BF16), bl, onorm.astype(F32).reshape(1, GLA_DV))


def _merge_kernel(a_ref, b_ref, c_ref, ga_ref, gb_ref, gc_ref, h_ref, wa_ref, wb_ref, wc_ref,
                  wo_ref, o_ref):
    mixed = (_sigmoid(ga_ref[...].astype(F32)) * _dot(a_ref[...], wa_ref[...])
             + _sigmoid(gb_ref[...].astype(F32)) * _dot(b_ref[...], wb_ref[...])
             + _sigmoid(gc_ref[...].astype(F32)) * _dot(c_ref[...], wc_ref[...]))
    o_ref[...] = h_ref[...] + _dot(mixed.astype(BF16), wo_ref[...])


def merge_out(ya, yb, yc, p2, h, wa, wb, wc, wo, tm=512):
    M, D = h.shape
    tm = _tile(M, tm)
    br = lambda w: pl.BlockSpec((tm, w), lambda i: (i, 0))
    gate = lambda c: pl.BlockSpec((tm, D), lambda i: (i, COL_GATE // D + c))
    wfull = lambda r: pl.BlockSpec((r, D), lambda i: (0, 0))
    return pl.pallas_call(
        _merge_kernel,
        grid=(M // tm,),
        in_specs=[br(ATT_Q_W), br(HY_WIDTH), br(GLA_V_W), gate(0), gate(1), gate(2),
                  pl.BlockSpec((tm, D), lambda i: (i, 0)),
                  wfull(ATT_Q_W), wfull(HY_WIDTH), wfull(GLA_V_W), wfull(D)],
        out_specs=pl.BlockSpec((tm, D), lambda i: (i, 0)),
        out_shape=jax.ShapeDtypeStruct((M, D), F32),
        compiler_params=_cparams("parallel"),
        name="gated_merge_out_proj",
    )(ya, yb, yc, p2, p2, p2, h, wa, wb, wc, wo)


def _head_rmsnorm(x, g):
    ms = jnp.mean(x * x, axis=-1, keepdims=True)
    return x * lax.rsqrt(ms + EPS) * g


def _xattn_kernel(h_ref, ln_ref, wq_ref, k_ref, v_ref, qn_ref, kn_ref, wo_ref, o_ref, att_ref):
    h = h_ref[0]
    hn = _head_rmsnorm(h, ln_ref[...]).astype(BF16)
    q = _dot(hn, wq_ref[...])
    for hd in range(X_HEADS):
        sl = slice(hd * X_HEAD_DIM, (hd + 1) * X_HEAD_DIM)
        qh = (_head_rmsnorm(q[:, sl], qn_ref[...]) * (X_HEAD_DIM ** -0.5)).astype(BF16)
        kh = _head_rmsnorm(k_ref[0, :, sl].astype(F32), kn_ref[...]).astype(BF16)
        s = _dot_nt(qh, kh)
        p = jnp.exp(s - jnp.max(s, axis=-1, keepdims=True))
        l = jnp.sum(p, axis=-1, keepdims=True)
        att_ref[:, sl] = (_dot(p.astype(BF16), v_ref[0, :, sl]) / l).astype(BF16)
    o_ref[0] = h + _dot(att_ref[...], wo_ref[...])


def cross_attention(h3, ln, wq, kv3, qn, kn, wo, tl=1024):
    B, L, D = h3.shape
    Mm = kv3.shape[1]
    tl = _tile(L, tl)
    vec = lambda w: pl.BlockSpec((1, w), lambda b, i: (0, 0))
    mat = pl.BlockSpec((D, D), lambda b, i: (0, 0))
    return pl.pallas_call(
        _xattn_kernel,
        grid=(B, L // tl),
        in_specs=[
            pl.BlockSpec((1, tl, D), lambda b, i: (b, i, 0)),
            vec(D), mat,
            pl.BlockSpec((1, Mm, D), lambda b, i: (b, 0, 0)),
            pl.BlockSpec((1, Mm, D), lambda b, i: (b, 0, 1)),
            vec(X_HEAD_DIM), vec(X_HEAD_DIM), mat,
        ],
        out_specs=pl.BlockSpec((1, tl, D), lambda b, i: (b, i, 0)),
        out_shape=jax.ShapeDtypeStruct((B, L, D), F32),
        scratch_shapes=[pltpu.VMEM((tl, D), BF16)],
        compiler_params=_cparams("parallel", "parallel"),
        name="memory_cross_attention",
    )(h3, ln.astype(F32).reshape(1, D), wq, kv3, kv3, qn.astype(F32).reshape(1, X_HEAD_DIM),
      kn.astype(F32).reshape(1, X_HEAD_DIM), wo)


MLP_HIDDEN_PIECE = 512


def _mlp_kernel(h_ref, g_ref, w1_ref, w2_ref, o_ref, hn_ref, acc_ref):
    j = pl.program_id(1)

    @pl.when(j == 0)
    def _():
        hn_ref[...] = _head_rmsnorm(h_ref[...], g_ref[...]).astype(BF16)
        acc_ref[...] = jnp.zeros_like(acc_ref)

    tf = w1_ref.shape[1]
    piece = min(tf, MLP_HIDDEN_PIECE)
    part = None
    for c in range(0, tf, piece):
        a = jnp.maximum(_dot(hn_ref[...], w1_ref[:, c:c + piece]), 0.0)
        y = _dot((a * a).astype(BF16), w2_ref[c:c + piece, :])
        part = y if part is None else part + y
    acc_ref[...] += part

    @pl.when(j == pl.num_programs(1) - 1)
    def _():
        o_ref[...] = h_ref[...] + acc_ref[...]


def mlp(h, g, w1, w2, tm=1024, tf=2048):
    M, D = h.shape
    F = w1.shape[1]
    tm, tf = _tile(M, tm), _tile(F, tf)
    return pl.pallas_call(
        _mlp_kernel,
        grid=(M // tm, F // tf),
        in_specs=[
            pl.BlockSpec((tm, D), lambda i, j: (i, 0)),
            pl.BlockSpec((1, D), lambda i, j: (0, 0)),
            pl.BlockSpec((D, tf), lambda i, j: (0, j)),
            pl.BlockSpec((tf, D), lambda i, j: (j, 0)),
        ],
        out_specs=pl.BlockSpec((tm, D), lambda i, j: (i, 0)),
        out_shape=jax.ShapeDtypeStruct((M, D), F32),
        scratch_shapes=[pltpu.VMEM((tm, D), BF16), pltpu.VMEM((tm, D), F32)],
        compiler_params=_cparams("parallel", "arbitrary"),
        name="relu2_mlp",
    )(h, g.astype(F32).reshape(1, D), w1, w2)


def _pack_w_in(w):
    kv0 = ATT_Q_W
    hy0 = kv0 + 2 * ATT_KV_W
    lr0 = hy0 + 3 * HY_WIDTH + 2 * GLA_K_W + 2 * GLA_V_W
    g0 = lr0 + 2 * GLA_RANK
    assert lr0 - hy0 == COL_AK - COL_HY and w.shape[2] - g0 == N_PACK - COL_GATE
    packed = jnp.zeros(w.shape[:2] + (N_PACK,), BF16)
    for src0, src1, dst in ((0, kv0, COL_AQ), (kv0, hy0, COL_AK), (hy0, lr0, COL_HY),
                            (lr0, g0, COL_LR), (g0, w.shape[2], COL_GATE)):
        packed = lax.dynamic_update_slice(packed, w[:, :, src0:src1].astype(BF16), (0, 0, dst))
    return packed


def kernel(x, mem, ln_mix, w_in, attn_qnorm, attn_knorm, hy_conv, hy_w1, hy_b1, hy_w2, hy_b2,
           hy_w3, hy_b3, hy_freq, hy_skip, gla_w_lr, gla_b_lr, gla_onorm, w_br_attn, w_br_hyena,
           w_br_gla, w_out, ln_x, ln_mem, x_wq, x_wk, x_wv, x_wo, x_qnorm, x_knorm, ln_mlp,
           mlp_w1, mlp_w2):
    B, L, D = x.shape
    Mm = mem.shape[1]
    depth = w_in.shape[0]
    M = B * L

    rope_tabs = _rope_tables(L)
    m = jnp.arange(L)
    z_neg, win_neg = _hyena_pos_features(L, (L - m).astype(F32))
    z_fwd, win_fwd = _hyena_pos_features(L, m.astype(F32))
    win_neg = jnp.where((m == 0)[:, None], 0.0, win_neg)
    zpad = lambda z: jnp.pad(z, ((0, 0), (0, HY_FFN - HY_EMB)))
    z_pair = jnp.concatenate([zpad(z_neg), zpad(z_fwd)], axis=1)
    win_pair = jnp.concatenate([win_neg, win_fwd], axis=1)
    fm, g_inv = _dft_tables(min(L, HY_BLOCK))
    bf = lambda a: a.astype(BF16)

    w_in_packed = _pack_w_in(w_in)
    wkv_all = jnp.concatenate([bf(x_wk), bf(x_wv)], axis=2)
    h = x.astype(F32).reshape(M, D)
    mem2 = mem.astype(F32).reshape(B * Mm, D)
    for i in range(depth):
        p2 = norm_matmul(h, ln_mix[i].astype(F32), w_in_packed, i, BF16)
        p3 = p2.reshape(B, L, N_PACK)
        y_a = attention(p3, attn_qnorm[i], attn_knorm[i], rope_tabs)
        f_all = hyena_filters(z_pair, win_pair, hy_w1[i].astype(F32), hy_b1[i].astype(F32),
                              hy_w2[i].astype(F32), hy_b2[i].astype(F32), hy_w3[i].astype(F32),
                              hy_b3[i].astype(F32), hy_freq[i].astype(F32))
        spec = hyena_spectra(fm, f_all)
        y_b = hyena_mixer(p3, hy_conv[i], spec, hy_skip[i], fm, g_inv)
        y_c = gla_mixer(p3, gla_w_lr[i], gla_b_lr[i], gla_onorm[i])
        h = merge_out(y_a.reshape(M, ATT_Q_W), y_b.reshape(M, HY_WIDTH), y_c.reshape(M, GLA_V_W),
                      p2, h, bf(w_br_attn[i]), bf(w_br_hyena[i]), bf(w_br_gla[i]), bf(w_out[i]))
        kv = norm_matmul(mem2, ln_mem[i].astype(F32), wkv_all, i, BF16).reshape(B, Mm, 2 * D)
        h = cross_attention(h.reshape(B, L, D), ln_x[i], bf(x_wq[i]), kv, x_qnorm[i],
                            x_knorm[i], bf(x_wo[i])).reshape(M, D)
        h = mlp(h, ln_mlp[i], bf(mlp_w1[i]), bf(mlp_w2[i]))
    return h.reshape(B, L, D).astype(x.dtype)
```

```python
import functools
import math

import jax
import jax.numpy as jnp
from jax import lax
from jax.experimental import pallas as pl
from jax.experimental.pallas import tpu as pltpu

F32 = jnp.float32
BF16 = jnp.bfloat16

D_MODEL = 1024
GRID_W = 64
ROPE_THETA = 10000.0
HEAD_DIM = 64
ATT_Q_HEADS = 8
ATT_KV_HEADS = 2
ATT_Q_W = ATT_Q_HEADS * HEAD_DIM
ATT_KV_W = ATT_KV_HEADS * HEAD_DIM
HY_WIDTH = 512
HY_ORDER = 2
HY_BANDS = 16
HY_EMB = 1 + 2 * HY_BANDS
HY_FFN = 64
HY_FAST_DECAY = 0.3
HY_SLOW_DECAY = 1.5
HY_TARGET = 1e-2
GLA_HEADS = 4
GLA_DK = 64
GLA_DV = 128
GLA_RANK = 16
GLA_NORMALIZER = 16.0
GLA_CHUNK = 64
GLA_K_W = GLA_HEADS * GLA_DK
GLA_V_W = GLA_HEADS * GLA_DV
X_HEADS = 4
X_HEAD_DIM = D_MODEL // X_HEADS
D_FF = 4 * D_MODEL
N_BRANCH = 3
EPS = 1e-6

COL_AQ = 0
COL_HY = 512
COL_GQ = 2048
COL_GK = 2304
COL_GV = 2560
COL_GO = 3072
COL_AK = 3584
COL_AV = 3712
COL_LR = 3840
COL_GATE = 4096
N_PACK = 7168

VMEM_LIMIT_BYTES = 52 * 1024 * 1024


def _cparams(*sem):
    return pltpu.CompilerParams(dimension_semantics=sem, vmem_limit_bytes=VMEM_LIMIT_BYTES)


def _tile(n, t):
    t = min(n, t)
    assert n % t == 0, (n, t)
    return t


def _dot(a, b):
    return jnp.dot(a, b, preferred_element_type=F32)


def _dot_nt(a, b):
    return lax.dot_general(a, b, (((1,), (1,)), ((), ())), preferred_element_type=F32)


def _dot_tn(a, b):
    return lax.dot_general(a, b, (((0,), (0,)), ((), ())), preferred_element_type=F32)


def _sigmoid(x):
    return 1.0 / (1.0 + jnp.exp(-x))


def _norm_mm_kernel(x_ref, g_ref, w_ref, o_ref, xn_ref):
    @pl.when(pl.program_id(1) == 0)
    def _():
        x = x_ref[...]
        ms = jnp.mean(x * x, axis=-1, keepdims=True)
        xn_ref[...] = (x * lax.rsqrt(ms + EPS) * g_ref[...]).astype(BF16)

    o_ref[...] = _dot(xn_ref[...], w_ref[...]).astype(o_ref.dtype)


def norm_matmul(x, g, w, layer, out_dtype, tm=1024, tn=3584):
    M, K = x.shape
    N = w.shape[2]
    tm, tn = _tile(M, tm), _tile(N, tn)
    return pl.pallas_call(
        _norm_mm_kernel,
        grid=(M // tm, N // tn),
        in_specs=[
            pl.BlockSpec((tm, K), lambda i, j: (i, 0)),
            pl.BlockSpec((1, K), lambda i, j: (0, 0)),
            pl.BlockSpec((None, K, tn), lambda i, j: (layer, 0, j)),
        ],
        out_specs=pl.BlockSpec((tm, tn), lambda i, j: (i, j)),
        out_shape=jax.ShapeDtypeStruct((M, N), out_dtype),
        scratch_shapes=[pltpu.VMEM((tm, K), BF16)],
        compiler_params=_cparams("parallel", "arbitrary"),
        name="norm_matmul",
    )(x, g.reshape(1, K), w)


def _group_mean_sq(x, gm_ref):
    return _dot((x * x).astype(BF16), gm_ref[...])


def _rope(x, c_ref, sa_ref, sb_ref):
    w = x.shape[-1]
    return (x * c_ref[...] + pltpu.roll(x, w - HEAD_DIM // 4, 1) * sa_ref[...]
            + pltpu.roll(x, HEAD_DIM // 4, 1) * sb_ref[...])


def _attn_kernel(q_ref, k_ref, v_ref, qn_ref, kn_ref, cq_ref, saq_ref, sbq_ref,
                 ck_ref, sak_ref, sbk_ref, gmq_ref, gmk_ref, o_ref, kp_ref, vx_ref):
    @pl.when(pl.program_id(1) == 0)
    def _():
        k = k_ref[0].astype(F32)
        kh = k * lax.rsqrt(_group_mean_sq(k, gmk_ref) + EPS) * kn_ref[...]
        kp_ref[...] = _rope(kh, ck_ref, sak_ref, sbk_ref).astype(BF16)
        v = v_ref[0]
        ones = jnp.ones((v.shape[0], HEAD_DIM), BF16)
        vx_ref[...] = jnp.concatenate(
            [piece for kv in range(ATT_KV_HEADS)
             for piece in (v[:, kv * HEAD_DIM:(kv + 1) * HEAD_DIM], ones)], axis=-1)

    q = q_ref[0].astype(F32)
    qh = q * lax.rsqrt(_group_mean_sq(q, gmq_ref) + EPS) * qn_ref[...]
    qb = (_rope(qh, cq_ref, saq_ref, sbq_ref) * (HEAD_DIM ** -0.5 * math.log2(math.e))).astype(BF16)
    group = ATT_Q_HEADS // ATT_KV_HEADS
    outs = []
    for h in range(ATT_Q_HEADS):
        kv = h // group
        kk = kp_ref[:, kv * HEAD_DIM:(kv + 1) * HEAD_DIM]
        s = _dot_nt(qb[:, h * HEAD_DIM:(h + 1) * HEAD_DIM], kk)
        p = jnp.exp2(s - jnp.max(s, axis=-1, keepdims=True))
        ox = _dot(p.astype(BF16), vx_ref[:, kv * 2 * HEAD_DIM:(kv + 1) * 2 * HEAD_DIM])
        outs.append(ox[:, :HEAD_DIM] / ox[:, HEAD_DIM:])
    o_ref[0] = jnp.concatenate(outs, axis=-1).astype(o_ref.dtype)


def _rope_tables(L):
    rows = L // GRID_W
    r, c = jnp.meshgrid(jnp.arange(rows), jnp.arange(GRID_W), indexing="ij")
    n_freq = HEAD_DIM // 4
    inv = ROPE_THETA ** (-jnp.arange(n_freq, dtype=F32) / n_freq)
    pos = jnp.stack([r.reshape(-1), c.reshape(-1)], axis=1).astype(F32)
    ang = pos[:, :, None] * inv
    cos, sin = jnp.cos(ang), jnp.sin(ang)
    zero = jnp.zeros_like(sin)
    c64 = jnp.concatenate([cos, cos], axis=-1).reshape(L, HEAD_DIM)
    sa64 = jnp.concatenate([-sin, zero], axis=-1).reshape(L, HEAD_DIM)
    sb64 = jnp.concatenate([zero, sin], axis=-1).reshape(L, HEAD_DIM)
    return c64, sa64, sb64


def attention(p3, qn, kn, tabs, tq=1024):
    B, L, _ = p3.shape
    tq = _tile(L, tq)
    c64, sa64, sb64 = tabs
    tq_tabs = [jnp.tile(t, (1, ATT_Q_HEADS)) for t in (c64, sa64, sb64)]
    tk_tabs = [jnp.tile(t, (1, ATT_KV_HEADS)) for t in (c64, sa64, sb64)]
    qn_t = jnp.tile(qn.astype(F32), ATT_Q_HEADS).reshape(1, ATT_Q_W)
    kn_t = jnp.tile(kn.astype(F32), ATT_KV_HEADS).reshape(1, ATT_KV_W)

    def group_mean(width):
        g = jnp.arange(width) // HEAD_DIM
        return jnp.where(g[:, None] == g[None, :], 1.0 / HEAD_DIM, 0.0).astype(BF16)

    qtab = pl.BlockSpec((tq, ATT_Q_W), lambda b, i: (i, 0))
    ktab = pl.BlockSpec((L, ATT_KV_W), lambda b, i: (0, 0))
    return pl.pallas_call(
        _attn_kernel,
        grid=(B, L // tq),
        in_specs=[
            pl.BlockSpec((1, tq, ATT_Q_W), lambda b, i: (b, i, COL_AQ // ATT_Q_W)),
            pl.BlockSpec((1, L, ATT_KV_W), lambda b, i: (b, 0, COL_AK // ATT_KV_W)),
            pl.BlockSpec((1, L, ATT_KV_W), lambda b, i: (b, 0, COL_AV // ATT_KV_W)),
            pl.BlockSpec((1, ATT_Q_W), lambda b, i: (0, 0)),
            pl.BlockSpec((1, ATT_KV_W), lambda b, i: (0, 0)),
            qtab, qtab, qtab, ktab, ktab, ktab,
            pl.BlockSpec((ATT_Q_W, ATT_Q_W), lambda b, i: (0, 0)),
            pl.BlockSpec((ATT_KV_W, ATT_KV_W), lambda b, i: (0, 0)),
        ],
        out_specs=pl.BlockSpec((1, tq, ATT_Q_W), lambda b, i: (b, i, 0)),
        out_shape=jax.ShapeDtypeStruct((B, L, ATT_Q_W), BF16),
        scratch_shapes=[pltpu.VMEM((L, ATT_KV_W), BF16), pltpu.VMEM((L, 2 * ATT_KV_W), BF16)],
        compiler_params=_cparams("parallel", "arbitrary"),
        name="gqa_attention",
    )(p3, p3, p3, qn_t, kn_t, *tq_tabs, *tk_tabs, group_mean(ATT_Q_W), group_mean(ATT_KV_W))


def _dft_tables(L):
    n = 2 * L
    k = jnp.arange(L, dtype=jnp.int32)[:, None]
    j = jnp.arange(L, dtype=jnp.int32)[None, :]
    step = 64
    ja = jnp.arange(0, L, step, dtype=jnp.int32)[None, :]
    jb = jnp.arange(step, dtype=jnp.int32)[None, :]
    ang_a = ((k * ja) % n).astype(F32) * (2.0 * math.pi / n)
    ang_b = ((k * jb) % n).astype(F32) * (2.0 * math.pi / n)
    ca, sa = jnp.cos(ang_a)[:, :, None], jnp.sin(ang_a)[:, :, None]
    cb, sb = jnp.cos(ang_b)[:, None, :], jnp.sin(ang_b)[:, None, :]
    c = (ca * cb - sa * sb).reshape(L, L)
    s = -(sa * cb + ca * sb).reshape(L, L)
    bottom = jnp.where(k == 0, jnp.where(j % 2 == 0, 1.0, -1.0), s)
    bottom_t = jnp.where(j == 0, jnp.where(k % 2 == 0, 1.0, -1.0), s)
    fm = jnp.stack([c, bottom]).astype(BF16)
    g = jnp.concatenate([c, bottom_t], axis=1).astype(BF16)
    return fm, g


def _hyena_pos_features(L, t):
    t_norm = t / max(L - 1, 1)
    w = 2.0 * math.pi * t / L
    f = jnp.linspace(1e-4, HY_BANDS - 1, HY_BANDS, dtype=F32)
    fw = w[:, None] * f
    z = jnp.concatenate([t_norm[:, None], jnp.cos(fw), -jnp.sin(fw)], axis=-1)
    deltas = jnp.abs(jnp.linspace(math.log(HY_TARGET) / HY_FAST_DECAY,
                                  math.log(HY_TARGET) / HY_SLOW_DECAY, HY_WIDTH, dtype=F32))
    window = jnp.exp(-t_norm[:, None] * deltas)
    return z, window


def _sconv_kernel(u_ref, w_ref, o_ref):
    u = u_ref[0].astype(F32)
    L = u.shape[0]
    row = lax.broadcasted_iota(jnp.int32, u.shape, 0)
    prev = jnp.where(row == 0, 0.0, pltpu.roll(u, 1, 0))
    nxt = jnp.where(row == L - 1, 0.0, pltpu.roll(u, L - 1, 0))
    o_ref[0] = (prev * w_ref[0:1, :] + u * w_ref[1:2, :] + nxt * w_ref[2:3, :]).astype(o_ref.dtype)


def short_conv(p3, w):
    B, L, _ = p3.shape
    nblk = 3
    return pl.pallas_call(
        _sconv_kernel,
        grid=(B, nblk),
        in_specs=[
            pl.BlockSpec((1, L, HY_WIDTH), lambda b, c: (b, 0, COL_HY // HY_WIDTH + c)),
            pl.BlockSpec((3, HY_WIDTH), lambda b, c: (0, c)),
        ],
        out_specs=pl.BlockSpec((1, L, HY_WIDTH), lambda b, c: (b, 0, c)),
        out_shape=jax.ShapeDtypeStruct((B, L, nblk * HY_WIDTH), BF16),
        compiler_params=_cparams("parallel", "parallel"),
        name="hyena_short_conv",
    )(p3, w)


def _hyfilter_kernel(z_ref, w1_ref, b1_ref, w2_ref, b2_ref, w3_ref, b3_ref, fr_ref, win_ref,
                     f_ref):
    L = z_ref.shape[0]
    hp = lax.Precision.HIGHEST
    h = jnp.sin(fr_ref[0:1, :] * (jnp.dot(z_ref[...], w1_ref[...], precision=hp,
                                          preferred_element_type=F32) + b1_ref[...]))
    h = jnp.sin(fr_ref[1:2, :] * (jnp.dot(h, w2_ref[...], precision=hp,
                                          preferred_element_type=F32) + b2_ref[...]))
    f = (_dot(h.astype(BF16), w3_ref[0].astype(BF16)) + b3_ref[0]) * win_ref[...]
    col = jnp.sum(jnp.abs(f), axis=0, keepdims=True)
    norm = col[:, :HY_WIDTH] + col[:, HY_WIDTH:] + EPS
    f_ref[:L, :] = (f[:, :HY_WIDTH] / norm).astype(f_ref.dtype)
    f_ref[L:, :] = (f[:, HY_WIDTH:] / norm).astype(f_ref.dtype)


def _block_diag2(w):
    z = jnp.zeros_like(w)
    return jnp.concatenate([jnp.concatenate([w, z], axis=1), jnp.concatenate([z, w], axis=1)], axis=0)


def hyena_filters(z_pair, win_pair, w1, b1, w2, b2, w3, b3, freq):
    L = z_pair.shape[0]
    pad_w1 = jnp.pad(w1, ((0, HY_FFN - HY_EMB), (0, 0)))
    two = lambda v: jnp.concatenate([v, v], axis=-1)
    w3r = w3.reshape(HY_FFN, HY_ORDER, 2, HY_WIDTH)
    zero = jnp.zeros((HY_ORDER, HY_FFN, HY_WIDTH), F32)
    w3n = jnp.moveaxis(w3r[:, :, 1], 1, 0)
    w3p = jnp.moveaxis(w3r[:, :, 0], 1, 0)
    w3bd = jnp.concatenate([jnp.concatenate([w3n, zero], axis=2),
                            jnp.concatenate([zero, w3p], axis=2)], axis=1)
    b3r = b3.reshape(HY_ORDER, 1, 2, HY_WIDTH)
    b3c = jnp.concatenate([b3r[:, :, 1], b3r[:, :, 0]], axis=-1)
    full = lambda shape: pl.BlockSpec(shape, lambda o: (0,) * len(shape))
    H2 = 2 * HY_FFN
    return pl.pallas_call(
        _hyfilter_kernel,
        grid=(HY_ORDER,),
        in_specs=[
            full((L, H2)), full((H2, H2)), full((1, H2)), full((H2, H2)), full((1, H2)),
            pl.BlockSpec((1, H2, 2 * HY_WIDTH), lambda o: (o, 0, 0)),
            pl.BlockSpec((1, 1, 2 * HY_WIDTH), lambda o: (o, 0, 0)),
            full((2, H2)), full((L, 2 * HY_WIDTH)),
        ],
        out_specs=pl.BlockSpec((2 * L, HY_WIDTH), lambda o: (0, o)),
        out_shape=jax.ShapeDtypeStruct((2 * L, HY_ORDER * HY_WIDTH), BF16),
        compiler_params=_cparams("parallel"),
        name="hyena_filter_mlp",
    )(z_pair, _block_diag2(pad_w1), two(b1.reshape(1, -1)), _block_diag2(w2),
      two(b2.reshape(1, -1)), w3bd, b3c, two(freq), win_pair)


def _spec_kernel(fm_ref, blk_ref, h_ref, prev_ref, *, n):
    P = fm_ref.shape[1]
    row = lax.broadcasted_iota(jnp.int32, (P, blk_ref.shape[1]), 0)
    sign = jnp.where(row % 2 == 0, 1.0, -1.0)
    scale = jnp.where(row == 0, 1.0 / n, 2.0 / n)
    @pl.when(pl.program_id(0) == 0)
    def _():
        prev_ref[...] = jnp.zeros_like(prev_ref)

    for half in range(2):
        phi = _dot(fm_ref[half], blk_ref[...])
        h_ref[0, half] = ((phi + sign * prev_ref[half]) * scale).astype(h_ref.dtype)
        prev_ref[half] = phi


def hyena_spectra(fm, f_all):
    P = fm.shape[1]
    W = f_all.shape[1]
    n_blk = f_all.shape[0] // P
    return pl.pallas_call(
        functools.partial(_spec_kernel, n=2 * P),
        grid=(n_blk,),
        in_specs=[
            pl.BlockSpec((2, P, P), lambda e: (0, 0, 0)),
            pl.BlockSpec((P, W), lambda e: (e, 0)),
        ],
        out_specs=pl.BlockSpec((1, 2, P, W), lambda e: (jnp.maximum(e - 1, 0), 0, 0, 0)),
        out_shape=jax.ShapeDtypeStruct((n_blk - 1, 2, P, W), BF16),
        scratch_shapes=[pltpu.VMEM((2, P, W), F32)],
        compiler_params=_cparams("arbitrary"),
        name="hyena_filter_spectrum",
    )(fm, f_all)


HY_BLOCK = 512
HY_CONV_ROWS = 16


def _lconv_kernel(fm_ref, g_ref, u_ref, gate_ref, skip_ref, h_ref, o_ref, x_ref, yt_ref, yb_ref):
    P = fm_ref.shape[1]
    nb = u_ref.shape[1] // P
    ct = u_ref.shape[2]
    for j in range(nb):
        uj = u_ref[0, j * P:(j + 1) * P, :]
        x_ref[j, 0] = _dot(fm_ref[0], uj).astype(BF16)
        x_ref[j, 1] = _dot(fm_ref[1], uj).astype(BF16)
    first = lax.broadcasted_iota(jnp.int32, (HY_CONV_ROWS, ct), 0) == 0
    zero = jnp.zeros((HY_CONV_ROWS, ct), BF16)
    for i in range(nb):
        for r in range(P // HY_CONV_ROWS):
            rows = slice(r * HY_CONV_ROWS, (r + 1) * HY_CONV_ROWS)
            at, ab = zero, zero
            for j in range(nb):
                d = i - j + nb - 1
                xt, xb = x_ref[j, 0, rows, :], x_ref[j, 1, rows, :]
                ht, hb = h_ref[d, 0, rows, :], h_ref[d, 1, rows, :]
                bb = xb * hb
                if r == 0:
                    at += xt * ht - jnp.where(first, zero, bb)
                    ab += jnp.where(first, bb, xt * hb + xb * ht)
                else:
                    at += xt * ht - bb
                    ab += xt * hb + xb * ht
            yt_ref[i, rows, :] = at
            yb_ref[i, rows, :] = ab
        y = _dot(g_ref[:, :P], yt_ref[i]) + _dot(g_ref[:, P:], yb_ref[i])
        blk = slice(i * P, (i + 1) * P)
        ui = u_ref[0, blk, :].astype(F32)
        o_ref[0, blk, :] = ((y + ui * skip_ref[...]) * gate_ref[0, blk, :].astype(F32)
                            ).astype(o_ref.dtype)


def long_conv(fm, g, spec, order, u, u_col, gate, gate_col, skip, ct=256):
    B, L, _ = u.shape
    P = fm.shape[1]
    nb = L // P
    n_lag = spec.shape[0]
    per = HY_WIDTH // ct
    return pl.pallas_call(
        _lconv_kernel,
        grid=(per, B),
        in_specs=[
            pl.BlockSpec((2, P, P), lambda c, b: (0, 0, 0)),
            pl.BlockSpec((P, 2 * P), lambda c, b: (0, 0)),
            pl.BlockSpec((1, L, ct), lambda c, b: (b, 0, u_col * per + c)),
            pl.BlockSpec((1, L, ct), lambda c, b: (b, 0, gate_col * per + c)),
            pl.BlockSpec((1, ct), lambda c, b: (0, c)),
            pl.BlockSpec((n_lag, 2, P, ct), lambda c, b: (0, 0, 0, order * per + c)),
        ],
        out_specs=pl.BlockSpec((1, L, ct), lambda c, b: (b, 0, c)),
        out_shape=jax.ShapeDtypeStruct((B, L, HY_WIDTH), BF16),
        scratch_shapes=[pltpu.VMEM((nb, 2, P, ct), BF16), pltpu.VMEM((nb, P, ct), BF16),
                        pltpu.VMEM((nb, P, ct), BF16)],
        compiler_params=_cparams("parallel", "parallel"),
        name="hyena_long_conv",
    )(fm, g, u, gate, skip.reshape(1, HY_WIDTH).astype(F32), spec)


def hyena_mixer(p3, conv_w, spec, skip, fm, g):
    uc = short_conv(p3, conv_w.astype(F32))
    z = long_conv(fm, g, spec, 0, uc, 0, uc, 1, skip[0])
    return long_conv(fm, g, spec, 1, z, 0, uc, 2, skip[1])


def _log_sigmoid(x):
    return -(jnp.maximum(-x, 0.0) + jnp.log(1.0 + jnp.exp(-jnp.abs(x))))


def _dot_hilo(m, x):
    hi = x.astype(BF16)
    lo = (x - hi.astype(F32)).astype(BF16)
    return _dot(m, hi) + _dot(m, lo)


GLA_PREP_ROWS = 256


def _gla_kernel(q_ref, k_ref, v_ref, og_ref, lr_ref, wlr_ref, blr_ref, on_ref, o_ref,
                qd_ref, kit_ref, kst_ref, dcol_ref, acc_ref):
    L = q_ref.shape[1]
    C = GLA_CHUNK
    H = GLA_HEADS
    RB = GLA_PREP_ROWS
    n_chunks = L // C
    KW, VW, DV = GLA_K_W, GLA_V_W, GLA_DV

    pr = lax.broadcasted_iota(jnp.int32, (RB, RB), 0)
    pc = lax.broadcasted_iota(jnp.int32, (RB, RB), 1)
    same = (pr // C) == (pc // C)
    cum_f = jnp.where(same & (pc <= pr), 1.0, 0.0).astype(BF16)
    cum_b = jnp.where(same & (pc >= pr), 1.0, 0.0).astype(BF16)
    tot_m = jnp.where(same, 1.0, 0.0).astype(BF16)
    PW = 2 * C
    n_pairs = L // PW

    def prep(i, carry):
        r0 = pl.multiple_of(i * RB, RB)
        logit = _dot(lr_ref[0, pl.ds(r0, RB), :], wlr_ref[...]) + blr_ref[...]
        g = _log_sigmoid(logit) * (1.0 / GLA_NORMALIZER)
        q = q_ref[0, pl.ds(r0, RB), :].astype(F32) * (GLA_DK ** -0.5)
        k = k_ref[0, pl.ds(r0, RB), :].astype(F32)
        for d, cum in ((0, cum_f), (1, cum_b)):
            gd = g[:, d * KW:(d + 1) * KW]
            b = _dot_hilo(cum, gd)
            tot = _dot_hilo(tot_m, gd)
            qd_ref[d, pl.ds(r0, RB), :] = (q * jnp.exp(b)).astype(BF16)
            ki_t = (k * jnp.exp(-b)).T
            ks_t = (k * jnp.exp(tot - b)).T
            dec_t = jnp.exp(tot).T
            for p in range(RB // PW):
                rows = pl.ds(pl.multiple_of((i * (RB // PW) + p) * KW, KW), KW)
                cols = slice(p * PW, (p + 1) * PW)
                kit_ref[d, rows, :] = ki_t[:, cols].astype(BF16)
                kst_ref[d, rows, :] = ks_t[:, cols].astype(BF16)
                dcol_ref[d, rows, :] = dec_t[:, cols]
        return carry

    lax.fori_loop(0, L // RB, prep, 0, unroll=2)

    kk_blk = (lax.broadcasted_iota(jnp.int32, (KW, H * C), 0) // GLA_DK
              == lax.broadcasted_iota(jnp.int32, (KW, H * C), 1) // C)
    kv_blk = (lax.broadcasted_iota(jnp.int32, (KW, VW), 0) // GLA_DK
              == lax.broadcasted_iota(jnp.int32, (KW, VW), 1) // DV)
    arow = lax.broadcasted_iota(jnp.int32, (C, H * C), 0)
    acol = lax.broadcasted_iota(jnp.int32, (C, H * C), 1) % C
    amasks = (acol <= arow, acol > arow)
    zero_b = jnp.zeros((), BF16)

    def body(ip, sts):
        new = []
        for d in (0, 1):
            st = sts[d]
            pair = ip if d == 0 else n_pairs - 1 - ip
            rk = pl.multiple_of(pair * KW, KW)
            kit2 = kit_ref[d, pl.ds(rk, KW), :]
            kst2 = kst_ref[d, pl.ds(rk, KW), :]
            dec2 = dcol_ref[d, pl.ds(rk, KW), :]
            for half in ((0, 1) if d == 0 else (1, 0)):
                r0 = pl.multiple_of(pair * PW + half * C, C)
                cols = slice(half * C, (half + 1) * C)
                qd = qd_ref[d, pl.ds(r0, C), :]
                v = v_ref[0, pl.ds(r0, C), :]
                kk = jnp.where(kk_blk, jnp.concatenate([kit2[:, cols]] * H, axis=1), zero_b)
                a = jnp.where(amasks[d], _dot(qd, kk), 0.0).astype(BF16)
                v_bd = jnp.where(kv_blk, jnp.concatenate([v] * H, axis=0), zero_b)
                st_b = st.astype(BF16)
                s_bd = jnp.where(kv_blk, jnp.concatenate([st_b] * H, axis=1), zero_b)
                acc_ref[d, pl.ds(r0, C), :] = _dot(a, v_bd) + _dot(qd, s_bd)
                ds = jnp.concatenate(
                    [_dot(kst2[h * GLA_DK:(h + 1) * GLA_DK, cols], v[:, h * DV:(h + 1) * DV])
                     for h in range(H)], axis=0)
                st = st * jnp.concatenate([dec2[:, cols]] * (DV // C), axis=1) + ds
            new.append(st)
        return tuple(new)

    zero = jnp.zeros((KW, DV), F32)
    lax.fori_loop(0, n_pairs, body, (zero, zero), unroll=4)

    def finalize(i, carry):
        r0 = pl.multiple_of(i * RB, RB)
        o = acc_ref[0, pl.ds(r0, RB), :] + acc_ref[1, pl.ds(r0, RB), :]
        og = og_ref[0, pl.ds(r0, RB), :].astype(F32)
        outs = []
        for h in range(H):
            oh = o[:, h * GLA_DV:(h + 1) * GLA_DV]
            ms = jnp.mean(oh * oh, axis=-1, keepdims=True)
            outs.append(oh * lax.rsqrt(ms + EPS) * on_ref[...])
        y = jnp.concatenate(outs, axis=-1) * (og * _sigmoid(og))
        o_ref[0, pl.ds(r0, RB), :] = y.astype(o_ref.dtype)
        return carry

    lax.fori_loop(0, L // RB, finalize, 0, unroll=2)


def gla_mixer(p3, w_lr, b_lr, onorm):
    B, L, _ = p3.shape
    lr_w = 256
    wl = jnp.zeros((lr_w, 2 * GLA_K_W), F32)
    wl = wl.at[0:GLA_RANK, 0:GLA_K_W].set(w_lr[0].astype(F32))
    wl = wl.at[GLA_RANK:2 * GLA_RANK, GLA_K_W:].set(w_lr[1].astype(F32))
    bl = b_lr.astype(F32).reshape(1, 2 * GLA_K_W)
    col = lambda width, c: pl.BlockSpec((1, L, width), lambda b: (b, 0, c // width))
    return pl.pallas_call(
        _gla_kernel,
        grid=(B,),
        in_specs=[
            col(GLA_K_W, COL_GQ), col(GLA_K_W, COL_GK), col(GLA_V_W, COL_GV),
            col(GLA_V_W, COL_GO), col(lr_w, COL_LR),
            pl.BlockSpec((lr_w, 2 * GLA_K_W), lambda b: (0, 0)),
            pl.BlockSpec((1, 2 * GLA_K_W), lambda b: (0, 0)),
            pl.BlockSpec((1, GLA_DV), lambda b: (0, 0)),
        ],
        out_specs=pl.BlockSpec((1, L, GLA_V_W), lambda b: (b, 0, 0)),
        out_shape=jax.ShapeDtypeStruct((B, L, GLA_V_W), BF16),
        scratch_shapes=[
            pltpu.VMEM((2, L, GLA_K_W), BF16),
            pltpu.VMEM((2, L // (2 * GLA_CHUNK) * GLA_K_W, 2 * GLA_CHUNK), BF16),
            pltpu.VMEM((2, L // (2 * GLA_CHUNK) * GLA_K_W, 2 * GLA_CHUNK), BF16),
            pltpu.VMEM((2, L // (2 * GLA_CHUNK) * GLA_K_W, 2 * GLA_CHUNK), F32),
            pltpu.VMEM((2, L, GLA_V_W), F32),
        ],
        compiler_params=_cparams("parallel"),
        name="gla_mixer",
    )(p3, p3, p3, p3, p3, wl.astype(BF16), bl, onorm.astype(F32).reshape(1, GLA_DV))


def _merge_kernel(a_ref, b_ref, c_ref, ga_ref, gb_ref, gc_ref, h_ref, wa_ref, wb_ref, wc_ref,
                  wo_ref, o_ref):
    mixed = (_sigmoid(ga_ref[...].astype(F32)) * _dot(a_ref[...], wa_ref[...])
             + _sigmoid(gb_ref[...].astype(F32)) * _dot(b_ref[...], wb_ref[...])
             + _sigmoid(gc_ref[...].astype(F32)) * _dot(c_ref[...], wc_ref[...]))
    o_ref[...] = h_ref[...] + _dot(mixed.astype(BF16), wo_ref[...])


def merge_out(ya, yb, yc, p2, h, wa, wb, wc, wo, tm=1024):
    M, D = h.shape
    tm = _tile(M, tm)
    br = lambda w: pl.BlockSpec((tm, w), lambda i: (i, 0))
    gate = lambda c: pl.BlockSpec((tm, D), lambda i: (i, COL_GATE // D + c))
    wfull = lambda r: pl.BlockSpec((r, D), lambda i: (0, 0))
    return pl.pallas_call(
        _merge_kernel,
        grid=(M // tm,),
        in_specs=[br(ATT_Q_W), br(HY_WIDTH), br(GLA_V_W), gate(0), gate(1), gate(2),
                  pl.BlockSpec((tm, D), lambda i: (i, 0)),
                  wfull(ATT_Q_W), wfull(HY_WIDTH), wfull(GLA_V_W), wfull(D)],
        out_specs=pl.BlockSpec((tm, D), lambda i: (i, 0)),
        out_shape=jax.ShapeDtypeStruct((M, D), F32),
        compiler_params=_cparams("parallel"),
        name="gated_merge_out_proj",
    )(ya, yb, yc, p2, p2, p2, h, wa, wb, wc, wo)


def _head_rmsnorm(x, g):
    ms = jnp.mean(x * x, axis=-1, keepdims=True)
    return x * lax.rsqrt(ms + EPS) * g


def _xattn_kernel(h_ref, ln_ref, wq_ref, k_ref, v_ref, qn_ref, kn_ref, wo_ref, o_ref, att_ref):
    h = h_ref[0]
    hn = _head_rmsnorm(h, ln_ref[...]).astype(BF16)
    q = _dot(hn, wq_ref[...])
    for hd in range(X_HEADS):
        sl = slice(hd * X_HEAD_DIM, (hd + 1) * X_HEAD_DIM)
        qh = (_head_rmsnorm(q[:, sl], qn_ref[...]) * (X_HEAD_DIM ** -0.5)).astype(BF16)
        kh = _head_rmsnorm(k_ref[0, :, sl].astype(F32), kn_ref[...]).astype(BF16)
        s = _dot_nt(qh, kh)
        p = jnp.exp(s - jnp.max(s, axis=-1, keepdims=True))
        l = jnp.sum(p, axis=-1, keepdims=True)
        att_ref[:, sl] = (_dot(p.astype(BF16), v_ref[0, :, sl]) / l).astype(BF16)
    o_ref[0] = h + _dot(att_ref[...], wo_ref[...])


def cross_attention(h3, ln, wq, kv3, qn, kn, wo, tl=1024):
    B, L, D = h3.shape
    Mm = kv3.shape[1]
    tl = _tile(L, tl)
    vec = lambda w: pl.BlockSpec((1, w), lambda b, i: (0, 0))
    mat = pl.BlockSpec((D, D), lambda b, i: (0, 0))
    return pl.pallas_call(
        _xattn_kernel,
        grid=(B, L // tl),
        in_specs=[
            pl.BlockSpec((1, tl, D), lambda b, i: (b, i, 0)),
            vec(D), mat,
            pl.BlockSpec((1, Mm, D), lambda b, i: (b, 0, 0)),
            pl.BlockSpec((1, Mm, D), lambda b, i: (b, 0, 1)),
            vec(X_HEAD_DIM), vec(X_HEAD_DIM), mat,
        ],
        out_specs=pl.BlockSpec((1, tl, D), lambda b, i: (b, i, 0)),
        out_shape=jax.ShapeDtypeStruct((B, L, D), F32),
        scratch_shapes=[pltpu.VMEM((tl, D), BF16)],
        compiler_params=_cparams("parallel", "parallel"),
        name="memory_cross_attention",
    )(h3, ln.astype(F32).reshape(1, D), wq, kv3, kv3, qn.astype(F32).reshape(1, X_HEAD_DIM),
      kn.astype(F32).reshape(1, X_HEAD_DIM), wo)


def _mlp_kernel(h_ref, g_ref, w1_ref, w2_ref, o_ref, hn_ref, acc_ref):
    j = pl.program_id(1)

    @pl.when(j == 0)
    def _():
        hn_ref[...] = _head_rmsnorm(h_ref[...], g_ref[...]).astype(BF16)
        acc_ref[...] = jnp.zeros_like(acc_ref)

    a = jnp.maximum(_dot(hn_ref[...], w1_ref[...]), 0.0)
    acc_ref[...] += _dot((a * a).astype(BF16), w2_ref[...])

    @pl.when(j == pl.num_programs(1) - 1)
    def _():
        o_ref[...] = h_ref[...] + acc_ref[...]


def mlp(h, g, w1, w2, tm=1024, tf=2048):
    M, D = h.shape
    F = w1.shape[1]
    tm, tf = _tile(M, tm), _tile(F, tf)
    return pl.pallas_call(
        _mlp_kernel,
        grid=(M // tm, F // tf),
        in_specs=[
            pl.BlockSpec((tm, D), lambda i, j: (i, 0)),
            pl.BlockSpec((1, D), lambda i, j: (0, 0)),
            pl.BlockSpec((D, tf), lambda i, j: (0, j)),
            pl.BlockSpec((tf, D), lambda i, j: (j, 0)),
        ],
        out_specs=pl.BlockSpec((tm, D), lambda i, j: (i, 0)),
        out_shape=jax.ShapeDtypeStruct((M, D), F32),
        scratch_shapes=[pltpu.VMEM((tm, D), BF16), pltpu.VMEM((tm, D), F32)],
        compiler_params=_cparams("parallel", "arbitrary"),
        name="relu2_mlp",
    )(h, g.astype(F32).reshape(1, D), w1, w2)


def _pack_w_in(w):
    kv0 = ATT_Q_W
    hy0 = kv0 + 2 * ATT_KV_W
    lr0 = hy0 + 3 * HY_WIDTH + 2 * GLA_K_W + 2 * GLA_V_W
    g0 = lr0 + 2 * GLA_RANK
    assert lr0 - hy0 == COL_AK - COL_HY and w.shape[2] - g0 == N_PACK - COL_GATE
    packed = jnp.zeros(w.shape[:2] + (N_PACK,), BF16)
    for src0, src1, dst in ((0, kv0, COL_AQ), (kv0, hy0, COL_AK), (hy0, lr0, COL_HY),
                            (lr0, g0, COL_LR), (g0, w.shape[2], COL_GATE)):
        packed = lax.dynamic_update_slice(packed, w[:, :, src0:src1].astype(BF16), (0, 0, dst))
    return packed


def kernel(x, mem, ln_mix, w_in, attn_qnorm, attn_knorm, hy_conv, hy_w1, hy_b1, hy_w2, hy_b2,
           hy_w3, hy_b3, hy_freq, hy_skip, gla_w_lr, gla_b_lr, gla_onorm, w_br_attn, w_br_hyena,
           w_br_gla, w_out, ln_x, ln_mem, x_wq, x_wk, x_wv, x_wo, x_qnorm, x_knorm, ln_mlp,
           mlp_w1, mlp_w2):
    B, L, D = x.shape
    Mm = mem.shape[1]
    depth = w_in.shape[0]
    M = B * L

    rope_tabs = _rope_tables(L)
    m = jnp.arange(L)
    z_neg, win_neg = _hyena_pos_features(L, (L - m).astype(F32))
    z_fwd, win_fwd = _hyena_pos_features(L, m.astype(F32))
    win_neg = jnp.where((m == 0)[:, None], 0.0, win_neg)
    zpad = lambda z: jnp.pad(z, ((0, 0), (0, HY_FFN - HY_EMB)))
    z_pair = jnp.concatenate([zpad(z_neg), zpad(z_fwd)], axis=1)
    win_pair = jnp.concatenate([win_neg, win_fwd], axis=1)
    fm, g_inv = _dft_tables(min(L, HY_BLOCK))
    bf = lambda a: a.astype(BF16)

    w_in_packed = _pack_w_in(w_in)
    wkv_all = jnp.concatenate([bf(x_wk), bf(x_wv)], axis=2)
    h = x.astype(F32).reshape(M, D)
    mem2 = mem.astype(F32).reshape(B * Mm, D)
    for i in range(depth):
        p2 = norm_matmul(h, ln_mix[i].astype(F32), w_in_packed, i, BF16)
        p3 = p2.reshape(B, L, N_PACK)
        y_a = attention(p3, attn_qnorm[i], attn_knorm[i], rope_tabs)
        f_all = hyena_filters(z_pair, win_pair, hy_w1[i].astype(F32), hy_b1[i].astype(F32),
                              hy_w2[i].astype(F32), hy_b2[i].astype(F32), hy_w3[i].astype(F32),
                              hy_b3[i].astype(F32), hy_freq[i].astype(F32))
        spec = hyena_spectra(fm, f_all)
        y_b = hyena_mixer(p3, hy_conv[i], spec, hy_skip[i], fm, g_inv)
        y_c = gla_mixer(p3, gla_w_lr[i], gla_b_lr[i], gla_onorm[i])
        h = merge_out(y_a.reshape(M, ATT_Q_W), y_b.reshape(M, HY_WIDTH), y_c.reshape(M, GLA_V_W),
                      p2, h, bf(w_br_attn[i]), bf(w_br_hyena[i]), bf(w_br_gla[i]), bf(w_out[i]))
        kv = norm_matmul(mem2, ln_mem[i].astype(F32), wkv_all, i, BF16).reshape(B, Mm, 2 * D)
        h = cross_attention(h.reshape(B, L, D), ln_x[i], bf(x_wq[i]), kv, x_qnorm[i],
                            x_knorm[i], bf(x_wo[i])).reshape(M, D)
        h = mlp(h, ln_mlp[i], bf(mlp_w1[i]), bf(mlp_w2[i]))
    return h.reshape(B, L, D).astype(x.dtype)
```
